```python
import math
import jax, jax.numpy as jnp
from jax import lax
import numpy as np

D_MODEL = 1024
BATCH = 16
SEQ = 2048
DEPTH = 1

N_META = 16
A_HEADS = 4
A_DK = 64
A_DV = 2 * A_DK
B_HEADS = 4
B_DQ = 128
B_DC = 256
B_DV = 128
IDX_HEADS = 8
IDX_DIM = 64
IDX_TOPK_MAX = 256
REL_BUCKETS = 32
REL_MAX_DIST = 128
N_BIAS_HEADS = A_HEADS + B_HEADS
D_FF = 2816
Q_BLOCK = 128
EPS = 1e-6
A_SCALE = A_DK ** -0.5
B_SCALE = B_DC ** -0.5

IN_SIZES = (
    A_HEADS * 2 * A_DK,
    A_HEADS * 2 * A_DK,
    A_HEADS * A_DV,
    B_HEADS * B_DQ,
    B_DC,
    IDX_HEADS * IDX_DIM,
    IDX_DIM,
    IDX_HEADS,
)
N_IN = sum(IN_SIZES)
IN_SPLITS = [int(v) for v in np.cumsum(IN_SIZES)[:-1]]
D_MIX = A_HEADS * A_DV + B_HEADS * B_DV

kernel_name = 'hymba_diffattn_dsa_macaron'


def rmsnorm(x, g):
    xf = x.astype(jnp.float32)
    y = xf * lax.rsqrt(jnp.mean(xf * xf, axis=-1, keepdims=True) + EPS)
    return (y * g.astype(jnp.float32)).astype(x.dtype)


def swiglu(x, w_gate, w_up, w_down):
    return (jax.nn.silu(x @ w_gate) * (x @ w_up)) @ w_down


def rel_bucket(q_pos, k_pos):
    max_exact = REL_BUCKETS // 2
    n = jnp.maximum(q_pos - k_pos, 0)
    nf = jnp.maximum(n, 1).astype(jnp.float32)
    large = max_exact + (jnp.log(nf / max_exact) / math.log(REL_MAX_DIST / max_exact)
                         * (REL_BUCKETS - max_exact)).astype(jnp.int32)
    large = jnp.minimum(large, REL_BUCKETS - 1)
    return jnp.where(n < max_exact, n, large)


def diff_attention(q, k, v, lam, bias_a, subln_g, lam_init):
    bsz, length = q.shape[0], q.shape[1]
    outs = []
    for start in range(0, length, Q_BLOCK):
        stop = min(start + Q_BLOCK, length)
        q_pos = jnp.arange(start, stop)
        k_pos = jnp.arange(stop)
        bias = bias_a[rel_bucket(q_pos[:, None], k_pos[None, :])].transpose(2, 0, 1)
        s = jnp.einsum('bqhmd,bkhmd->bhmqk', q[:, start:stop], k[:, :stop]).astype(jnp.float32)
        s = s * A_SCALE + bias[None, :, None].astype(jnp.float32)
        s = jnp.where(k_pos[None, :] <= q_pos[:, None], s, -jnp.inf)
        p = jax.nn.softmax(s, axis=-1)
        a = p[:, :, 0] - lam * p[:, :, 1]
        outs.append(jnp.einsum('bhqk,bkhd->bqhd', a.astype(v.dtype), v[:, :stop]))
    o = jnp.concatenate(outs, axis=1)
    o = rmsnorm(o, subln_g) * (1.0 - lam_init)
    return o.reshape(bsz, length, A_HEADS * A_DV)


def dsa_attention(q_lat, c, iq, ik, iw, w_uv, bias_b, topk):
    bsz, length = q_lat.shape[0], q_lat.shape[1]
    gather = jax.vmap(lambda cb, ib: cb[ib])
    outs = []
    for start in range(0, length, Q_BLOCK):
        stop = min(start + Q_BLOCK, length)
        q_pos = jnp.arange(start, stop)
        k_pos = jnp.arange(stop)
        dots = jnp.einsum('bqhd,bkd->bqhk', iq[:, start:stop], ik[:, :stop]).astype(jnp.float32)
        w = iw[:, start:stop].astype(jnp.float32) * (IDX_HEADS ** -0.5)
        score = jnp.einsum('bqh,bqhk->bqk', w, jax.nn.relu(dots * (IDX_DIM ** -0.5)))
        score = jnp.where(k_pos[None, None, :] <= q_pos[None, :, None], score, -jnp.inf)
        _, sel = lax.top_k(score, min(topk, stop))
        valid = sel <= q_pos[None, :, None]
        g = gather(c, sel)
        bias = bias_b[rel_bucket(q_pos[None, :, None], sel)].transpose(0, 3, 1, 2)
        s = jnp.einsum('bqhc,bqkc->bhqk', q_lat[:, start:stop], g).astype(jnp.float32)
        s = s * B_SCALE + bias.astype(jnp.float32)
        s = jnp.where(valid[:, None], s, -jnp.inf)
        p = jax.nn.softmax(s, axis=-1)
        o_lat = jnp.einsum('bhqk,bqkc->bqhc', p.astype(g.dtype), g)
        outs.append(jnp.einsum('bqhc,hcd->bqhd', o_lat, w_uv))
    o = jnp.concatenate(outs, axis=1)
    return o.reshape(bsz, length, B_HEADS * B_DV)


def setup_inputs(seed: int = 0) -> dict:
    key = jax.random.key(seed)
    ks = jax.random.split(key, 32)
    f32 = jnp.float32

    def nrm(k, shape, scale):
        return jax.random.normal(k, shape, f32) * scale

    def gain(k, shape):
        return 1.0 + 0.02 * jax.random.normal(k, shape, f32)

    L = DEPTH
    return {
        'x': nrm(ks[0], (BATCH, SEQ, D_MODEL), 1.0),
        'meta_tokens': nrm(ks[1], (N_META, D_MODEL), 1.0),
        'rel_bias': nrm(ks[2], (REL_BUCKETS, N_BIAS_HEADS), 0.1),
        'ffn1_norm': gain(ks[3], (L, D_MODEL)),
        'ffn1_w_gate': nrm(ks[4], (L, D_MODEL, D_FF), D_MODEL ** -0.5),
        'ffn1_w_up': nrm(ks[5], (L, D_MODEL, D_FF), D_MODEL ** -0.5),
        'ffn1_w_down': nrm(ks[6], (L, D_FF, D_MODEL), D_FF ** -0.5),
        'mix_norm': gain(ks[7], (L, D_MODEL)),
        'w_in': nrm(ks[8], (L, D_MODEL, N_IN), D_MODEL ** -0.5),
        'a_q_norm': gain(ks[9], (L, A_DK)),
        'a_k_norm': gain(ks[10], (L, A_DK)),
        'a_lambda_q1': nrm(ks[11], (L, A_DK), 0.1),
        'a_lambda_k1': nrm(ks[12], (L, A_DK), 0.1),
        'a_lambda_q2': nrm(ks[13], (L, A_DK), 0.1),
        'a_lambda_k2': nrm(ks[14], (L, A_DK), 0.1),
        'a_subln': gain(ks[15], (L, A_DV)),
        'b_kv_norm': gain(ks[16], (L, B_DC)),
        'b_w_uk': nrm(ks[17], (L, B_HEADS, B_DQ, B_DC), B_DQ ** -0.5),
        'b_q_norm': gain(ks[18], (L, B_DC)),
        'b_w_uv': nrm(ks[19], (L, B_HEADS, B_DC, B_DV), B_DC ** -0.5),
        'w_out': nrm(ks[20], (L, D_MIX, D_MODEL), D_MIX ** -0.5),
        'ffn2_norm': gain(ks[21], (L, D_MODEL)),
        'ffn2_w_gate': nrm(ks[22], (L, D_MODEL, D_FF), D_MODEL ** -0.5),
        'ffn2_w_up': nrm(ks[23], (L, D_MODEL, D_FF), D_MODEL ** -0.5),
        'ffn2_w_down': nrm(ks[24], (L, D_FF, D_MODEL), D_FF ** -0.5),
    }


def reference(x, meta_tokens, rel_bias, ffn1_norm, ffn1_w_gate, ffn1_w_up, ffn1_w_down,
              mix_norm, w_in, a_q_norm, a_k_norm, a_lambda_q1, a_lambda_k1, a_lambda_q2,
              a_lambda_k2, a_subln, b_kv_norm, b_w_uk, b_q_norm, b_w_uv, w_out,
              ffn2_norm, ffn2_w_gate, ffn2_w_up, ffn2_w_down):
    bsz, seq = x.shape[0], x.shape[1]
    topk = min(IDX_TOPK_MAX, seq // 4)
    meta = jnp.broadcast_to(meta_tokens.astype(x.dtype)[None], (bsz, N_META, D_MODEL))
    h = jnp.concatenate([meta, x], axis=1)
    length = h.shape[1]
    bias_a = rel_bias[:, :A_HEADS]
    bias_b = rel_bias[:, A_HEADS:]

    for l in range(DEPTH):
        h = h + 0.5 * swiglu(rmsnorm(h, ffn1_norm[l]), ffn1_w_gate[l], ffn1_w_up[l], ffn1_w_down[l])

        u = rmsnorm(h, mix_norm[l]) @ w_in[l]
        qa, ka, va, qb, ckv, iq, ik, iw = jnp.split(u, IN_SPLITS, axis=-1)

        qa = rmsnorm(qa.reshape(bsz, length, A_HEADS, 2, A_DK), a_q_norm[l])
        ka = rmsnorm(ka.reshape(bsz, length, A_HEADS, 2, A_DK), a_k_norm[l])
        va = va.reshape(bsz, length, A_HEADS, A_DV)
        lam_init = 0.8 - 0.6 * math.exp(-0.3 * l)
        lam = (jnp.exp(jnp.sum(a_lambda_q1[l].astype(jnp.float32) * a_lambda_k1[l].astype(jnp.float32)))
               - jnp.exp(jnp.sum(a_lambda_q2[l].astype(jnp.float32) * a_lambda_k2[l].astype(jnp.float32)))
               + lam_init)
        o_a = diff_attention(qa, ka, va, lam, bias_a, a_subln[l], lam_init)

        qb = qb.reshape(bsz, length, B_HEADS, B_DQ)
        q_lat = rmsnorm(jnp.einsum('blhd,hdc->blhc', qb, b_w_uk[l]), b_q_norm[l])
        c = rmsnorm(ckv, b_kv_norm[l])
        iq = iq.reshape(bsz, length, IDX_HEADS, IDX_DIM)
        o_b = dsa_attention(q_lat, c, iq, ik, iw, b_w_uv[l], bias_b, topk)

        h = h + jnp.concatenate([o_a, o_b], axis=-1) @ w_out[l]

        h = h + 0.5 * swiglu(rmsnorm(h, ffn2_norm[l]), ffn2_w_gate[l], ffn2_w_up[l], ffn2_w_down[l])

    return h[:, N_META:]
```

```python
import functools
import math

import jax
import jax.numpy as jnp
import numpy as np
from jax import lax
from jax.experimental import pallas as pl
from jax.experimental.pallas import tpu as pltpu

F32 = jnp.float32
BF16 = jnp.bfloat16

D_MODEL = 1024
N_META = 16
A_HEADS = 4
A_DK = 64
A_DV = 128
B_HEADS = 4
B_DQ = 128
B_DC = 256
B_DV = 128
IDX_HEADS = 8
IDX_DIM = 64
IDX_TOPK_MAX = 256
REL_BUCKETS = 32
REL_MAX_DIST = 128
D_FF = 2816
EPS = 1e-6
A_SCALE = A_DK ** -0.5
B_SCALE = B_DC ** -0.5
LAM_INIT = 0.8 - 0.6 * math.exp(-0.3 * 0)

NEG = -1e30
INT_MIN = -(2 ** 31)

TQ = 256
META_PAD = 128
FF_CHUNK = 1408
FFN_TM = 512
PROJ_TM = 512
V7X_VMEM_LIMIT = 56 * 1024 * 1024

_T_QA, _T_VA, _T_QB, _T_C, _T_IQ, _T_IW, _T_END = 0, 512, 1024, 1536, 1792, 2304, 2320
_R_KA, _R_C, _R_IK, _R_END = 0, 512, 768, 896


def _bucket_edges():
    max_exact = REL_BUCKETS // 2
    n = np.arange(0, 4 * REL_MAX_DIST)
    nf = np.maximum(n, 1).astype(np.float64)
    large = max_exact + (np.log(nf / max_exact) / math.log(REL_MAX_DIST / max_exact)
                         * (REL_BUCKETS - max_exact)).astype(np.int64)
    large = np.minimum(large, REL_BUCKETS - 1)
    bucket = np.where(n < max_exact, n, large)
    return [int(n[np.argmax(bucket >= j)]) for j in range(REL_BUCKETS)]


_EDGES = _bucket_edges()
assert _EDGES[-1] <= TQ, "tiles two or more away from the diagonal must sit in the last bucket"


def _const_spec(shape):
    nd = len(shape)
    return pl.BlockSpec(shape, lambda *_: (0,) * nd, pipeline_mode=pl.Buffered(1))


def _params(n_grid):
    return pltpu.CompilerParams(dimension_semantics=("arbitrary",) * n_grid,
                                vmem_limit_bytes=V7X_VMEM_LIMIT)


def _rms_lanes(x, g_row):
    ms = jnp.mean(x * x, axis=-1, keepdims=True)
    return x * lax.rsqrt(ms + EPS) * g_row


def _bias_kernel(rb_ref, tall_ref, tmall_ref):
    h = pl.program_id(0)

    def table(d):
        val = jnp.full(d.shape, rb_ref[0, h], F32)
        for j in range(1, REL_BUCKETS):
            val = jnp.where(d >= _EDGES[j], rb_ref[j, h], val)
        return val

    far = rb_ref[REL_BUCKETS - 1, h]
    c = lax.broadcasted_iota(jnp.int32, (TQ, TQ), 0)
    r = lax.broadcasted_iota(jnp.int32, (TQ, TQ), 1)
    d0 = r - c
    tall_ref[0, 0] = jnp.where(d0 >= 0, table(d0), NEG)
    tall_ref[0, 1] = table(d0 + TQ)
    tall_ref[0, 2] = jnp.full((TQ, TQ), far, F32)
    m = lax.broadcasted_iota(jnp.int32, (META_PAD, TQ), 0)
    r2 = lax.broadcasted_iota(jnp.int32, (META_PAD, TQ), 1)
    tmall_ref[0, 0] = jnp.where(m < N_META, table(r2 + N_META - m), NEG)
    tmall_ref[0, 1] = jnp.where(m < N_META, far, NEG)


def _bias_tiles(rel_bias):
    nh = rel_bias.shape[1]
    return pl.pallas_call(
        _bias_kernel,
        grid=(nh,),
        in_specs=[pl.BlockSpec(memory_space=pltpu.SMEM)],
        out_specs=[pl.BlockSpec((1, 3, TQ, TQ), lambda h: (h, 0, 0, 0)),
                   pl.BlockSpec((1, 2, META_PAD, TQ), lambda h: (h, 0, 0, 0))],
        out_shape=[jax.ShapeDtypeStruct((nh, 3, TQ, TQ), F32),
                   jax.ShapeDtypeStruct((nh, 2, META_PAD, TQ), F32)],
        compiler_params=_params(1),
        name="bias_tiles",
    )(rel_bias)


def _ffn_kernel(x_ref, g_ref, wg_ref, wu_ref, wd_ref, o_ref):
    x = x_ref[...]
    xn = _rms_lanes(x, g_ref[...]).astype(BF16)
    y = None
    for c in range(D_FF // FF_CHUNK):
        sl = slice(c * FF_CHUNK, (c + 1) * FF_CHUNK)
        g = jnp.dot(xn, wg_ref[:, sl], preferred_element_type=F32)
        u = jnp.dot(xn, wu_ref[:, sl], preferred_element_type=F32)
        a = (g * jax.nn.sigmoid(g) * u).astype(BF16)
        part = jnp.dot(a, wd_ref[sl, :], preferred_element_type=F32)
        y = part if y is None else y + part
    o_ref[...] = x + 0.5 * y


def _ffn(h2d, g_row, wg, wu, wd, tm):
    m, d = h2d.shape
    return pl.pallas_call(
        _ffn_kernel,
        grid=(m // tm,),
        in_specs=[pl.BlockSpec((tm, d), lambda i: (i, 0)),
                  _const_spec((1, d)),
                  _const_spec(wg.shape), _const_spec(wu.shape), _const_spec(wd.shape)],
        out_specs=pl.BlockSpec((tm, d), lambda i: (i, 0)),
        out_shape=jax.ShapeDtypeStruct((m, d), F32),
        compiler_params=_params(1),
        name="ffn",
    )(h2d, g_row, wg, wu, wd)


def _inproj_kernel(h_ref, g_ref, wt_ref, wr_ref, wukt_ref, gq_ref, gk_ref, gql_ref, gcr_ref, gcc_ref,
                   qat_ref, ka_ref, vat_ref, qlt_ref, c_ref, ct_ref, iqt_ref, ik_ref, iwt_ref, *, tm, tw):
    xn = _rms_lanes(h_ref[0], g_ref[...]).astype(BF16)
    ut = lax.dot_general(wt_ref[...], xn, (((1,), (1,)), ((), ())),
                         preferred_element_type=F32)
    ur = jnp.dot(xn, wr_ref[...], preferred_element_type=F32)

    qa = ut[_T_QA:_T_VA].reshape(2 * A_HEADS, A_DK, tm)
    qa = qa * lax.rsqrt(jnp.mean(qa * qa, axis=1, keepdims=True) + EPS)
    qat_ref[0] = (qa.reshape(2 * A_HEADS * A_DK, tm) * gq_ref[...]).astype(BF16)

    for t in range(tm // tw):
        vat_ref[0, t] = ut[_T_VA:_T_QB, t * tw:(t + 1) * tw].astype(BF16)

    qb = ut[_T_QB:_T_C].astype(BF16)
    for h in range(B_HEADS):
        ql = jnp.dot(wukt_ref[h], qb[h * B_DQ:(h + 1) * B_DQ], preferred_element_type=F32)
        ql = ql * lax.rsqrt(jnp.mean(ql * ql, axis=0, keepdims=True) + EPS) * gql_ref[...]
        qlt_ref[0, h * B_DC:(h + 1) * B_DC, :] = ql.astype(BF16)

    ct = ut[_T_C:_T_IQ]
    ct = (ct * lax.rsqrt(jnp.mean(ct * ct, axis=0, keepdims=True) + EPS) * gcc_ref[...]).astype(BF16)
    for t in range(tm // tw):
        ct_ref[0, t] = ct[:, t * tw:(t + 1) * tw]

    iqt_ref[0] = ut[_T_IQ:_T_IW].astype(BF16)
    iwt_ref[0] = ut[_T_IW:_T_IW + IDX_HEADS] * (IDX_HEADS ** -0.5 * IDX_DIM ** -0.5)

    for hm in range(2 * A_HEADS):
        k = ur[:, _R_KA + hm * A_DK:_R_KA + (hm + 1) * A_DK]
        ka_ref[0, hm] = _rms_lanes(k, gk_ref[...]).astype(BF16)
    c_ref[0] = _rms_lanes(ur[:, _R_C:_R_IK], gcr_ref[...]).astype(BF16)
    ik_ref[0] = ur[:, _R_IK:_R_IK + IDX_DIM].astype(BF16)


def _inproj(h3d, g_row, wt, wr, wukt, gq, gk, gql, gcr, gcc, tm, tw):
    b, s, d = h3d.shape
    nt = s // tw
    kern = functools.partial(_inproj_kernel, tm=tm, tw=tw)
    out_shape = [
        jax.ShapeDtypeStruct((b, 2 * A_HEADS * A_DK, s), BF16),
        jax.ShapeDtypeStruct((b, 2 * A_HEADS, s, A_DK), BF16),
        jax.ShapeDtypeStruct((b, nt, A_HEADS * A_DV, tw), BF16),
        jax.ShapeDtypeStruct((b, B_HEADS * B_DC, s), BF16),
        jax.ShapeDtypeStruct((b, s, B_DC), BF16),
        jax.ShapeDtypeStruct((b, nt, B_DC, tw), BF16),
        jax.ShapeDtypeStruct((b, IDX_HEADS * IDX_DIM, s), BF16),
        jax.ShapeDtypeStruct((b, s, IDX_DIM), BF16),
        jax.ShapeDtypeStruct((b, IDX_HEADS, s), F32),
    ]
    out_specs = [
        pl.BlockSpec((1, 2 * A_HEADS * A_DK, tm), lambda bi, i: (bi, 0, i)),
        pl.BlockSpec((1, 2 * A_HEADS, tm, A_DK), lambda bi, i: (bi, 0, i, 0)),
        pl.BlockSpec((1, tm // tw, A_HEADS * A_DV, tw), lambda bi, i: (bi, i, 0, 0)),
        pl.BlockSpec((1, B_HEADS * B_DC, tm), lambda bi, i: (bi, 0, i)),
        pl.BlockSpec((1, tm, B_DC), lambda bi, i: (bi, i, 0)),
        pl.BlockSpec((1, tm // tw, B_DC, tw), lambda bi, i: (bi, i, 0, 0)),
        pl.BlockSpec((1, IDX_HEADS * IDX_DIM, tm), lambda bi, i: (bi, 0, i)),
        pl.BlockSpec((1, tm, IDX_DIM), lambda bi, i: (bi, i, 0)),
        pl.BlockSpec((1, IDX_HEADS, tm), lambda bi, i: (bi, 0, i)),
    ]
    consts = (g_row, wt, wr, wukt, gq, gk, gql, gcr, gcc)
    return pl.pallas_call(
        kern,
        grid=(b, s // tm),
        in_specs=[pl.BlockSpec((1, tm, d), lambda bi, i: (bi, i, 0))] + [_const_spec(a.shape) for a in consts],
        out_specs=out_specs,
        out_shape=out_shape,
        compiler_params=_params(2),
        name="inproj",
    )(h3d, *consts)


def _key_to_f32(u):
    bits = u ^ (jnp.right_shift(u, 31) & 0x7FFFFFFF)
    return lax.bitcast_convert_type(bits, F32)


def _online(s, m_old, l_old, acc_old, v_lhs):
    m_new = jnp.maximum(m_old, jnp.max(s, axis=0, keepdims=True))
    alpha = jnp.exp(m_old - m_new)
    p = jnp.exp(s - m_new)
    l_new = alpha * l_old + jnp.sum(p, axis=0, keepdims=True)
    acc_new = alpha * acc_old + jnp.dot(v_lhs, p.astype(BF16), preferred_element_type=F32)
    return m_new, l_new, acc_new


def _attn_kernel(qat_ref, qlt_ref, iqt_ref, iwt_ref, h_ref,
                 ka_ref, vat_ref, c_ref, ct_ref, ik_ref,
                 kam_ref, vatm_ref, cm_ref, ctm_ref, ikm_ref,
                 tall_ref, tmall_ref, lamp_ref, subln_ref, wuvt_ref, wout_ref,
                 o_ref,
                 st_ref, stm_ref, m_ref, l_ref, acc_ref, m2_ref, l2_ref, acc2_ref, ot_ref):
    i = pl.program_id(1)
    n_tiles = i + 1
    midx = jnp.minimum(i, 1)

    def tile_rows(j):
        return pl.ds(pl.multiple_of(j * TQ, TQ), TQ)

    m_ref[...] = jnp.full(m_ref.shape, NEG, F32)
    l_ref[...] = jnp.zeros(l_ref.shape, F32)
    acc_ref[...] = jnp.zeros(acc_ref.shape, F32)
    m2_ref[...] = jnp.full(m2_ref.shape, NEG, F32)
    l2_ref[...] = jnp.zeros(l2_ref.shape, F32)
    acc2_ref[...] = jnp.zeros(acc2_ref.shape, F32)

    def a_tile(k_get, v_get, bias_get):
        for hm in range(2 * A_HEADS):
            h = hm // 2
            qt = qat_ref[0, hm * A_DK:(hm + 1) * A_DK, :]
            s = jnp.dot(k_get(hm), qt, preferred_element_type=F32) + bias_get(h)
            m_new, l_new, acc_new = _online(s, m_ref[hm:hm + 1, :], l_ref[hm:hm + 1, :],
                                            acc_ref[hm], v_get(h))
            m_ref[hm:hm + 1, :] = m_new
            l_ref[hm:hm + 1, :] = l_new
            acc_ref[hm] = acc_new

    a_tile(lambda hm: kam_ref[0, hm],
           lambda h: vatm_ref[0, 0, h * A_DV:(h + 1) * A_DV, :],
           lambda h: tmall_ref[h, midx])

    def a_body(j, carry):
        kidx = jnp.minimum(i - j, 2)
        a_tile(lambda hm: ka_ref[0, hm, tile_rows(j), :],
               lambda h: vat_ref[0, j, h * A_DV:(h + 1) * A_DV, :],
               lambda h: tall_ref[h, kidx])
        return carry

    lax.fori_loop(0, n_tiles, a_body, 0)

    lp = lamp_ref[...]
    lam = (jnp.exp(jnp.sum(lp[0:1] * lp[1:2], axis=-1, keepdims=True))
           - jnp.exp(jnp.sum(lp[2:3] * lp[3:4], axis=-1, keepdims=True)) + LAM_INIT)
    for h in range(A_HEADS):
        o0 = acc_ref[2 * h] / l_ref[2 * h:2 * h + 1, :]
        o1 = acc_ref[2 * h + 1] / l_ref[2 * h + 1:2 * h + 2, :]
        o = o0 - lam * o1
        o = o * lax.rsqrt(jnp.mean(o * o, axis=0, keepdims=True) + EPS) * subln_ref[...]
        ot_ref[h * A_DV:(h + 1) * A_DV, :] = o.astype(BF16)

    qrel = i * TQ + lax.broadcasted_iota(jnp.int32, (1, TQ), 1)

    def idx_scores(ik_tile):
        acc = jnp.zeros((ik_tile.shape[0], TQ), F32)
        for h in range(IDX_HEADS):
            d = jnp.dot(ik_tile, iqt_ref[0, h * IDX_DIM:(h + 1) * IDX_DIM, :], preferred_element_type=F32)
            acc = acc + iwt_ref[0, h:h + 1, :] * jnp.maximum(d, 0.0)
        return acc

    mrow = lax.broadcasted_iota(jnp.int32, (META_PAD, TQ), 0)
    trow = lax.broadcasted_iota(jnp.int32, (TQ, TQ), 0)
    stm_ref[...] = jnp.where(mrow < N_META, idx_scores(ikm_ref[0]), -jnp.inf)

    def b_body(j, carry):
        sc = idx_scores(ik_ref[0, tile_rows(j), :])
        st_ref[j] = jnp.where(j * TQ + trow <= qrel, sc, -jnp.inf)
        return carry

    lax.fori_loop(0, n_tiles, b_body, 0)

    kf = jnp.minimum(IDX_TOPK_MAX, qrel + (N_META + 1)).astype(F32)

    def count(ind_meta, ind_tile):
        p = ind_meta(stm_ref[...]).reshape(META_PAD // 8, 8, TQ).sum(axis=0)
        p = lax.fori_loop(0, n_tiles,
                          lambda j, p: p + ind_tile(st_ref[j], j).reshape(TQ // 8, 8, TQ).sum(axis=0), p)
        return jnp.sum(p, axis=0, keepdims=True)

    def count_ge(t):
        ind = lambda x: jnp.where(x >= t, 1.0, 0.0)
        return count(ind, lambda x, j: ind(x))

    def count_gt(t):
        ind = lambda x: jnp.where(x > t, 1.0, 0.0)
        return count(ind, lambda x, j: ind(x))

    def bit_body(b, u):
        cand = u + jnp.left_shift(jnp.int32(1), 31 - b)
        return jnp.where(count_ge(_key_to_f32(cand)) >= kf, cand, u)

    u = lax.fori_loop(0, 32, bit_body, jnp.full((1, TQ), INT_MIN, jnp.int32))
    thr = _key_to_f32(u)

    need = kf - count_gt(thr)
    overflow = jnp.max(jnp.where(count_ge(thr) > kf, 1.0, 0.0)) > 0.0

    @pl.when(overflow)
    def _():
        def count_ties_before(xc):
            return count(lambda x: jnp.where((x == thr) & (mrow < xc), 1.0, 0.0),
                         lambda x, j: jnp.where((x == thr) & (j * TQ + N_META + trow < xc), 1.0, 0.0))

        def pos_body(b, xs):
            cand = xs | jnp.left_shift(jnp.int32(1), 11 - b)
            return jnp.where(count_ties_before(cand) < need, cand, xs)

        last = lax.fori_loop(0, 12, pos_body, jnp.zeros((1, TQ), jnp.int32))
        sm = stm_ref[...]
        stm_ref[...] = jnp.where((sm == thr) & (mrow > last), -jnp.inf, sm)

        def drop_body(j, carry):
            sx = st_ref[j]
            st_ref[j] = jnp.where((sx == thr) & (j * TQ + N_META + trow > last), -jnp.inf, sx)
            return carry

        lax.fori_loop(0, n_tiles, drop_body, 0)

    def d_tile(st_tile, c_tile, ct_tile, bias_get):
        sel = jnp.where(st_tile >= thr, 0.0, NEG)
        for h in range(B_HEADS):
            s = (jnp.dot(c_tile, qlt_ref[0, h * B_DC:(h + 1) * B_DC, :], preferred_element_type=F32)
                 + bias_get(A_HEADS + h) + sel)
            m_new, l_new, acc_new = _online(s, m2_ref[h:h + 1, :], l2_ref[h:h + 1, :], acc2_ref[h], ct_tile)
            m2_ref[h:h + 1, :] = m_new
            l2_ref[h:h + 1, :] = l_new
            acc2_ref[h] = acc_new

    d_tile(stm_ref[...], cm_ref[0], ctm_ref[0, 0], lambda h: tmall_ref[h, midx])

    def d_body(j, carry):
        kidx = jnp.minimum(i - j, 2)
        d_tile(st_ref[j], c_ref[0, tile_rows(j), :], ct_ref[0, j], lambda h: tall_ref[h, kidx])
        return carry

    lax.fori_loop(0, n_tiles, d_body, 0)

    for h in range(B_HEADS):
        olat = (acc2_ref[h] / l2_ref[h:h + 1, :]).astype(BF16)
        ob = jnp.dot(wuvt_ref[h], olat, preferred_element_type=F32)
        r0 = A_HEADS * A_DV + h * B_DV
        ot_ref[r0:r0 + B_DV, :] = ob.astype(BF16)

    y = lax.dot_general(ot_ref[...], wout_ref[...], (((0,), (0,)), ((), ())),
                        preferred_element_type=F32)
    o_ref[0] = h_ref[0] + y


def _attention(q_side, h3d, k_side, meta_side, tall, tmall, lamp, subln, wuvt, wout):
    qat, qlt, iqt, iwt = q_side
    ka, vat, c, ct, ik = k_side
    kam, vatm, cm, ctm, ikm = meta_side
    b, s, d = h3d.shape
    nt = s // TQ
    qspec = lambda rows: pl.BlockSpec((1, rows, TQ), lambda bi, i: (bi, 0, i))
    in_specs = [
        qspec(qat.shape[1]), qspec(qlt.shape[1]), qspec(iqt.shape[1]), qspec(iwt.shape[1]),
        pl.BlockSpec((1, TQ, d), lambda bi, i: (bi, i, 0)),
        pl.BlockSpec((1,) + ka.shape[1:], lambda bi, i: (bi, 0, 0, 0)),
        pl.BlockSpec((1,) + vat.shape[1:], lambda bi, i: (bi, 0, 0, 0)),
        pl.BlockSpec((1,) + c.shape[1:], lambda bi, i: (bi, 0, 0)),
        pl.BlockSpec((1,) + ct.shape[1:], lambda bi, i: (bi, 0, 0, 0)),
        pl.BlockSpec((1,) + ik.shape[1:], lambda bi, i: (bi, 0, 0)),
    ] + [_const_spec(a.shape) for a in (kam, vatm, cm, ctm, ikm, tall, tmall, lamp, subln, wuvt, wout)]
    scratch = [
        pltpu.VMEM((nt, TQ, TQ), F32),
        pltpu.VMEM((META_PAD, TQ), F32),
        pltpu.VMEM((2 * A_HEADS, TQ), F32),
        pltpu.VMEM((2 * A_HEADS, TQ), F32),
        pltpu.VMEM((2 * A_HEADS, A_DV, TQ), F32),
        pltpu.VMEM((B_HEADS, TQ), F32),
        pltpu.VMEM((B_HEADS, TQ), F32),
        pltpu.VMEM((B_HEADS, B_DC, TQ), F32),
        pltpu.VMEM((A_HEADS * A_DV + B_HEADS * B_DV, TQ), BF16),
    ]
    return pl.pallas_call(
        _attn_kernel,
        grid=(b, nt),
        in_specs=in_specs,
        out_specs=pl.BlockSpec((1, TQ, d), lambda bi, i: (bi, i, 0)),
        out_shape=jax.ShapeDtypeStruct((b, s, d), F32),
        scratch_shapes=scratch,
        compiler_params=_params(2),
        name="attention",
    )(qat, qlt, iqt, iwt, h3d, ka, vat, c, ct, ik, kam, vatm, cm, ctm, ikm,
      tall, tmall, lamp, subln, wuvt, wout)


def kernel(x, meta_tokens, rel_bias, ffn1_norm, ffn1_w_gate, ffn1_w_up, ffn1_w_down, mix_norm, w_in, a_q_norm, a_k_norm, a_lambda_q1, a_lambda_k1, a_lambda_q2, a_lambda_k2, a_subln, b_kv_norm, b_w_uk, b_q_norm, b_w_uv, w_out, ffn2_norm, ffn2_w_gate, ffn2_w_up, ffn2_w_down):
    bsz, seq, d = x.shape
    assert d == D_MODEL and seq % PROJ_TM == 0 and (bsz * seq) % FFN_TM == 0
    assert min(IDX_TOPK_MAX, (seq + N_META) // 4) == IDX_TOPK_MAX
    assert ffn1_norm.shape[0] == 1, "single layer"
    lyr = 0
    row = lambda v: v.reshape(1, -1).astype(F32)
    col = lambda v: v.reshape(-1, 1).astype(F32)

    w = w_in[lyr]
    o_qa, o_ka, o_va, o_qb, o_c, o_iq, o_ik, o_iw = np.cumsum(
        [0, 2 * A_HEADS * A_DK, 2 * A_HEADS * A_DK, A_HEADS * A_DV, B_HEADS * B_DQ, B_DC,
         IDX_HEADS * IDX_DIM, IDX_DIM])[:8]
    w_qa, w_ka, w_va, w_qb = w[:, o_qa:o_ka], w[:, o_ka:o_va], w[:, o_va:o_qb], w[:, o_qb:o_c]
    w_c, w_iq, w_ik, w_iw = w[:, o_c:o_iq], w[:, o_iq:o_ik], w[:, o_ik:o_iw], w[:, o_iw:o_iw + IDX_HEADS]
    wt = jnp.concatenate([w_qa, w_va, w_qb, w_c, w_iq, w_iw,
                          jnp.zeros((d, _T_END - _T_IW - IDX_HEADS), w.dtype)], axis=1).T.astype(BF16)
    wr = jnp.concatenate([w_ka, w_c, w_ik, jnp.zeros((d, _R_END - _R_IK - IDX_DIM), w.dtype)],
                         axis=1).astype(BF16)
    wukt = jnp.transpose(b_w_uk[lyr], (0, 2, 1)).astype(BF16)
    wuvt = jnp.transpose(b_w_uv[lyr], (0, 2, 1)).astype(BF16)
    gq = col(jnp.tile(a_q_norm[lyr], 2 * A_HEADS)) * A_SCALE
    gk = row(a_k_norm[lyr])
    gql = col(b_q_norm[lyr]) * B_SCALE
    gcr, gcc = row(b_kv_norm[lyr]), col(b_kv_norm[lyr])
    subln = col(a_subln[lyr]) * (1.0 - LAM_INIT)
    lamp = jnp.stack([a_lambda_q1[lyr], a_lambda_k1[lyr], a_lambda_q2[lyr], a_lambda_k2[lyr]]).astype(F32)
    proj_consts = (row(mix_norm[lyr]), wt, wr, wukt, gq, gk, gql, gcr, gcc)

    tall, tmall = _bias_tiles(rel_bias.astype(F32))

    ffn1 = (row(ffn1_norm[lyr]), ffn1_w_gate[lyr].astype(BF16), ffn1_w_up[lyr].astype(BF16),
            ffn1_w_down[lyr].astype(BF16))
    h1 = _ffn(x.reshape(bsz * seq, d), *ffn1, tm=FFN_TM).reshape(bsz, seq, d)
    meta_pad = jnp.pad(meta_tokens.astype(x.dtype), ((0, META_PAD - N_META), (0, 0)))
    h1m = _ffn(meta_pad, *ffn1, tm=META_PAD).reshape(1, META_PAD, d)

    qat, ka, vat, qlt, c, ct, iqt, ik, iwt = _inproj(h1, *proj_consts, tm=PROJ_TM, tw=TQ)
    _, kam, vatm, _, cm, ctm, _, ikm, _ = _inproj(h1m, *proj_consts, tm=META_PAD, tw=META_PAD)

    h2 = _attention((qat, qlt, iqt, iwt), h1, (ka, vat, c, ct, ik), (kam, vatm, cm, ctm, ikm),
                    tall, tmall, lamp, subln, wuvt, w_out[lyr].astype(BF16))

    ffn2 = (row(ffn2_norm[lyr]), ffn2_w_gate[lyr].astype(BF16), ffn2_w_up[lyr].astype(BF16),
            ffn2_w_down[lyr].astype(BF16))
    return _ffn(h2.reshape(bsz * seq, d), *ffn2, tm=FFN_TM).reshape(bsz, seq, d)
```

```python
import functools
import math

import jax
import jax.numpy as jnp
import numpy as np
from jax import lax
from jax.experimental import pallas as pl
from jax.experimental.pallas import tpu as pltpu

F32 = jnp.float32
BF16 = jnp.bfloat16

D_MODEL = 1024
N_META = 16
A_HEADS = 4
A_DK = 64
A_DV = 128
B_HEADS = 4
B_DQ = 128
B_DC = 256
B_DV = 128
IDX_HEADS = 8
IDX_DIM = 64
IDX_TOPK_MAX = 256
REL_BUCKETS = 32
REL_MAX_DIST = 128
D_FF = 2816
EPS = 1e-6
LOG2E = math.log2(math.e)
A_SCALE = A_DK ** -0.5
B_SCALE = B_DC ** -0.5
LAM_INIT = 0.8 - 0.6 * math.exp(-0.3 * 0)

NEG = -1e30
INT_MIN = -(2 ** 31)

TQ = 256
META_PAD = 128
FF_CHUNK = 1408
FFN_TM = 512
PROJ_TM = 512
V7X_VMEM_LIMIT = 56 * 1024 * 1024

A_LANES = 2 * TQ
_T_QA, _T_VA, _T_QB, _T_C, _T_IQ, _T_IW, _T_END = 0, 512, 1024, 1536, 1792, 2304, 2320
_R_KA, _R_C, _R_IK, _R_END = 0, 512, 768, 896


def _bucket_edges():
    max_exact = REL_BUCKETS // 2
    n = np.arange(0, 4 * REL_MAX_DIST)
    nf = np.maximum(n, 1).astype(np.float64)
    large = max_exact + (np.log(nf / max_exact) / math.log(REL_MAX_DIST / max_exact)
                         * (REL_BUCKETS - max_exact)).astype(np.int64)
    large = np.minimum(large, REL_BUCKETS - 1)
    bucket = np.where(n < max_exact, n, large)
    return [int(n[np.argmax(bucket >= j)]) for j in range(REL_BUCKETS)]


_EDGES = _bucket_edges()
assert _EDGES[-1] <= TQ, "tiles two or more away from the diagonal must sit in the last bucket"


def _const_spec(shape):
    nd = len(shape)
    return pl.BlockSpec(shape, lambda *_: (0,) * nd, pipeline_mode=pl.Buffered(1))


def _params(n_grid):
    return pltpu.CompilerParams(dimension_semantics=("arbitrary",) * n_grid,
                                vmem_limit_bytes=V7X_VMEM_LIMIT)


def _rms_lanes(x, g_row):
    ms = jnp.mean(x * x, axis=-1, keepdims=True)
    return x * lax.rsqrt(ms + EPS) * g_row


def _bias_kernel(rb_ref, tall_ref, tmall_ref):
    h = pl.program_id(0)

    def table(d):
        val = jnp.full(d.shape, rb_ref[0, h] * LOG2E, F32)
        for j in range(1, REL_BUCKETS):
            val = jnp.where(d >= _EDGES[j], rb_ref[j, h] * LOG2E, val)
        return val

    far = rb_ref[REL_BUCKETS - 1, h] * LOG2E
    c = lax.broadcasted_iota(jnp.int32, (TQ, TQ), 0)
    r = lax.broadcasted_iota(jnp.int32, (TQ, TQ), 1)
    d0 = r - c
    tall_ref[0, 0] = jnp.where(d0 >= 0, table(d0), NEG)
    tall_ref[0, 1] = table(d0 + TQ)
    tall_ref[0, 2] = jnp.full((TQ, TQ), far, F32)
    m = lax.broadcasted_iota(jnp.int32, (META_PAD, TQ), 0)
    r2 = lax.broadcasted_iota(jnp.int32, (META_PAD, TQ), 1)
    tmall_ref[0, 0] = jnp.where(m < N_META, table(r2 + N_META - m), NEG)
    tmall_ref[0, 1] = jnp.where(m < N_META, far, NEG)


def _bias_tiles(rel_bias):
    nh = rel_bias.shape[1]
    return pl.pallas_call(
        _bias_kernel,
        grid=(nh,),
        in_specs=[pl.BlockSpec(memory_space=pltpu.SMEM)],
        out_specs=[pl.BlockSpec((1, 3, TQ, TQ), lambda h: (h, 0, 0, 0)),
                   pl.BlockSpec((1, 2, META_PAD, TQ), lambda h: (h, 0, 0, 0))],
        out_shape=[jax.ShapeDtypeStruct((nh, 3, TQ, TQ), F32),
                   jax.ShapeDtypeStruct((nh, 2, META_PAD, TQ), F32)],
        compiler_params=_params(1),
        name="bias_tiles",
    )(rel_bias)


def _ffn_kernel(x_ref, g_ref, wg_ref, wu_ref, wd_ref, o_ref):
    x = x_ref[...]
    xn = _rms_lanes(x, g_ref[...]).astype(BF16)
    y = None
    for c in range(D_FF // FF_CHUNK):
        sl = slice(c * FF_CHUNK, (c + 1) * FF_CHUNK)
        g = jnp.dot(xn, wg_ref[:, sl], preferred_element_type=F32)
        u = jnp.dot(xn, wu_ref[:, sl], preferred_element_type=F32)
        a = (g * jax.nn.sigmoid(g) * u).astype(BF16)
        part = jnp.dot(a, wd_ref[sl, :], preferred_element_type=F32)
        y = part if y is None else y + part
    o_ref[...] = x + 0.5 * y


def _ffn(h2d, g_row, wg, wu, wd, tm):
    m, d = h2d.shape
    return pl.pallas_call(
        _ffn_kernel,
        grid=(m // tm,),
        in_specs=[pl.BlockSpec((tm, d), lambda i: (i, 0)),
                  _const_spec((1, d)),
                  _const_spec(wg.shape), _const_spec(wu.shape), _const_spec(wd.shape)],
        out_specs=pl.BlockSpec((tm, d), lambda i: (i, 0)),
        out_shape=jax.ShapeDtypeStruct((m, d), F32),
        compiler_params=_params(1),
        name="ffn",
    )(h2d, g_row, wg, wu, wd)


def _inproj_kernel(h_ref, g_ref, wt_ref, wr_ref, wukt_ref, gq_ref, gk_ref, gql_ref, gcr_ref, gcc_ref,
                   qat_ref, ka_ref, vat_ref, qlt_ref, c_ref, ct_ref, iqt_ref, ik_ref, iwt_ref, *, tm, tw):
    xn = _rms_lanes(h_ref[0], g_ref[...]).astype(BF16)
    ut = lax.dot_general(wt_ref[...], xn, (((1,), (1,)), ((), ())),
                         preferred_element_type=F32)
    ur = jnp.dot(xn, wr_ref[...], preferred_element_type=F32)
    subtiles = [(t, slice(t * tw, (t + 1) * tw)) for t in range(tm // tw)]

    qa = ut[_T_QA:_T_VA].reshape(2 * A_HEADS, A_DK, tm)
    qa = qa * lax.rsqrt(jnp.mean(qa * qa, axis=1, keepdims=True) + EPS)
    qat_ref[0] = (qa.reshape(2 * A_HEADS * A_DK, tm) * gq_ref[...]).astype(BF16)

    for t, sl in subtiles:
        vat_ref[0, t] = ut[_T_VA:_T_QB, sl].astype(BF16)

    qb = ut[_T_QB:_T_C].astype(BF16)
    for h in range(B_HEADS):
        ql = jnp.dot(wukt_ref[h], qb[h * B_DQ:(h + 1) * B_DQ], preferred_element_type=F32)
        ql = (ql * lax.rsqrt(jnp.mean(ql * ql, axis=0, keepdims=True) + EPS) * gql_ref[...]).astype(BF16)
        for t, sl in subtiles:
            qlt_ref[0, t, :, h * tw:(h + 1) * tw] = ql[:, sl]

    ct = ut[_T_C:_T_IQ]
    ct = (ct * lax.rsqrt(jnp.mean(ct * ct, axis=0, keepdims=True) + EPS) * gcc_ref[...]).astype(BF16)
    for t, sl in subtiles:
        ct_ref[0, t] = ct[:, sl]

    for h in range(IDX_HEADS):
        iq = ut[_T_IQ + h * IDX_DIM:_T_IQ + (h + 1) * IDX_DIM].astype(BF16)
        for t, sl in subtiles:
            iqt_ref[0, t, :, h * tw:(h + 1) * tw] = iq[:, sl]
    iwt_ref[0] = ut[_T_IW:_T_IW + IDX_HEADS] * (IDX_HEADS ** -0.5 * IDX_DIM ** -0.5)

    for hm in range(2 * A_HEADS):
        cols = slice(_R_KA + hm * A_DK, _R_KA + (hm + 1) * A_DK)
        ka_ref[0, :, cols] = _rms_lanes(ur[:, cols], gk_ref[...]).astype(BF16)
    c_ref[0] = _rms_lanes(ur[:, _R_C:_R_IK], gcr_ref[...]).astype(BF16)
    ik_ref[0] = ur[:, _R_IK:_R_IK + IDX_DIM].astype(BF16)


def _inproj(h3d, g_row, wt, wr, wukt, gq, gk, gql, gcr, gcc, tm, tw):
    b, s, d = h3d.shape
    nt = s // tw
    kern = functools.partial(_inproj_kernel, tm=tm, tw=tw)
    out_shape = [
        jax.ShapeDtypeStruct((b, 2 * A_HEADS * A_DK, s), BF16),
        jax.ShapeDtypeStruct((b, s, 2 * A_HEADS * A_DK), BF16),
        jax.ShapeDtypeStruct((b, nt, A_HEADS * A_DV, tw), BF16),
        jax.ShapeDtypeStruct((b, nt, B_DC, B_HEADS * tw), BF16),
        jax.ShapeDtypeStruct((b, s, B_DC), BF16),
        jax.ShapeDtypeStruct((b, nt, B_DC, tw), BF16),
        jax.ShapeDtypeStruct((b, nt, IDX_DIM, IDX_HEADS * tw), BF16),
        jax.ShapeDtypeStruct((b, s, IDX_DIM), BF16),
        jax.ShapeDtypeStruct((b, IDX_HEADS, s), F32),
    ]
    k = tm // tw
    out_specs = [
        pl.BlockSpec((1, 2 * A_HEADS * A_DK, tm), lambda bi, i: (bi, 0, i)),
        pl.BlockSpec((1, tm, 2 * A_HEADS * A_DK), lambda bi, i: (bi, i, 0)),
        pl.BlockSpec((1, k, A_HEADS * A_DV, tw), lambda bi, i: (bi, i, 0, 0)),
        pl.BlockSpec((1, k, B_DC, B_HEADS * tw), lambda bi, i: (bi, i, 0, 0)),
        pl.BlockSpec((1, tm, B_DC), lambda bi, i: (bi, i, 0)),
        pl.BlockSpec((1, k, B_DC, tw), lambda bi, i: (bi, i, 0, 0)),
        pl.BlockSpec((1, k, IDX_DIM, IDX_HEADS * tw), lambda bi, i: (bi, i, 0, 0)),
        pl.BlockSpec((1, tm, IDX_DIM), lambda bi, i: (bi, i, 0)),
        pl.BlockSpec((1, IDX_HEADS, tm), lambda bi, i: (bi, 0, i)),
    ]
    consts = (g_row, wt, wr, wukt, gq, gk, gql, gcr, gcc)
    return pl.pallas_call(
        kern,
        grid=(b, s // tm),
        in_specs=[pl.BlockSpec((1, tm, d), lambda bi, i: (bi, i, 0))] + [_const_spec(a.shape) for a in consts],
        out_specs=out_specs,
        out_shape=out_shape,
        compiler_params=_params(2),
        name="inproj",
    )(h3d, *consts)


def _key_to_f32(u):
    bits = u ^ (jnp.right_shift(u, 31) & 0x7FFFFFFF)
    return lax.bitcast_convert_type(bits, F32)


def _online(s_ref, lanes, m_ref, l_ref, acc_ref, v_lhs):
    m_old = m_ref[:, lanes]
    m_new = jnp.maximum(m_old, jnp.max(s_ref[...], axis=0, keepdims=True))
    m_ref[:, lanes] = m_new
    alpha = jnp.exp2(m_old - m_new)
    p = jnp.exp2(s_ref[...] - m_new)
    l_ref[:, lanes] = alpha * l_ref[:, lanes] + jnp.sum(p, axis=0, keepdims=True)
    acc_ref[:, lanes] = alpha * acc_ref[:, lanes] + jnp.dot(v_lhs, p.astype(BF16),
                                                           preferred_element_type=F32)


def _attn_kernel(qat_ref, qlt_ref, iqt_ref, iwt_ref, h_ref,
                 ka_ref, vat_ref, c_ref, ct_ref, ik_ref,
                 kam_ref, vatm_ref, cm_ref, ctm_ref, ikm_ref,
                 tall_ref, tmall_ref, lamp_ref, subln_ref, wuvt_ref, wout_ref,
                 o_ref,
                 st_ref, stm_ref, qbd_ref, sa_ref, sd_ref, m_ref, l_ref, acc_ref, m2_ref, l2_ref, acc2_ref, ot_ref):
    i = pl.program_id(1)
    n_tiles = i + 1
    midx = jnp.minimum(i, 1)

    def tile_rows(j):
        return pl.ds(pl.multiple_of(j * TQ, TQ), TQ)

    m_ref[...] = jnp.full(m_ref.shape, NEG, F32)
    l_ref[...] = jnp.zeros(l_ref.shape, F32)
    acc_ref[...] = jnp.zeros(acc_ref.shape, F32)
    m2_ref[...] = jnp.full(m2_ref.shape, NEG, F32)
    l2_ref[...] = jnp.zeros(l2_ref.shape, F32)
    acc2_ref[...] = jnp.zeros(acc2_ref.shape, F32)

    zq = jnp.zeros((A_DK, TQ), BF16)
    for h in range(A_HEADS):
        q0 = qat_ref[0, (2 * h) * A_DK:(2 * h + 1) * A_DK, :]
        q1 = qat_ref[0, (2 * h + 1) * A_DK:(2 * h + 2) * A_DK, :]
        qbd_ref[h, 0:A_DK, 0:TQ] = q0
        qbd_ref[h, 0:A_DK, TQ:A_LANES] = zq
        qbd_ref[h, A_DK:2 * A_DK, 0:TQ] = zq
        qbd_ref[h, A_DK:2 * A_DK, TQ:A_LANES] = q1

    def a_tile(k_get, v_get, bias_get):
        rows = v_get(0).shape[1]
        for h in range(A_HEADS):
            bias = bias_get(h)
            sa_ref[h, 0:rows, :] = (jnp.dot(k_get(h), qbd_ref[h], preferred_element_type=F32)
                                    + jnp.concatenate([bias, bias], axis=1))
        for h in range(A_HEADS):
            _online(sa_ref.at[h, 0:rows, :], slice(h * A_LANES, (h + 1) * A_LANES),
                    m_ref, l_ref, acc_ref, v_get(h))

    a_tile(lambda h: kam_ref[0, :, h * 2 * A_DK:(h + 1) * 2 * A_DK],
           lambda h: vatm_ref[0, 0, h * A_DV:(h + 1) * A_DV, :],
           lambda h: tmall_ref[h, midx])

    def a_body(j, carry):
        kidx = jnp.minimum(i - j, 2)
        a_tile(lambda h: ka_ref[0, tile_rows(j), h * 2 * A_DK:(h + 1) * 2 * A_DK],
               lambda h: vat_ref[0, j, h * A_DV:(h + 1) * A_DV, :],
               lambda h: tall_ref[h, kidx])
        return carry

    lax.fori_loop(0, n_tiles, a_body, 0)

    lp = lamp_ref[...]
    lam = (jnp.exp(jnp.sum(lp[0:1] * lp[1:2], axis=-1, keepdims=True))
           - jnp.exp(jnp.sum(lp[2:3] * lp[3:4], axis=-1, keepdims=True)) + LAM_INIT)
    for h in range(A_HEADS):
        l0 = slice(h * A_LANES, h * A_LANES + TQ)
        l1 = slice(h * A_LANES + TQ, (h + 1) * A_LANES)
        o = acc_ref[:, l0] / l_ref[:, l0] - lam * (acc_ref[:, l1] / l_ref[:, l1])
        o = o * lax.rsqrt(jnp.mean(o * o, axis=0, keepdims=True) + EPS) * subln_ref[...]
        ot_ref[h * A_DV:(h + 1) * A_DV, :] = o.astype(BF16)

    qrel = i * TQ + lax.broadcasted_iota(jnp.int32, (1, TQ), 1)

    def idx_scores(ik_tile):
        d = jnp.dot(ik_tile, iqt_ref[0, 0], preferred_element_type=F32)
        acc = jnp.zeros((ik_tile.shape[0], TQ), F32)
        for h in range(IDX_HEADS):
            acc = acc + iwt_ref[0, h:h + 1, :] * jnp.maximum(d[:, h * TQ:(h + 1) * TQ], 0.0)
        return acc

    mrow = lax.broadcasted_iota(jnp.int32, (META_PAD, TQ), 0)
    trow = lax.broadcasted_iota(jnp.int32, (TQ, TQ), 0)
    stm_ref[...] = jnp.where(mrow < N_META, idx_scores(ikm_ref[0]), -jnp.inf)

    def b_body(j, carry):
        sc = idx_scores(ik_ref[0, tile_rows(j), :])
        st_ref[j] = jnp.where(j * TQ + trow <= qrel, sc, -jnp.inf)
        return carry

    lax.fori_loop(0, n_tiles, b_body, 0)

    kf = jnp.minimum(IDX_TOPK_MAX, qrel + (N_META + 1)).astype(F32)

    def count(ind_meta, ind_tile):
        p = ind_meta(stm_ref[...]).reshape(META_PAD // 8, 8, TQ).sum(axis=0)
        p = lax.fori_loop(0, n_tiles,
                          lambda j, p: p + ind_tile(st_ref[j], j).reshape(TQ // 8, 8, TQ).sum(axis=0), p)
        return jnp.sum(p, axis=0, keepdims=True)

    def count_ge(t):
        ind = lambda x: jnp.where(x >= t, 1.0, 0.0)
        return count(ind, lambda x, j: ind(x))

    def count_gt(t):
        ind = lambda x: jnp.where(x > t, 1.0, 0.0)
        return count(ind, lambda x, j: ind(x))

    def bit_body(b, u):
        cand = u + jnp.left_shift(jnp.int32(1), 31 - b)
        return jnp.where(count_ge(_key_to_f32(cand)) >= kf, cand, u)

    u = lax.fori_loop(0, 32, bit_body, jnp.full((1, TQ), INT_MIN, jnp.int32))
    thr = _key_to_f32(u)

    need = kf - count_gt(thr)
    overflow = jnp.max(jnp.where(count_ge(thr) > kf, 1.0, 0.0)) > 0.0

    @pl.when(overflow)
    def _():
        def count_ties_before(xc):
            return count(lambda x: jnp.where((x == thr) & (mrow < xc), 1.0, 0.0),
                         lambda x, j: jnp.where((x == thr) & (j * TQ + N_META + trow < xc), 1.0, 0.0))

        def pos_body(b, xs):
            cand = xs | jnp.left_shift(jnp.int32(1), 11 - b)
            return jnp.where(count_ties_before(cand) < need, cand, xs)

        last = lax.fori_loop(0, 12, pos_body, jnp.zeros((1, TQ), jnp.int32))
        sm = stm_ref[...]
        stm_ref[...] = jnp.where((sm == thr) & (mrow > last), -jnp.inf, sm)

        def drop_body(j, carry):
            sx = st_ref[j]
            st_ref[j] = jnp.where((sx == thr) & (j * TQ + N_META + trow > last), -jnp.inf, sx)
            return carry

        lax.fori_loop(0, n_tiles, drop_body, 0)

    def d_tile(st_tile, c_tile, ct_tile, bias_get):
        sel = jnp.where(st_tile >= thr, 0.0, NEG)
        rows = c_tile.shape[0]
        for h in range(B_HEADS):
            sd_ref[h, 0:rows, :] = (jnp.dot(c_tile, qlt_ref[0, 0, :, h * TQ:(h + 1) * TQ],
                                            preferred_element_type=F32)
                                    + (bias_get(A_HEADS + h) + sel))
        for h in range(B_HEADS):
            _online(sd_ref.at[h, 0:rows, :], slice(h * TQ, (h + 1) * TQ), m2_ref, l2_ref, acc2_ref, ct_tile)

    d_tile(stm_ref[...], cm_ref[0], ctm_ref[0, 0], lambda h: tmall_ref[h, midx])

    def d_body(j, carry):
        kidx = jnp.minimum(i - j, 2)
        d_tile(st_ref[j], c_ref[0, tile_rows(j), :], ct_ref[0, j], lambda h: tall_ref[h, kidx])
        return carry

    lax.fori_loop(0, n_tiles, d_body, 0)

    for h in range(B_HEADS):
        lanes = slice(h * TQ, (h + 1) * TQ)
        olat = (acc2_ref[:, lanes] / l2_ref[:, lanes]).astype(BF16)
        ob = jnp.dot(wuvt_ref[h], olat, preferred_element_type=F32)
        r0 = A_HEADS * A_DV + h * B_DV
        ot_ref[r0:r0 + B_DV, :] = ob.astype(BF16)

    y = lax.dot_general(ot_ref[...], wout_ref[...], (((0,), (0,)), ((), ())),
                        preferred_element_type=F32)
    o_ref[0] = h_ref[0] + y


def _attention(q_side, h3d, k_side, meta_side, tall, tmall, lamp, subln, wuvt, wout):
    qat, qlt, iqt, iwt = q_side
    ka, vat, c, ct, ik = k_side
    kam, vatm, cm, ctm, ikm = meta_side
    b, s, d = h3d.shape
    nt = s // TQ
    in_specs = [
        pl.BlockSpec((1, qat.shape[1], TQ), lambda bi, i: (bi, 0, i)),
        pl.BlockSpec((1, 1) + qlt.shape[2:], lambda bi, i: (bi, i, 0, 0)),
        pl.BlockSpec((1, 1) + iqt.shape[2:], lambda bi, i: (bi, i, 0, 0)),
        pl.BlockSpec((1, iwt.shape[1], TQ), lambda bi, i: (bi, 0, i)),
        pl.BlockSpec((1, TQ, d), lambda bi, i: (bi, i, 0)),
        pl.BlockSpec((1,) + ka.shape[1:], lambda bi, i: (bi, 0, 0)),
        pl.BlockSpec((1,) + vat.shape[1:], lambda bi, i: (bi, 0, 0, 0)),
        pl.BlockSpec((1,) + c.shape[1:], lambda bi, i: (bi, 0, 0)),
        pl.BlockSpec((1,) + ct.shape[1:], lambda bi, i: (bi, 0, 0, 0)),
        pl.BlockSpec((1,) + ik.shape[1:], lambda bi, i: (bi, 0, 0)),
    ] + [_const_spec(a.shape) for a in (kam, vatm, cm, ctm, ikm, tall, tmall, lamp, subln, wuvt, wout)]
    scratch = [
        pltpu.VMEM((nt, TQ, TQ), F32),
        pltpu.VMEM((META_PAD, TQ), F32),
        pltpu.VMEM((A_HEADS, 2 * A_DK, A_LANES), BF16),
        pltpu.VMEM((A_HEADS, TQ, A_LANES), F32),
        pltpu.VMEM((B_HEADS, TQ, TQ), F32),
        pltpu.VMEM((1, A_HEADS * A_LANES), F32),
        pltpu.VMEM((1, A_HEADS * A_LANES), F32),
        pltpu.VMEM((A_DV, A_HEADS * A_LANES), F32),
        pltpu.VMEM((1, B_HEADS * TQ), F32),
        pltpu.VMEM((1, B_HEADS * TQ), F32),
        pltpu.VMEM((B_DC, B_HEADS * TQ), F32),
        pltpu.VMEM((A_HEADS * A_DV + B_HEADS * B_DV, TQ), BF16),
    ]
    return pl.pallas_call(
        _attn_kernel,
        grid=(b, nt),
        in_specs=in_specs,
        out_specs=pl.BlockSpec((1, TQ, d), lambda bi, i: (bi, i, 0)),
        out_shape=jax.ShapeDtypeStruct((b, s, d), F32),
        scratch_shapes=scratch,
        compiler_params=_params(2),
        name="attention",
    )(qat, qlt, iqt, iwt, h3d, ka, vat, c, ct, ik, kam, vatm, cm, ctm, ikm,
      tall, tmall, lamp, subln, wuvt, wout)


def kernel(x, meta_tokens, rel_bias, ffn1_norm, ffn1_w_gate, ffn1_w_up, ffn1_w_down, mix_norm, w_in, a_q_norm, a_k_norm, a_lambda_q1, a_lambda_k1, a_lambda_q2, a_lambda_k2, a_subln, b_kv_norm, b_w_uk, b_q_norm, b_w_uv, w_out, ffn2_norm, ffn2_w_gate, ffn2_w_up, ffn2_w_down):
    bsz, seq, d = x.shape
    assert d == D_MODEL and seq % PROJ_TM == 0 and (bsz * seq) % FFN_TM == 0
    assert min(IDX_TOPK_MAX, (seq + N_META) // 4) == IDX_TOPK_MAX
    assert ffn1_norm.shape[0] == 1, "single layer"
    lyr = 0
    row = lambda v: v.reshape(1, -1).astype(F32)
    col = lambda v: v.reshape(-1, 1).astype(F32)

    w = w_in[lyr]
    o_qa, o_ka, o_va, o_qb, o_c, o_iq, o_ik, o_iw = np.cumsum(
        [0, 2 * A_HEADS * A_DK, 2 * A_HEADS * A_DK, A_HEADS * A_DV, B_HEADS * B_DQ, B_DC,
         IDX_HEADS * IDX_DIM, IDX_DIM])[:8]
    w_qa, w_ka, w_va, w_qb = w[:, o_qa:o_ka], w[:, o_ka:o_va], w[:, o_va:o_qb], w[:, o_qb:o_c]
    w_c, w_iq, w_ik, w_iw = w[:, o_c:o_iq], w[:, o_iq:o_ik], w[:, o_ik:o_iw], w[:, o_iw:o_iw + IDX_HEADS]
    wt = jnp.concatenate([w_qa, w_va, w_qb, w_c, w_iq, w_iw,
                          jnp.zeros((d, _T_END - _T_IW - IDX_HEADS), w.dtype)], axis=1).T.astype(BF16)
    wr = jnp.concatenate([w_ka, w_c, w_ik, jnp.zeros((d, _R_END - _R_IK - IDX_DIM), w.dtype)],
                         axis=1).astype(BF16)
    wukt = jnp.transpose(b_w_uk[lyr], (0, 2, 1)).astype(BF16)
    wuvt = jnp.transpose(b_w_uv[lyr], (0, 2, 1)).astype(BF16)
    gq = col(jnp.tile(a_q_norm[lyr], 2 * A_HEADS)) * (A_SCALE * LOG2E)
    gk = row(a_k_norm[lyr])
    gql = col(b_q_norm[lyr]) * (B_SCALE * LOG2E)
    gcr, gcc = row(b_kv_norm[lyr]), col(b_kv_norm[lyr])
    subln = col(a_subln[lyr]) * (1.0 - LAM_INIT)
    lamp = jnp.stack([a_lambda_q1[lyr], a_lambda_k1[lyr], a_lambda_q2[lyr], a_lambda_k2[lyr]]).astype(F32)
    proj_consts = (row(mix_norm[lyr]), wt, wr, wukt, gq, gk, gql, gcr, gcc)

    tall, tmall = _bias_tiles(rel_bias.astype(F32))

    ffn1 = (row(ffn1_norm[lyr]), ffn1_w_gate[lyr].astype(BF16), ffn1_w_up[lyr].astype(BF16),
            ffn1_w_down[lyr].astype(BF16))
    h1 = _ffn(x.reshape(bsz * seq, d), *ffn1, tm=FFN_TM).reshape(bsz, seq, d)
    meta_pad = jnp.pad(meta_tokens.astype(x.dtype), ((0, META_PAD - N_META), (0, 0)))
    h1m = _ffn(meta_pad, *ffn1, tm=META_PAD).reshape(1, META_PAD, d)

    qat, ka, vat, qlt, c, ct, iqt, ik, iwt = _inproj(h1, *proj_consts, tm=PROJ_TM, tw=TQ)
    _, kam, vatm, _, cm, ctm, _, ikm, _ = _inproj(h1m, *proj_consts, tm=META_PAD, tw=META_PAD)

    h2 = _attention((qat, qlt, iqt, iwt), h1, (ka, vat, c, ct, ik), (kam, vatm, cm, ctm, ikm),
                    tall, tmall, lamp, subln, wuvt, w_out[lyr].astype(BF16))

    ffn2 = (row(ffn2_norm[lyr]), ffn2_w_gate[lyr].astype(BF16), ffn2_w_up[lyr].astype(BF16),
            ffn2_w_down[lyr].astype(BF16))
    return _ffn(h2.reshape(bsz * seq, d), *ffn2, tm=FFN_TM).reshape(bsz, seq, d)
```

```python
import functools
import math

import jax
import jax.numpy as jnp
import numpy as np
from jax import lax
from jax.experimental import pallas as pl
from jax.experimental.pallas import tpu as pltpu

F32 = jnp.float32
BF16 = jnp.bfloat16

D_MODEL = 1024
N_META = 16
A_HEADS = 4
A_DK = 64
A_DV = 128
B_HEADS = 4
B_DQ = 128
B_DC = 256
B_DV = 128
IDX_HEADS = 8
IDX_DIM = 64
IDX_TOPK_MAX = 256
REL_BUCKETS = 32
REL_MAX_DIST = 128
D_FF = 2816
EPS = 1e-6
LOG2E = math.log2(math.e)
A_SCALE = A_DK ** -0.5
B_SCALE = B_DC ** -0.5
LAM_INIT = 0.8 - 0.6 * math.exp(-0.3 * 0)

NEG = -1e30
INT_MIN = -(2 ** 31)

TQ = 256
META_PAD = 128
FF_CHUNK = 1408
FFN_TM = 512
PROJ_TM = 512
V7X_VMEM_LIMIT = 56 * 1024 * 1024

A_LANES = 2 * TQ
PACK = 16
assert N_META % PACK == 0
_T_QA, _T_VA, _T_QB, _T_C, _T_IQ, _T_IW, _T_END = 0, 512, 1024, 1536, 1792, 2304, 2320
_R_KA, _R_C, _R_IK, _R_END = 0, 512, 768, 896


def _bucket_edges():
    max_exact = REL_BUCKETS // 2
    n = np.arange(0, 4 * REL_MAX_DIST)
    nf = np.maximum(n, 1).astype(np.float64)
    large = max_exact + (np.log(nf / max_exact) / math.log(REL_MAX_DIST / max_exact)
                         * (REL_BUCKETS - max_exact)).astype(np.int64)
    large = np.minimum(large, REL_BUCKETS - 1)
    bucket = np.where(n < max_exact, n, large)
    return [int(n[np.argmax(bucket >= j)]) for j in range(REL_BUCKETS)]


_EDGES = _bucket_edges()
assert _EDGES[-1] <= TQ, "tiles two or more away from the diagonal must sit in the last bucket"


def _const_spec(shape):
    nd = len(shape)
    return pl.BlockSpec(shape, lambda *_: (0,) * nd, pipeline_mode=pl.Buffered(1))


def _params(n_grid):
    return pltpu.CompilerParams(dimension_semantics=("arbitrary",) * n_grid,
                                vmem_limit_bytes=V7X_VMEM_LIMIT)


def _rms_lanes(x, g_row):
    ms = jnp.mean(x * x, axis=-1, keepdims=True)
    return x * lax.rsqrt(ms + EPS) * g_row


def _bias_kernel(rb_ref, tall_ref, tmall_ref):
    h = pl.program_id(0)

    def table(d):
        val = jnp.full(d.shape, rb_ref[0, h] * LOG2E, F32)
        for j in range(1, REL_BUCKETS):
            val = jnp.where(d >= _EDGES[j], rb_ref[j, h] * LOG2E, val)
        return val

    far = rb_ref[REL_BUCKETS - 1, h] * LOG2E
    c = lax.broadcasted_iota(jnp.int32, (TQ, TQ), 0)
    r = lax.broadcasted_iota(jnp.int32, (TQ, TQ), 1)
    d0 = r - c
    tall_ref[0, 0] = jnp.where(d0 >= 0, table(d0), NEG)
    tall_ref[0, 1] = table(d0 + TQ)
    tall_ref[0, 2] = jnp.full((TQ, TQ), far, F32)
    m = lax.broadcasted_iota(jnp.int32, (N_META, TQ), 0)
    r2 = lax.broadcasted_iota(jnp.int32, (N_META, TQ), 1)
    tmall_ref[0, 0] = table(r2 + N_META - m)
    tmall_ref[0, 1] = jnp.full((N_META, TQ), far, F32)


def _bias_tiles(rel_bias):
    nh = rel_bias.shape[1]
    return pl.pallas_call(
        _bias_kernel,
        grid=(nh,),
        in_specs=[pl.BlockSpec(memory_space=pltpu.SMEM)],
        out_specs=[pl.BlockSpec((1, 3, TQ, TQ), lambda h: (h, 0, 0, 0)),
                   pl.BlockSpec((1, 2, N_META, TQ), lambda h: (h, 0, 0, 0))],
        out_shape=[jax.ShapeDtypeStruct((nh, 3, TQ, TQ), F32),
                   jax.ShapeDtypeStruct((nh, 2, N_META, TQ), F32)],
        compiler_params=_params(1),
        name="bias_tiles",
    )(rel_bias)


def _ffn_kernel(x_ref, g_ref, wg_ref, wu_ref, wd_ref, o_ref):
    x = x_ref[...]
    xn = _rms_lanes(x, g_ref[...]).astype(BF16)
    y = None
    for c in range(D_FF // FF_CHUNK):
        sl = slice(c * FF_CHUNK, (c + 1) * FF_CHUNK)
        g = jnp.dot(xn, wg_ref[:, sl], preferred_element_type=F32)
        u = jnp.dot(xn, wu_ref[:, sl], preferred_element_type=F32)
        a = (g * jax.nn.sigmoid(g) * u).astype(BF16)
        part = jnp.dot(a, wd_ref[sl, :], preferred_element_type=F32)
        y = part if y is None else y + part
    o_ref[...] = x + 0.5 * y


def _ffn(h2d, g_row, wg, wu, wd, tm):
    m, d = h2d.shape
    return pl.pallas_call(
        _ffn_kernel,
        grid=(m // tm,),
        in_specs=[pl.BlockSpec((tm, d), lambda i: (i, 0)),
                  _const_spec((1, d)),
                  _const_spec(wg.shape), _const_spec(wu.shape), _const_spec(wd.shape)],
        out_specs=pl.BlockSpec((tm, d), lambda i: (i, 0)),
        out_shape=jax.ShapeDtypeStruct((m, d), F32),
        compiler_params=_params(1),
        name="ffn",
    )(h2d, g_row, wg, wu, wd)


def _inproj_kernel(h_ref, g_ref, wt_ref, wr_ref, wukt_ref, gq_ref, gk_ref, gql_ref, gcr_ref, gcc_ref,
                   qat_ref, ka_ref, vat_ref, qlt_ref, c_ref, ct_ref, iqt_ref, ik_ref, iwt_ref, *, tm, tw):
    xn = _rms_lanes(h_ref[0], g_ref[...]).astype(BF16)
    ut = lax.dot_general(wt_ref[...], xn, (((1,), (1,)), ((), ())),
                         preferred_element_type=F32)
    ur = jnp.dot(xn, wr_ref[...], preferred_element_type=F32)
    subtiles = [(t, slice(t * tw, (t + 1) * tw)) for t in range(tm // tw)]

    qa = ut[_T_QA:_T_VA].reshape(2 * A_HEADS, A_DK, tm)
    qa = qa * lax.rsqrt(jnp.mean(qa * qa, axis=1, keepdims=True) + EPS)
    qat_ref[0] = (qa.reshape(2 * A_HEADS * A_DK, tm) * gq_ref[...]).astype(BF16)

    for t, sl in subtiles:
        vat_ref[0, t] = ut[_T_VA:_T_QB, sl].astype(BF16)

    qb = ut[_T_QB:_T_C].astype(BF16)
    for h in range(B_HEADS):
        ql = jnp.dot(wukt_ref[h], qb[h * B_DQ:(h + 1) * B_DQ], preferred_element_type=F32)
        ql = (ql * lax.rsqrt(jnp.mean(ql * ql, axis=0, keepdims=True) + EPS) * gql_ref[...]).astype(BF16)
        for t, sl in subtiles:
            qlt_ref[0, t, :, h * tw:(h + 1) * tw] = ql[:, sl]

    ct = ut[_T_C:_T_IQ]
    ct = (ct * lax.rsqrt(jnp.mean(ct * ct, axis=0, keepdims=True) + EPS) * gcc_ref[...]).astype(BF16)
    for t, sl in subtiles:
        ct_ref[0, t] = ct[:, sl]

    for h in range(IDX_HEADS):
        iq = ut[_T_IQ + h * IDX_DIM:_T_IQ + (h + 1) * IDX_DIM].astype(BF16)
        for t, sl in subtiles:
            iqt_ref[0, t, :, h * tw:(h + 1) * tw] = iq[:, sl]
    iwt_ref[0] = ut[_T_IW:_T_IW + IDX_HEADS] * (IDX_HEADS ** -0.5 * IDX_DIM ** -0.5)

    for hm in range(2 * A_HEADS):
        cols = slice(_R_KA + hm * A_DK, _R_KA + (hm + 1) * A_DK)
        ka_ref[0, :, cols] = _rms_lanes(ur[:, cols], gk_ref[...]).astype(BF16)
    c_ref[0] = _rms_lanes(ur[:, _R_C:_R_IK], gcr_ref[...]).astype(BF16)
    ik_ref[0] = ur[:, _R_IK:_R_IK + IDX_DIM].astype(BF16)


def _inproj(h3d, g_row, wt, wr, wukt, gq, gk, gql, gcr, gcc, tm, tw):
    b, s, d = h3d.shape
    nt = s // tw
    kern = functools.partial(_inproj_kernel, tm=tm, tw=tw)
    out_shape = [
        jax.ShapeDtypeStruct((b, 2 * A_HEADS * A_DK, s), BF16),
        jax.ShapeDtypeStruct((b, s, 2 * A_HEADS * A_DK), BF16),
        jax.ShapeDtypeStruct((b, nt, A_HEADS * A_DV, tw), BF16),
        jax.ShapeDtypeStruct((b, nt, B_DC, B_HEADS * tw), BF16),
        jax.ShapeDtypeStruct((b, s, B_DC), BF16),
        jax.ShapeDtypeStruct((b, nt, B_DC, tw), BF16),
        jax.ShapeDtypeStruct((b, nt, IDX_DIM, IDX_HEADS * tw), BF16),
        jax.ShapeDtypeStruct((b, s, IDX_DIM), BF16),
        jax.ShapeDtypeStruct((b, IDX_HEADS, s), F32),
    ]
    k = tm // tw
    out_specs = [
        pl.BlockSpec((1, 2 * A_HEADS * A_DK, tm), lambda bi, i: (bi, 0, i)),
        pl.BlockSpec((1, tm, 2 * A_HEADS * A_DK), lambda bi, i: (bi, i, 0)),
        pl.BlockSpec((1, k, A_HEADS * A_DV, tw), lambda bi, i: (bi, i, 0, 0)),
        pl.BlockSpec((1, k, B_DC, B_HEADS * tw), lambda bi, i: (bi, i, 0, 0)),
        pl.BlockSpec((1, tm, B_DC), lambda bi, i: (bi, i, 0)),
        pl.BlockSpec((1, k, B_DC, tw), lambda bi, i: (bi, i, 0, 0)),
        pl.BlockSpec((1, k, IDX_DIM, IDX_HEADS * tw), lambda bi, i: (bi, i, 0, 0)),
        pl.BlockSpec((1, tm, IDX_DIM), lambda bi, i: (bi, i, 0)),
        pl.BlockSpec((1, IDX_HEADS, tm), lambda bi, i: (bi, 0, i)),
    ]
    consts = (g_row, wt, wr, wukt, gq, gk, gql, gcr, gcc)
    return pl.pallas_call(
        kern,
        grid=(b, s // tm),
        in_specs=[pl.BlockSpec((1, tm, d), lambda bi, i: (bi, i, 0))] + [_const_spec(a.shape) for a in consts],
        out_specs=out_specs,
        out_shape=out_shape,
        compiler_params=_params(2),
        name="inproj",
    )(h3d, *consts)


def _key_to_f32(u):
    bits = u ^ (jnp.right_shift(u, 31) & 0x7FFFFFFF)
    return lax.bitcast_convert_type(bits, F32)


def _online(s_ref, lanes, m_ref, l_ref, acc_ref, v_lhs, first=False):
    if first:
        m_new = jnp.max(s_ref[...], axis=0, keepdims=True)
        m_ref[:, lanes] = m_new
        p = jnp.exp2(s_ref[...] - m_new)
        l_ref[:, lanes] = jnp.sum(p, axis=0, keepdims=True)
        acc_ref[:, lanes] = jnp.dot(v_lhs, p.astype(BF16), preferred_element_type=F32)
        return
    m_old = m_ref[:, lanes]
    m_new = jnp.maximum(m_old, jnp.max(s_ref[...], axis=0, keepdims=True))
    m_ref[:, lanes] = m_new
    alpha = jnp.exp2(m_old - m_new)
    p = jnp.exp2(s_ref[...] - m_new)
    l_ref[:, lanes] = alpha * l_ref[:, lanes] + jnp.sum(p, axis=0, keepdims=True)
    acc_ref[:, lanes] = alpha * acc_ref[:, lanes] + jnp.dot(v_lhs, p.astype(BF16),
                                                           preferred_element_type=F32)


def _attn_kernel(qat_ref, qlt_ref, iqt_ref, iwt_ref, h_ref,
                 ka_ref, vat_ref, c_ref, ct_ref, ik_ref,
                 kam_ref, vatm_ref, cm_ref, ctm_ref, ikm_ref,
                 tall_ref, tmall_ref, lamp_ref, subln_ref, wuvt_ref, wout_ref,
                 o_ref,
                 st_ref, stm_ref, hi_ref, him_ref, b1_ref, b1m_ref, b0_ref, b0m_ref,
                 qbd_ref, sa_ref, sd_ref, m_ref, l_ref, acc_ref, m2_ref, l2_ref, acc2_ref, ot_ref):
    i = pl.program_id(1)
    n_tiles = i + 1
    midx = jnp.minimum(i, 1)

    def tile_rows(j):
        return pl.ds(pl.multiple_of(j * TQ, TQ), TQ)

    zq = jnp.zeros((A_DK, TQ), BF16)
    for h in range(A_HEADS):
        q0 = qat_ref[0, (2 * h) * A_DK:(2 * h + 1) * A_DK, :]
        q1 = qat_ref[0, (2 * h + 1) * A_DK:(2 * h + 2) * A_DK, :]
        qbd_ref[h, 0:A_DK, 0:TQ] = q0
        qbd_ref[h, 0:A_DK, TQ:A_LANES] = zq
        qbd_ref[h, A_DK:2 * A_DK, 0:TQ] = zq
        qbd_ref[h, A_DK:2 * A_DK, TQ:A_LANES] = q1

    def a_tile(k_get, v_get, bias_get, first=False):
        rows = v_get(0).shape[1]
        for h in range(A_HEADS):
            bias = bias_get(h)
            sa_ref[h, 0:rows, :] = (jnp.dot(k_get(h), qbd_ref[h], preferred_element_type=F32)
                                    + jnp.concatenate([bias, bias], axis=1))
        for h in range(A_HEADS):
            _online(sa_ref.at[h, 0:rows, :], slice(h * A_LANES, (h + 1) * A_LANES),
                    m_ref, l_ref, acc_ref, v_get(h), first)

    qrel = i * TQ + lax.broadcasted_iota(jnp.int32, (1, TQ), 1)

    def idx_scores(ik_tile):
        d = jnp.dot(ik_tile, iqt_ref[0, 0], preferred_element_type=F32)
        acc = jnp.zeros((ik_tile.shape[0], TQ), F32)
        for h in range(IDX_HEADS):
            acc = acc + iwt_ref[0, h:h + 1, :] * jnp.maximum(d[:, h * TQ:(h + 1) * TQ], 0.0)
        return acc

    def store_digits(st, hi_dst, b1_dst, b0_dst):
        bits = lax.bitcast_convert_type(st, jnp.int32)
        key = bits ^ (jnp.right_shift(bits, 31) & 0x7FFFFFFF)
        hi_dst[...] = lax.bitcast_convert_type(bits & jnp.int32(-65536), F32).astype(BF16)
        b1_dst[...] = (jnp.right_shift(key, 8) & 255).astype(F32).astype(BF16)
        b0_dst[...] = (key & 255).astype(F32).astype(BF16)

    mrow = lax.broadcasted_iota(jnp.int32, (N_META, TQ), 0)
    trow = lax.broadcasted_iota(jnp.int32, (TQ, TQ), 0)

    a_tile(lambda h: kam_ref[0, 0:N_META, h * 2 * A_DK:(h + 1) * 2 * A_DK],
           lambda h: vatm_ref[0, 0, h * A_DV:(h + 1) * A_DV, 0:N_META],
           lambda h: tmall_ref[h, midx], first=True)
    sm = idx_scores(ikm_ref[0, 0:N_META, :])
    stm_ref[...] = sm
    store_digits(sm, him_ref, b1m_ref, b0m_ref)

    def ab_body(j, carry):
        kidx = jnp.minimum(i - j, 2)
        a_tile(lambda h: ka_ref[0, tile_rows(j), h * 2 * A_DK:(h + 1) * 2 * A_DK],
               lambda h: vat_ref[0, j, h * A_DV:(h + 1) * A_DV, :],
               lambda h: tall_ref[h, kidx])
        sc = idx_scores(ik_ref[0, tile_rows(j), :])
        sx = jnp.where(j * TQ + trow <= qrel, sc, -jnp.inf)
        st_ref[j] = sx
        store_digits(sx, hi_ref.at[j], b1_ref.at[j], b0_ref.at[j])
        return carry

    lax.fori_loop(0, n_tiles, ab_body, 0)

    lp = lamp_ref[...]
    lam = (jnp.exp(jnp.sum(lp[0:1] * lp[1:2], axis=-1, keepdims=True))
           - jnp.exp(jnp.sum(lp[2:3] * lp[3:4], axis=-1, keepdims=True)) + LAM_INIT)
    for h in range(A_HEADS):
        l0 = slice(h * A_LANES, h * A_LANES + TQ)
        l1 = slice(h * A_LANES + TQ, (h + 1) * A_LANES)
        o = acc_ref[:, l0] / l_ref[:, l0] - lam * (acc_ref[:, l1] / l_ref[:, l1])
        o = o * lax.rsqrt(jnp.mean(o * o, axis=0, keepdims=True) + EPS) * subln_ref[...]
        ot_ref[h * A_DV:(h + 1) * A_DV, :] = o.astype(BF16)

    kf = jnp.minimum(IDX_TOPK_MAX, qrel + (N_META + 1)).astype(F32)
    one, zero = jnp.full((), 1, BF16), jnp.full((), 0, BF16)

    def pk_count(meta_ref, tile_ref, pred, t_row):
        tb = jnp.broadcast_to(t_row.astype(BF16), (PACK, TQ))[None]

        def part(x):
            x3 = x.reshape(x.shape[0] // PACK, PACK, TQ)
            ind = jnp.where(pred(x3, tb), one, zero)
            acc = ind[0]
            for k in range(1, ind.shape[0]):
                acc = acc + ind[k]
            return acc.astype(F32)

        p = lax.fori_loop(0, n_tiles, lambda j, p: p + part(tile_ref[j]), part(meta_ref[...]))
        return jnp.sum(p, axis=0, keepdims=True)

    def pk_keep(meta_ref, tile_ref, sel_meta_ref, sel_tile_ref, t_row):
        tb = jnp.broadcast_to(t_row.astype(BF16), (1, TQ))
        meta_ref[...] = jnp.where(sel_meta_ref[...] == tb, meta_ref[...], -one)

        def body(j, carry):
            tile_ref[j] = jnp.where(sel_tile_ref[j] == tb, tile_ref[j], -one)
            return carry

        lax.fori_loop(0, n_tiles, body, 0)

    ge = lambda x, t: x >= t
    gt = lambda x, t: x > t

    def top_half_value(w):
        k = w - 32768
        b16 = (k ^ (jnp.right_shift(k, 15) & 0x7FFF)) & 0xFFFF
        return lax.bitcast_convert_type(jnp.left_shift(b16, 16), F32)

    def bisect(n_bits, to_value, meta_ref, tile_ref, need):
        def body(b, w):
            cand = w | jnp.left_shift(jnp.int32(1), n_bits - 1 - b)
            return jnp.where(pk_count(meta_ref, tile_ref, ge, to_value(cand)) >= need, cand, w)

        return lax.fori_loop(0, n_bits, body, jnp.zeros((1, TQ), jnp.int32))

    as_f32 = lambda d: d.astype(F32)
    w_hi = bisect(16, top_half_value, him_ref, hi_ref, kf)
    need1 = kf - pk_count(him_ref, hi_ref, gt, top_half_value(w_hi))
    pk_keep(b1m_ref, b1_ref, him_ref, hi_ref, top_half_value(w_hi))
    d1 = bisect(8, as_f32, b1m_ref, b1_ref, need1)
    need0 = need1 - pk_count(b1m_ref, b1_ref, gt, as_f32(d1))
    pk_keep(b0m_ref, b0_ref, b1m_ref, b1_ref, as_f32(d1))
    d0 = bisect(8, as_f32, b0m_ref, b0_ref, need0)
    u = jnp.left_shift(w_hi - 32768, 16) | jnp.left_shift(d1, 8) | d0
    thr = _key_to_f32(u)

    def count(ind_meta, ind_tile):
        p = ind_meta(stm_ref[...]).reshape(N_META // 8, 8, TQ).sum(axis=0)
        p = lax.fori_loop(0, n_tiles,
                          lambda j, p: p + ind_tile(st_ref[j], j).reshape(TQ // 8, 8, TQ).sum(axis=0), p)
        return jnp.sum(p, axis=0, keepdims=True)

    def count_ge(t):
        ind = lambda x: jnp.where(x >= t, 1.0, 0.0)
        return count(ind, lambda x, j: ind(x))

    def count_gt(t):
        ind = lambda x: jnp.where(x > t, 1.0, 0.0)
        return count(ind, lambda x, j: ind(x))

    need = kf - count_gt(thr)
    overflow = jnp.max(jnp.where(count_ge(thr) > kf, 1.0, 0.0)) > 0.0

    @pl.when(overflow)
    def _():
        def count_ties_before(xc):
            return count(lambda x: jnp.where((x == thr) & (mrow < xc), 1.0, 0.0),
                         lambda x, j: jnp.where((x == thr) & (j * TQ + N_META + trow < xc), 1.0, 0.0))

        def pos_body(b, xs):
            cand = xs | jnp.left_shift(jnp.int32(1), 11 - b)
            return jnp.where(count_ties_before(cand) < need, cand, xs)

        last = lax.fori_loop(0, 12, pos_body, jnp.zeros((1, TQ), jnp.int32))
        sm = stm_ref[...]
        stm_ref[...] = jnp.where((sm == thr) & (mrow > last), -jnp.inf, sm)

        def drop_body(j, carry):
            sx = st_ref[j]
            st_ref[j] = jnp.where((sx == thr) & (j * TQ + N_META + trow > last), -jnp.inf, sx)
            return carry

        lax.fori_loop(0, n_tiles, drop_body, 0)

    def d_tile(st_tile, c_tile, ct_tile, bias_get, first=False):
        sel = jnp.where(st_tile >= thr, 0.0, NEG)
        rows = c_tile.shape[0]
        for h in range(B_HEADS):
            sd_ref[h, 0:rows, :] = (jnp.dot(c_tile, qlt_ref[0, 0, :, h * TQ:(h + 1) * TQ],
                                            preferred_element_type=F32)
                                    + (bias_get(A_HEADS + h) + sel))
        for h in range(B_HEADS):
            _online(sd_ref.at[h, 0:rows, :], slice(h * TQ, (h + 1) * TQ), m2_ref, l2_ref, acc2_ref,
                    ct_tile, first)

    d_tile(stm_ref[...], cm_ref[0, 0:N_META, :], ctm_ref[0, 0, :, 0:N_META], lambda h: tmall_ref[h, midx],
           first=True)

    def d_body(j, carry):
        kidx = jnp.minimum(i - j, 2)
        d_tile(st_ref[j], c_ref[0, tile_rows(j), :], ct_ref[0, j], lambda h: tall_ref[h, kidx])
        return carry

    lax.fori_loop(0, n_tiles, d_body, 0)

    for h in range(B_HEADS):
        lanes = slice(h * TQ, (h + 1) * TQ)
        olat = (acc2_ref[:, lanes] / l2_ref[:, lanes]).astype(BF16)
        ob = jnp.dot(wuvt_ref[h], olat, preferred_element_type=F32)
        r0 = A_HEADS * A_DV + h * B_DV
        ot_ref[r0:r0 + B_DV, :] = ob.astype(BF16)

    y = lax.dot_general(ot_ref[...], wout_ref[...], (((0,), (0,)), ((), ())),
                        preferred_element_type=F32)
    o_ref[0] = h_ref[0] + y


def _attention(q_side, h3d, k_side, meta_side, tall, tmall, lamp, subln, wuvt, wout):
    qat, qlt, iqt, iwt = q_side
    ka, vat, c, ct, ik = k_side
    kam, vatm, cm, ctm, ikm = meta_side
    b, s, d = h3d.shape
    nt = s // TQ
    in_specs = [
        pl.BlockSpec((1, qat.shape[1], TQ), lambda bi, i: (bi, 0, i)),
        pl.BlockSpec((1, 1) + qlt.shape[2:], lambda bi, i: (bi, i, 0, 0)),
        pl.BlockSpec((1, 1) + iqt.shape[2:], lambda bi, i: (bi, i, 0, 0)),
        pl.BlockSpec((1, iwt.shape[1], TQ), lambda bi, i: (bi, 0, i)),
        pl.BlockSpec((1, TQ, d), lambda bi, i: (bi, i, 0)),
        pl.BlockSpec((1,) + ka.shape[1:], lambda bi, i: (bi, 0, 0)),
        pl.BlockSpec((1,) + vat.shape[1:], lambda bi, i: (bi, 0, 0, 0)),
        pl.BlockSpec((1,) + c.shape[1:], lambda bi, i: (bi, 0, 0)),
        pl.BlockSpec((1,) + ct.shape[1:], lambda bi, i: (bi, 0, 0, 0)),
        pl.BlockSpec((1,) + ik.shape[1:], lambda bi, i: (bi, 0, 0)),
    ] + [_const_spec(a.shape) for a in (kam, vatm, cm, ctm, ikm, tall, tmall, lamp, subln, wuvt, wout)]
    scratch = [
        pltpu.VMEM((nt, TQ, TQ), F32),
        pltpu.VMEM((N_META, TQ), F32),
        pltpu.VMEM((nt, TQ, TQ), BF16), pltpu.VMEM((N_META, TQ), BF16),
        pltpu.VMEM((nt, TQ, TQ), BF16), pltpu.VMEM((N_META, TQ), BF16),
        pltpu.VMEM((nt, TQ, TQ), BF16), pltpu.VMEM((N_META, TQ), BF16),
        pltpu.VMEM((A_HEADS, 2 * A_DK, A_LANES), BF16),
        pltpu.VMEM((A_HEADS, TQ, A_LANES), F32),
        pltpu.VMEM((B_HEADS, TQ, TQ), F32),
        pltpu.VMEM((1, A_HEADS * A_LANES), F32),
        pltpu.VMEM((1, A_HEADS * A_LANES), F32),
        pltpu.VMEM((A_DV, A_HEADS * A_LANES), F32),
        pltpu.VMEM((1, B_HEADS * TQ), F32),
        pltpu.VMEM((1, B_HEADS * TQ), F32),
        pltpu.VMEM((B_DC, B_HEADS * TQ), F32),
        pltpu.VMEM((A_HEADS * A_DV + B_HEADS * B_DV, TQ), BF16),
    ]
    return pl.pallas_call(
        _attn_kernel,
        grid=(b, nt),
        in_specs=in_specs,
        out_specs=pl.BlockSpec((1, TQ, d), lambda bi, i: (bi, i, 0)),
        out_shape=jax.ShapeDtypeStruct((b, s, d), F32),
        scratch_shapes=scratch,
        compiler_params=_params(2),
        name="attention",
    )(qat, qlt, iqt, iwt, h3d, ka, vat, c, ct, ik, kam, vatm, cm, ctm, ikm,
      tall, tmall, lamp, subln, wuvt, wout)


def kernel(x, meta_tokens, rel_bias, ffn1_norm, ffn1_w_gate, ffn1_w_up, ffn1_w_down, mix_norm, w_in, a_q_norm, a_k_norm, a_lambda_q1, a_lambda_k1, a_lambda_q2, a_lambda_k2, a_subln, b_kv_norm, b_w_uk, b_q_norm, b_w_uv, w_out, ffn2_norm, ffn2_w_gate, ffn2_w_up, ffn2_w_down):
    bsz, seq, d = x.shape
    assert d == D_MODEL and seq % PROJ_TM == 0 and (bsz * seq) % FFN_TM == 0
    assert min(IDX_TOPK_MAX, (seq + N_META) // 4) == IDX_TOPK_MAX
    assert ffn1_norm.shape[0] == 1, "single layer"
    lyr = 0
    row = lambda v: v.reshape(1, -1).astype(F32)
    col = lambda v: v.reshape(-1, 1).astype(F32)

    w = w_in[lyr]
    o_qa, o_ka, o_va, o_qb, o_c, o_iq, o_ik, o_iw = np.cumsum(
        [0, 2 * A_HEADS * A_DK, 2 * A_HEADS * A_DK, A_HEADS * A_DV, B_HEADS * B_DQ, B_DC,
         IDX_HEADS * IDX_DIM, IDX_DIM])[:8]
    w_qa, w_ka, w_va, w_qb = w[:, o_qa:o_ka], w[:, o_ka:o_va], w[:, o_va:o_qb], w[:, o_qb:o_c]
    w_c, w_iq, w_ik, w_iw = w[:, o_c:o_iq], w[:, o_iq:o_ik], w[:, o_ik:o_iw], w[:, o_iw:o_iw + IDX_HEADS]
    wt = jnp.concatenate([w_qa, w_va, w_qb, w_c, w_iq, w_iw,
                          jnp.zeros((d, _T_END - _T_IW - IDX_HEADS), w.dtype)], axis=1).T.astype(BF16)
    wr = jnp.concatenate([w_ka, w_c, w_ik, jnp.zeros((d, _R_END - _R_IK - IDX_DIM), w.dtype)],
                         axis=1).astype(BF16)
    wukt = jnp.transpose(b_w_uk[lyr], (0, 2, 1)).astype(BF16)
    wuvt = jnp.transpose(b_w_uv[lyr], (0, 2, 1)).astype(BF16)
    gq = col(jnp.tile(a_q_norm[lyr], 2 * A_HEADS)) * (A_SCALE * LOG2E)
    gk = row(a_k_norm[lyr])
    gql = col(b_q_norm[lyr]) * (B_SCALE * LOG2E)
    gcr, gcc = row(b_kv_norm[lyr]), col(b_kv_norm[lyr])
    subln = col(a_subln[lyr]) * (1.0 - LAM_INIT)
    lamp = jnp.stack([a_lambda_q1[lyr], a_lambda_k1[lyr], a_lambda_q2[lyr], a_lambda_k2[lyr]]).astype(F32)
    proj_consts = (row(mix_norm[lyr]), wt, wr, wukt, gq, gk, gql, gcr, gcc)

    tall, tmall = _bias_tiles(rel_bias.astype(F32))

    ffn1 = (row(ffn1_norm[lyr]), ffn1_w_gate[lyr].astype(BF16), ffn1_w_up[lyr].astype(BF16),
            ffn1_w_down[lyr].astype(BF16))
    h1 = _ffn(x.reshape(bsz * seq, d), *ffn1, tm=FFN_TM).reshape(bsz, seq, d)
    meta_pad = jnp.pad(meta_tokens.astype(x.dtype), ((0, META_PAD - N_META), (0, 0)))
    h1m = _ffn(meta_pad, *ffn1, tm=META_PAD).reshape(1, META_PAD, d)

    qat, ka, vat, qlt, c, ct, iqt, ik, iwt = _inproj(h1, *proj_consts, tm=PROJ_TM, tw=TQ)
    _, kam, vatm, _, cm, ctm, _, ikm, _ = _inproj(h1m, *proj_consts, tm=META_PAD, tw=META_PAD)

    h2 = _attention((qat, qlt, iqt, iwt), h1, (ka, vat, c, ct, ik), (kam, vatm, cm, ctm, ikm),
                    tall, tmall, lamp, subln, wuvt, w_out[lyr].astype(BF16))

    ffn2 = (row(ffn2_norm[lyr]), ffn2_w_gate[lyr].astype(BF16), ffn2_w_up[lyr].astype(BF16),
            ffn2_w_down[lyr].astype(BF16))
    return _ffn(h2.reshape(bsz * seq, d), *ffn2, tm=FFN_TM).reshape(bsz, seq, d)
```

```python
import functools
import math

import jax
import jax.numpy as jnp
import numpy as np
from jax import lax
from jax.experimental import pallas as pl
from jax.experimental.pallas import tpu as pltpu

F32 = jnp.float32
BF16 = jnp.bfloat16

D_MODEL = 1024
N_META = 16
A_HEADS = 4
A_DK = 64
A_DV = 128
B_HEADS = 4
B_DQ = 128
B_DC = 256
B_DV = 128
IDX_HEADS = 8
IDX_DIM = 64
IDX_TOPK_MAX = 256
REL_BUCKETS = 32
REL_MAX_DIST = 128
D_FF = 2816
EPS = 1e-6
LOG2E = math.log2(math.e)
A_SCALE = A_DK ** -0.5
B_SCALE = B_DC ** -0.5
LAM_INIT = 0.8 - 0.6 * math.exp(-0.3 * 0)

NEG = -1e30
INT_MIN = -(2 ** 31)

TQ = 256
META_PAD = 128
FF_CHUNK = 1408
FFN_TM = 512
PROJ_TM = 512
V7X_VMEM_LIMIT = 56 * 1024 * 1024

A_LANES = 2 * TQ
PACK = 16
assert N_META % PACK == 0
_T_QA, _T_VA, _T_QB, _T_C, _T_IQ, _T_IW, _T_END = 0, 512, 1024, 1536, 1792, 2304, 2320
_R_KA, _R_C, _R_IK, _R_END = 0, 512, 768, 896


def _bucket_edges():
    max_exact = REL_BUCKETS // 2
    n = np.arange(0, 4 * REL_MAX_DIST)
    nf = np.maximum(n, 1).astype(np.float64)
    large = max_exact + (np.log(nf / max_exact) / math.log(REL_MAX_DIST / max_exact)
                         * (REL_BUCKETS - max_exact)).astype(np.int64)
    large = np.minimum(large, REL_BUCKETS - 1)
    bucket = np.where(n < max_exact, n, large)
    return [int(n[np.argmax(bucket >= j)]) for j in range(REL_BUCKETS)]


_EDGES = _bucket_edges()
assert _EDGES[-1] <= TQ, "tiles two or more away from the diagonal must sit in the last bucket"


def _const_spec(shape):
    nd = len(shape)
    return pl.BlockSpec(shape, lambda *_: (0,) * nd, pipeline_mode=pl.Buffered(1))


def _params(n_grid):
    return pltpu.CompilerParams(dimension_semantics=("arbitrary",) * n_grid,
                                vmem_limit_bytes=V7X_VMEM_LIMIT)


def _rms_lanes(x, g_row):
    ms = jnp.mean(x * x, axis=-1, keepdims=True)
    return x * lax.rsqrt(ms + EPS) * g_row


def _bias_kernel(rb_ref, tall_ref, tmall_ref):
    h = pl.program_id(0)

    def table(d):
        val = jnp.full(d.shape, rb_ref[0, h] * LOG2E, F32)
        for j in range(1, REL_BUCKETS):
            val = jnp.where(d >= _EDGES[j], rb_ref[j, h] * LOG2E, val)
        return val

    far = rb_ref[REL_BUCKETS - 1, h] * LOG2E
    c = lax.broadcasted_iota(jnp.int32, (TQ, TQ), 0)
    r = lax.broadcasted_iota(jnp.int32, (TQ, TQ), 1)
    d0 = r - c
    tall_ref[0, 0] = jnp.where(d0 >= 0, table(d0), NEG)
    tall_ref[0, 1] = table(d0 + TQ)
    tall_ref[0, 2] = jnp.full((TQ, TQ), far, F32)
    m = lax.broadcasted_iota(jnp.int32, (N_META, TQ), 0)
    r2 = lax.broadcasted_iota(jnp.int32, (N_META, TQ), 1)
    tmall_ref[0, 0] = table(r2 + N_META - m)
    tmall_ref[0, 1] = jnp.full((N_META, TQ), far, F32)


def _bias_tiles(rel_bias):
    nh = rel_bias.shape[1]
    return pl.pallas_call(
        _bias_kernel,
        grid=(nh,),
        in_specs=[pl.BlockSpec(memory_space=pltpu.SMEM)],
        out_specs=[pl.BlockSpec((1, 3, TQ, TQ), lambda h: (h, 0, 0, 0)),
                   pl.BlockSpec((1, 2, N_META, TQ), lambda h: (h, 0, 0, 0))],
        out_shape=[jax.ShapeDtypeStruct((nh, 3, TQ, TQ), F32),
                   jax.ShapeDtypeStruct((nh, 2, N_META, TQ), F32)],
        compiler_params=_params(1),
        name="bias_tiles",
    )(rel_bias)


def _ffn_kernel(x_ref, g_ref, wg_ref, wu_ref, wd_ref, o_ref):
    x = x_ref[...]
    xn = _rms_lanes(x, g_ref[...]).astype(BF16)
    y = None
    for c in range(D_FF // FF_CHUNK):
        sl = slice(c * FF_CHUNK, (c + 1) * FF_CHUNK)
        g = jnp.dot(xn, wg_ref[:, sl], preferred_element_type=F32)
        u = jnp.dot(xn, wu_ref[:, sl], preferred_element_type=F32)
        a = (g * jax.nn.sigmoid(g) * u).astype(BF16)
        part = jnp.dot(a, wd_ref[sl, :], preferred_element_type=F32)
        y = part if y is None else y + part
    o_ref[...] = x + 0.5 * y


def _ffn(h2d, g_row, wg, wu, wd, tm):
    m, d = h2d.shape
    return pl.pallas_call(
        _ffn_kernel,
        grid=(m // tm,),
        in_specs=[pl.BlockSpec((tm, d), lambda i: (i, 0)),
                  _const_spec((1, d)),
                  _const_spec(wg.shape), _const_spec(wu.shape), _const_spec(wd.shape)],
        out_specs=pl.BlockSpec((tm, d), lambda i: (i, 0)),
        out_shape=jax.ShapeDtypeStruct((m, d), F32),
        compiler_params=_params(1),
        name="ffn",
    )(h2d, g_row, wg, wu, wd)


def _inproj_kernel(h_ref, g_ref, wt_ref, wr_ref, wukt_ref, gq_ref, gk_ref, gql_ref, gcr_ref, gcc_ref,
                   qat_ref, ka_ref, vat_ref, qlt_ref, c_ref, ct_ref, iqt_ref, ik_ref, iwt_ref, *, tm, tw):
    xn = _rms_lanes(h_ref[0], g_ref[...]).astype(BF16)
    subtiles = [(t, slice(t * tw, (t + 1) * tw)) for t in range(tm // tw)]

    def proj_t(r0, r1):
        return lax.dot_general(wt_ref[r0:r1, :], xn, (((1,), (1,)), ((), ())), preferred_element_type=F32)

    def proj_r(c0, c1):
        return jnp.dot(xn, wr_ref[:, c0:c1], preferred_element_type=F32)

    ka = proj_r(_R_KA, _R_C)
    for hm in range(2 * A_HEADS):
        cols = slice(hm * A_DK, (hm + 1) * A_DK)
        ka_ref[0, :, cols] = _rms_lanes(ka[:, cols], gk_ref[...]).astype(BF16)

    qa = proj_t(_T_QA, _T_VA).reshape(2 * A_HEADS, A_DK, tm)
    qa = qa * lax.rsqrt(jnp.mean(qa * qa, axis=1, keepdims=True) + EPS)
    qat_ref[0] = (qa.reshape(2 * A_HEADS * A_DK, tm) * gq_ref[...]).astype(BF16)

    qb = proj_t(_T_QB, _T_C).astype(BF16)
    for h in range(B_HEADS):
        ql = jnp.dot(wukt_ref[h], qb[h * B_DQ:(h + 1) * B_DQ], preferred_element_type=F32)
        ql = (ql * lax.rsqrt(jnp.mean(ql * ql, axis=0, keepdims=True) + EPS) * gql_ref[...]).astype(BF16)
        for t, sl in subtiles:
            qlt_ref[0, t, :, h * tw:(h + 1) * tw] = ql[:, sl]

    c_ref[0] = _rms_lanes(proj_r(_R_C, _R_IK), gcr_ref[...]).astype(BF16)

    ct = proj_t(_T_C, _T_IQ)
    ct = (ct * lax.rsqrt(jnp.mean(ct * ct, axis=0, keepdims=True) + EPS) * gcc_ref[...]).astype(BF16)
    for t, sl in subtiles:
        ct_ref[0, t] = ct[:, sl]

    va = proj_t(_T_VA, _T_QB).astype(BF16)
    for t, sl in subtiles:
        vat_ref[0, t] = va[:, sl]

    iq = proj_t(_T_IQ, _T_END)
    for h in range(IDX_HEADS):
        iqh = iq[h * IDX_DIM:(h + 1) * IDX_DIM].astype(BF16)
        for t, sl in subtiles:
            iqt_ref[0, t, :, h * tw:(h + 1) * tw] = iqh[:, sl]
    iw0 = _T_IW - _T_IQ
    iwt_ref[0] = iq[iw0:iw0 + IDX_HEADS] * (IDX_HEADS ** -0.5 * IDX_DIM ** -0.5)

    ik_ref[0] = proj_r(_R_IK, _R_END)[:, 0:IDX_DIM].astype(BF16)


def _inproj(h3d, g_row, wt, wr, wukt, gq, gk, gql, gcr, gcc, tm, tw):
    b, s, d = h3d.shape
    nt = s // tw
    kern = functools.partial(_inproj_kernel, tm=tm, tw=tw)
    out_shape = [
        jax.ShapeDtypeStruct((b, 2 * A_HEADS * A_DK, s), BF16),
        jax.ShapeDtypeStruct((b, s, 2 * A_HEADS * A_DK), BF16),
        jax.ShapeDtypeStruct((b, nt, A_HEADS * A_DV, tw), BF16),
        jax.ShapeDtypeStruct((b, nt, B_DC, B_HEADS * tw), BF16),
        jax.ShapeDtypeStruct((b, s, B_DC), BF16),
        jax.ShapeDtypeStruct((b, nt, B_DC, tw), BF16),
        jax.ShapeDtypeStruct((b, nt, IDX_DIM, IDX_HEADS * tw), BF16),
        jax.ShapeDtypeStruct((b, s, IDX_DIM), BF16),
        jax.ShapeDtypeStruct((b, IDX_HEADS, s), F32),
    ]
    k = tm // tw
    out_specs = [
        pl.BlockSpec((1, 2 * A_HEADS * A_DK, tm), lambda bi, i: (bi, 0, i)),
        pl.BlockSpec((1, tm, 2 * A_HEADS * A_DK), lambda bi, i: (bi, i, 0)),
        pl.BlockSpec((1, k, A_HEADS * A_DV, tw), lambda bi, i: (bi, i, 0, 0)),
        pl.BlockSpec((1, k, B_DC, B_HEADS * tw), lambda bi, i: (bi, i, 0, 0)),
        pl.BlockSpec((1, tm, B_DC), lambda bi, i: (bi, i, 0)),
        pl.BlockSpec((1, k, B_DC, tw), lambda bi, i: (bi, i, 0, 0)),
        pl.BlockSpec((1, k, IDX_DIM, IDX_HEADS * tw), lambda bi, i: (bi, i, 0, 0)),
        pl.BlockSpec((1, tm, IDX_DIM), lambda bi, i: (bi, i, 0)),
        pl.BlockSpec((1, IDX_HEADS, tm), lambda bi, i: (bi, 0, i)),
    ]
    consts = (g_row, wt, wr, wukt, gq, gk, gql, gcr, gcc)
    return pl.pallas_call(
        kern,
        grid=(b, s // tm),
        in_specs=[pl.BlockSpec((1, tm, d), lambda bi, i: (bi, i, 0))] + [_const_spec(a.shape) for a in consts],
        out_specs=out_specs,
        out_shape=out_shape,
        compiler_params=_params(2),
        name="inproj",
    )(h3d, *consts)


def _key_to_f32(u):
    bits = u ^ (jnp.right_shift(u, 31) & 0x7FFFFFFF)
    return lax.bitcast_convert_type(bits, F32)


def _online(s_ref, lanes, m_ref, l_ref, acc_ref, v_lhs, first=False):
    if first:
        m_new = jnp.max(s_ref[...], axis=0, keepdims=True)
        m_ref[:, lanes] = m_new
        p = jnp.exp2(s_ref[...] - m_new)
        l_ref[:, lanes] = jnp.sum(p, axis=0, keepdims=True)
        acc_ref[:, lanes] = jnp.dot(v_lhs, p.astype(BF16), preferred_element_type=F32)
        return
    m_old = m_ref[:, lanes]
    m_new = jnp.maximum(m_old, jnp.max(s_ref[...], axis=0, keepdims=True))
    m_ref[:, lanes] = m_new
    alpha = jnp.exp2(m_old - m_new)
    p = jnp.exp2(s_ref[...] - m_new)
    l_ref[:, lanes] = alpha * l_ref[:, lanes] + jnp.sum(p, axis=0, keepdims=True)
    acc_ref[:, lanes] = alpha * acc_ref[:, lanes] + jnp.dot(v_lhs, p.astype(BF16),
                                                           preferred_element_type=F32)


def _attn_kernel(qat_ref, qlt_ref, iqt_ref, iwt_ref, h_ref,
                 ka_ref, vat_ref, c_ref, ct_ref, ik_ref,
                 kam_ref, vatm_ref, cm_ref, ctm_ref, ikm_ref,
                 tall_ref, tmall_ref, lamp_ref, subln_ref, wuvt_ref, wout_ref,
                 o_ref,
                 st_ref, stm_ref, hi_ref, him_ref, b1_ref, b1m_ref, b0_ref, b0m_ref,
                 qbd_ref, sa_ref, sd_ref, m_ref, l_ref, acc_ref, m2_ref, l2_ref, acc2_ref, ot_ref):
    i = pl.program_id(1)
    n_tiles = i + 1
    midx = jnp.minimum(i, 1)

    def tile_rows(j):
        return pl.ds(pl.multiple_of(j * TQ, TQ), TQ)

    zq = jnp.zeros((A_DK, TQ), BF16)
    for h in range(A_HEADS):
        q0 = qat_ref[0, (2 * h) * A_DK:(2 * h + 1) * A_DK, :]
        q1 = qat_ref[0, (2 * h + 1) * A_DK:(2 * h + 2) * A_DK, :]
        qbd_ref[h, 0:A_DK, 0:TQ] = q0
        qbd_ref[h, 0:A_DK, TQ:A_LANES] = zq
        qbd_ref[h, A_DK:2 * A_DK, 0:TQ] = zq
        qbd_ref[h, A_DK:2 * A_DK, TQ:A_LANES] = q1

    def a_tile(k_get, v_get, bias_get, first=False):
        rows = v_get(0).shape[1]
        for h in range(A_HEADS):
            bias = bias_get(h)
            sa_ref[h, 0:rows, :] = (jnp.dot(k_get(h), qbd_ref[h], preferred_element_type=F32)
                                    + jnp.concatenate([bias, bias], axis=1))
        for h in range(A_HEADS):
            _online(sa_ref.at[h, 0:rows, :], slice(h * A_LANES, (h + 1) * A_LANES),
                    m_ref, l_ref, acc_ref, v_get(h), first)

    qrel = i * TQ + lax.broadcasted_iota(jnp.int32, (1, TQ), 1)

    def idx_scores(ik_tile):
        d = jnp.dot(ik_tile, iqt_ref[0, 0], preferred_element_type=F32)
        acc = jnp.zeros((ik_tile.shape[0], TQ), F32)
        for h in range(IDX_HEADS):
            acc = acc + iwt_ref[0, h:h + 1, :] * jnp.maximum(d[:, h * TQ:(h + 1) * TQ], 0.0)
        return acc

    def store_digits(st, hi_dst, b1_dst, b0_dst):
        bits = lax.bitcast_convert_type(st, jnp.int32)
        key = bits ^ (jnp.right_shift(bits, 31) & 0x7FFFFFFF)
        hi_dst[...] = lax.bitcast_convert_type(bits & jnp.int32(-65536), F32).astype(BF16)
        b1_dst[...] = (jnp.right_shift(key, 8) & 255).astype(F32).astype(BF16)
        b0_dst[...] = (key & 255).astype(F32).astype(BF16)

    mrow = lax.broadcasted_iota(jnp.int32, (N_META, TQ), 0)
    trow = lax.broadcasted_iota(jnp.int32, (TQ, TQ), 0)

    a_tile(lambda h: kam_ref[0, 0:N_META, h * 2 * A_DK:(h + 1) * 2 * A_DK],
           lambda h: vatm_ref[0, 0, h * A_DV:(h + 1) * A_DV, 0:N_META],
           lambda h: tmall_ref[h, midx], first=True)
    sm = idx_scores(ikm_ref[0, 0:N_META, :])
    stm_ref[...] = sm
    store_digits(sm, him_ref, b1m_ref, b0m_ref)

    def ab_body(j, carry):
        kidx = jnp.minimum(i - j, 2)
        a_tile(lambda h: ka_ref[0, tile_rows(j), h * 2 * A_DK:(h + 1) * 2 * A_DK],
               lambda h: vat_ref[0, j, h * A_DV:(h + 1) * A_DV, :],
               lambda h: tall_ref[h, kidx])
        sc = idx_scores(ik_ref[0, tile_rows(j), :])
        sx = jnp.where(j * TQ + trow <= qrel, sc, -jnp.inf)
        st_ref[j] = sx
        store_digits(sx, hi_ref.at[j], b1_ref.at[j], b0_ref.at[j])
        return carry

    lax.fori_loop(0, n_tiles, ab_body, 0)

    lp = lamp_ref[...]
    lam = (jnp.exp(jnp.sum(lp[0:1] * lp[1:2], axis=-1, keepdims=True))
           - jnp.exp(jnp.sum(lp[2:3] * lp[3:4], axis=-1, keepdims=True)) + LAM_INIT)
    for h in range(A_HEADS):
        l0 = slice(h * A_LANES, h * A_LANES + TQ)
        l1 = slice(h * A_LANES + TQ, (h + 1) * A_LANES)
        o = acc_ref[:, l0] / l_ref[:, l0] - lam * (acc_ref[:, l1] / l_ref[:, l1])
        o = o * lax.rsqrt(jnp.mean(o * o, axis=0, keepdims=True) + EPS) * subln_ref[...]
        ot_ref[h * A_DV:(h + 1) * A_DV, :] = o.astype(BF16)

    kf = jnp.minimum(IDX_TOPK_MAX, qrel + (N_META + 1)).astype(F32)
    one, zero = jnp.full((), 1, BF16), jnp.full((), 0, BF16)

    def pk_count(meta_ref, tile_ref, pred, t_row):
        tb = jnp.broadcast_to(t_row.astype(BF16), (PACK, TQ))[None]

        def part(x):
            x3 = x.reshape(x.shape[0] // PACK, PACK, TQ)
            ind = jnp.where(pred(x3, tb), one, zero)
            acc = ind[0]
            for k in range(1, ind.shape[0]):
                acc = acc + ind[k]
            return acc.astype(F32)

        p = lax.fori_loop(0, n_tiles, lambda j, p: p + part(tile_ref[j]), part(meta_ref[...]))
        return jnp.sum(p, axis=0, keepdims=True)

    def pk_keep(meta_ref, tile_ref, sel_meta_ref, sel_tile_ref, t_row):
        tb = jnp.broadcast_to(t_row.astype(BF16), (1, TQ))
        meta_ref[...] = jnp.where(sel_meta_ref[...] == tb, meta_ref[...], -one)

        def body(j, carry):
            tile_ref[j] = jnp.where(sel_tile_ref[j] == tb, tile_ref[j], -one)
            return carry

        lax.fori_loop(0, n_tiles, body, 0)

    ge = lambda x, t: x >= t
    gt = lambda x, t: x > t

    def top_half_value(w):
        k = w - 32768
        b16 = (k ^ (jnp.right_shift(k, 15) & 0x7FFF)) & 0xFFFF
        return lax.bitcast_convert_type(jnp.left_shift(b16, 16), F32)

    def bisect(n_bits, to_value, meta_ref, tile_ref, need):
        def body(b, w):
            cand = w | jnp.left_shift(jnp.int32(1), n_bits - 1 - b)
            return jnp.where(pk_count(meta_ref, tile_ref, ge, to_value(cand)) >= need, cand, w)

        return lax.fori_loop(0, n_bits, body, jnp.zeros((1, TQ), jnp.int32))

    as_f32 = lambda d: d.astype(F32)
    w_hi = bisect(16, top_half_value, him_ref, hi_ref, kf)
    need1 = kf - pk_count(him_ref, hi_ref, gt, top_half_value(w_hi))
    pk_keep(b1m_ref, b1_ref, him_ref, hi_ref, top_half_value(w_hi))
    d1 = bisect(8, as_f32, b1m_ref, b1_ref, need1)
    need0 = need1 - pk_count(b1m_ref, b1_ref, gt, as_f32(d1))
    pk_keep(b0m_ref, b0_ref, b1m_ref, b1_ref, as_f32(d1))
    d0 = bisect(8, as_f32, b0m_ref, b0_ref, need0)
    u = jnp.left_shift(w_hi - 32768, 16) | jnp.left_shift(d1, 8) | d0
    thr = _key_to_f32(u)

    def count(ind_meta, ind_tile):
        p = ind_meta(stm_ref[...]).reshape(N_META // 8, 8, TQ).sum(axis=0)
        p = lax.fori_loop(0, n_tiles,
                          lambda j, p: p + ind_tile(st_ref[j], j).reshape(TQ // 8, 8, TQ).sum(axis=0), p)
        return jnp.sum(p, axis=0, keepdims=True)

    def count_ge(t):
        ind = lambda x: jnp.where(x >= t, 1.0, 0.0)
        return count(ind, lambda x, j: ind(x))

    def count_gt(t):
        ind = lambda x: jnp.where(x > t, 1.0, 0.0)
        return count(ind, lambda x, j: ind(x))

    need = kf - count_gt(thr)
    overflow = jnp.max(jnp.where(count_ge(thr) > kf, 1.0, 0.0)) > 0.0

    @pl.when(overflow)
    def _():
        def count_ties_before(xc):
            return count(lambda x: jnp.where((x == thr) & (mrow < xc), 1.0, 0.0),
                         lambda x, j: jnp.where((x == thr) & (j * TQ + N_META + trow < xc), 1.0, 0.0))

        def pos_body(b, xs):
            cand = xs | jnp.left_shift(jnp.int32(1), 11 - b)
            return jnp.where(count_ties_before(cand) < need, cand, xs)

        last = lax.fori_loop(0, 12, pos_body, jnp.zeros((1, TQ), jnp.int32))
        sm = stm_ref[...]
        stm_ref[...] = jnp.where((sm == thr) & (mrow > last), -jnp.inf, sm)

        def drop_body(j, carry):
            sx = st_ref[j]
            st_ref[j] = jnp.where((sx == thr) & (j * TQ + N_META + trow > last), -jnp.inf, sx)
            return carry

        lax.fori_loop(0, n_tiles, drop_body, 0)

    def d_tile(st_tile, c_tile, ct_tile, bias_get, first=False):
        sel = jnp.where(st_tile >= thr, 0.0, NEG)
        rows = c_tile.shape[0]
        for h in range(B_HEADS):
            sd_ref[h, 0:rows, :] = (jnp.dot(c_tile, qlt_ref[0, 0, :, h * TQ:(h + 1) * TQ],
                                            preferred_element_type=F32)
                                    + (bias_get(A_HEADS + h) + sel))
        for h in range(B_HEADS):
            _online(sd_ref.at[h, 0:rows, :], slice(h * TQ, (h + 1) * TQ), m2_ref, l2_ref, acc2_ref,
                    ct_tile, first)

    d_tile(stm_ref[...], cm_ref[0, 0:N_META, :], ctm_ref[0, 0, :, 0:N_META], lambda h: tmall_ref[h, midx],
           first=True)

    def d_body(j, carry):
        kidx = jnp.minimum(i - j, 2)
        d_tile(st_ref[j], c_ref[0, tile_rows(j), :], ct_ref[0, j], lambda h: tall_ref[h, kidx])
        return carry

    lax.fori_loop(0, n_tiles, d_body, 0)

    for h in range(B_HEADS):
        lanes = slice(h * TQ, (h + 1) * TQ)
        olat = (acc2_ref[:, lanes] / l2_ref[:, lanes]).astype(BF16)
        ob = jnp.dot(wuvt_ref[h], olat, preferred_element_type=F32)
        r0 = A_HEADS * A_DV + h * B_DV
        ot_ref[r0:r0 + B_DV, :] = ob.astype(BF16)

    y = lax.dot_general(ot_ref[...], wout_ref[...], (((0,), (0,)), ((), ())),
                        preferred_element_type=F32)
    o_ref[0] = h_ref[0] + y


def _attention(q_side, h3d, k_side, meta_side, tall, tmall, lamp, subln, wuvt, wout):
    qat, qlt, iqt, iwt = q_side
    ka, vat, c, ct, ik = k_side
    kam, vatm, cm, ctm, ikm = meta_side
    b, s, d = h3d.shape
    nt = s // TQ
    in_specs = [
        pl.BlockSpec((1, qat.shape[1], TQ), lambda bi, i: (bi, 0, i)),
        pl.BlockSpec((1, 1) + qlt.shape[2:], lambda bi, i: (bi, i, 0, 0)),
        pl.BlockSpec((1, 1) + iqt.shape[2:], lambda bi, i: (bi, i, 0, 0)),
        pl.BlockSpec((1, iwt.shape[1], TQ), lambda bi, i: (bi, 0, i)),
        pl.BlockSpec((1, TQ, d), lambda bi, i: (bi, i, 0)),
        pl.BlockSpec((1,) + ka.shape[1:], lambda bi, i: (bi, 0, 0)),
        pl.BlockSpec((1,) + vat.shape[1:], lambda bi, i: (bi, 0, 0, 0)),
        pl.BlockSpec((1,) + c.shape[1:], lambda bi, i: (bi, 0, 0)),
        pl.BlockSpec((1,) + ct.shape[1:], lambda bi, i: (bi, 0, 0, 0)),
        pl.BlockSpec((1,) + ik.shape[1:], lambda bi, i: (bi, 0, 0)),
    ] + [_const_spec(a.shape) for a in (kam, vatm, cm, ctm, ikm, tall, tmall, lamp, subln, wuvt, wout)]
    scratch = [
        pltpu.VMEM((nt, TQ, TQ), F32),
        pltpu.VMEM((N_META, TQ), F32),
        pltpu.VMEM((nt, TQ, TQ), BF16), pltpu.VMEM((N_META, TQ), BF16),
        pltpu.VMEM((nt, TQ, TQ), BF16), pltpu.VMEM((N_META, TQ), BF16),
        pltpu.VMEM((nt, TQ, TQ), BF16), pltpu.VMEM((N_META, TQ), BF16),
        pltpu.VMEM((A_HEADS, 2 * A_DK, A_LANES), BF16),
        pltpu.VMEM((A_HEADS, TQ, A_LANES), F32),
        pltpu.VMEM((B_HEADS, TQ, TQ), F32),
        pltpu.VMEM((1, A_HEADS * A_LANES), F32),
        pltpu.VMEM((1, A_HEADS * A_LANES), F32),
        pltpu.VMEM((A_DV, A_HEADS * A_LANES), F32),
        pltpu.VMEM((1, B_HEADS * TQ), F32),
        pltpu.VMEM((1, B_HEADS * TQ), F32),
        pltpu.VMEM((B_DC, B_HEADS * TQ), F32),
        pltpu.VMEM((A_HEADS * A_DV + B_HEADS * B_DV, TQ), BF16),
    ]
    return pl.pallas_call(
        _attn_kernel,
        grid=(b, nt),
        in_specs=in_specs,
        out_specs=pl.BlockSpec((1, TQ, d), lambda bi, i: (bi, i, 0)),
        out_shape=jax.ShapeDtypeStruct((b, s, d), F32),
        scratch_shapes=scratch,
        compiler_params=_params(2),
        name="attention",
    )(qat, qlt, iqt, iwt, h3d, ka, vat, c, ct, ik, kam, vatm, cm, ctm, ikm,
      tall, tmall, lamp, subln, wuvt, wout)


def kernel(x, meta_tokens, rel_bias, ffn1_norm, ffn1_w_gate, ffn1_w_up, ffn1_w_down, mix_norm, w_in, a_q_norm, a_k_norm, a_lambda_q1, a_lambda_k1, a_lambda_q2, a_lambda_k2, a_subln, b_kv_norm, b_w_uk, b_q_norm, b_w_uv, w_out, ffn2_norm, ffn2_w_gate, ffn2_w_up, ffn2_w_down):
    bsz, seq, d = x.shape
    assert d == D_MODEL and seq % PROJ_TM == 0 and (bsz * seq) % FFN_TM == 0
    assert min(IDX_TOPK_MAX, (seq + N_META) // 4) == IDX_TOPK_MAX
    assert ffn1_norm.shape[0] == 1, "single layer"
    lyr = 0
    row = lambda v: v.reshape(1, -1).astype(F32)
    col = lambda v: v.reshape(-1, 1).astype(F32)

    w = w_in[lyr]
    o_qa, o_ka, o_va, o_qb, o_c, o_iq, o_ik, o_iw = np.cumsum(
        [0, 2 * A_HEADS * A_DK, 2 * A_HEADS * A_DK, A_HEADS * A_DV, B_HEADS * B_DQ, B_DC,
         IDX_HEADS * IDX_DIM, IDX_DIM])[:8]
    w_qa, w_ka, w_va, w_qb = w[:, o_qa:o_ka], w[:, o_ka:o_va], w[:, o_va:o_qb], w[:, o_qb:o_c]
    w_c, w_iq, w_ik, w_iw = w[:, o_c:o_iq], w[:, o_iq:o_ik], w[:, o_ik:o_iw], w[:, o_iw:o_iw + IDX_HEADS]
    wt = jnp.concatenate([w_qa, w_va, w_qb, w_c, w_iq, w_iw,
                          jnp.zeros((d, _T_END - _T_IW - IDX_HEADS), w.dtype)], axis=1).T.astype(BF16)
    wr = jnp.concatenate([w_ka, w_c, w_ik, jnp.zeros((d, _R_END - _R_IK - IDX_DIM), w.dtype)],
                         axis=1).astype(BF16)
    wukt = jnp.transpose(b_w_uk[lyr], (0, 2, 1)).astype(BF16)
    wuvt = jnp.transpose(b_w_uv[lyr], (0, 2, 1)).astype(BF16)
    gq = col(jnp.tile(a_q_norm[lyr], 2 * A_HEADS)) * (A_SCALE * LOG2E)
    gk = row(a_k_norm[lyr])
    gql = col(b_q_norm[lyr]) * (B_SCALE * LOG2E)
    gcr, gcc = row(b_kv_norm[lyr]), col(b_kv_norm[lyr])
    subln = col(a_subln[lyr]) * (1.0 - LAM_INIT)
    lamp = jnp.stack([a_lambda_q1[lyr], a_lambda_k1[lyr], a_lambda_q2[lyr], a_lambda_k2[lyr]]).astype(F32)
    proj_consts = (row(mix_norm[lyr]), wt, wr, wukt, gq, gk, gql, gcr, gcc)

    tall, tmall = _bias_tiles(rel_bias.astype(F32))

    ffn1 = (row(ffn1_norm[lyr]), ffn1_w_gate[lyr].astype(BF16), ffn1_w_up[lyr].astype(BF16),
            ffn1_w_down[lyr].astype(BF16))
    h1 = _ffn(x.reshape(bsz * seq, d), *ffn1, tm=FFN_TM).reshape(bsz, seq, d)
    meta_pad = jnp.pad(meta_tokens.astype(x.dtype), ((0, META_PAD - N_META), (0, 0)))
    h1m = _ffn(meta_pad, *ffn1, tm=META_PAD).reshape(1, META_PAD, d)

    qat, ka, vat, qlt, c, ct, iqt, ik, iwt = _inproj(h1, *proj_consts, tm=PROJ_TM, tw=TQ)
    _, kam, vatm, _, cm, ctm, _, ikm, _ = _inproj(h1m, *proj_consts, tm=META_PAD, tw=META_PAD)

    h2 = _attention((qat, qlt, iqt, iwt), h1, (ka, vat, c, ct, ik), (kam, vatm, cm, ctm, ikm),
                    tall, tmall, lamp, subln, wuvt, w_out[lyr].astype(BF16))

    ffn2 = (row(ffn2_norm[lyr]), ffn2_w_gate[lyr].astype(BF16), ffn2_w_up[lyr].astype(BF16),
            ffn2_w_down[lyr].astype(BF16))
    return _ffn(h2.reshape(bsz * seq, d), *ffn2, tm=FFN_TM).reshape(bsz, seq, d)
```

```python
import functools
import math

import jax
import jax.numpy as jnp
import numpy as np
from jax import lax
from jax.experimental import pallas as pl
from jax.experimental.pallas import tpu as pltpu

F32 = jnp.float32
BF16 = jnp.bfloat16

D_MODEL = 1024
N_META = 16
A_HEADS = 4
A_DK = 64
A_DV = 128
B_HEADS = 4
B_DQ = 128
B_DC = 256
B_DV = 128
IDX_HEADS = 8
IDX_DIM = 64
IDX_TOPK_MAX = 256
REL_BUCKETS = 32
REL_MAX_DIST = 128
D_FF = 2816
EPS = 1e-6
LOG2E = math.log2(math.e)
A_SCALE = A_DK ** -0.5
B_SCALE = B_DC ** -0.5
LAM_INIT = 0.8 - 0.6 * math.exp(-0.3 * 0)

NEG = -1e30
INT_MIN = -(2 ** 31)

TQ = 256
META_PAD = 128
FF_CHUNK = 1408
FFN_TM = 512
PROJ_TM = 512
V7X_VMEM_LIMIT = 56 * 1024 * 1024

A_LANES = 2 * TQ
PACK = 16
assert N_META % PACK == 0
_T_QA, _T_VA, _T_QB, _T_C, _T_IQ, _T_IW, _T_END = 0, 512, 1024, 1536, 1792, 2304, 2320
_R_KA, _R_C, _R_IK, _R_END = 0, 512, 768, 896


def _bucket_edges():
    max_exact = REL_BUCKETS // 2
    n = np.arange(0, 4 * REL_MAX_DIST)
    nf = np.maximum(n, 1).astype(np.float64)
    large = max_exact + (np.log(nf / max_exact) / math.log(REL_MAX_DIST / max_exact)
                         * (REL_BUCKETS - max_exact)).astype(np.int64)
    large = np.minimum(large, REL_BUCKETS - 1)
    bucket = np.where(n < max_exact, n, large)
    return [int(n[np.argmax(bucket >= j)]) for j in range(REL_BUCKETS)]


_EDGES = _bucket_edges()
assert _EDGES[-1] <= TQ, "tiles two or more away from the diagonal must sit in the last bucket"


def _const_spec(shape):
    nd = len(shape)
    return pl.BlockSpec(shape, lambda *_: (0,) * nd, pipeline_mode=pl.Buffered(1))


def _params(n_grid):
    return pltpu.CompilerParams(dimension_semantics=("arbitrary",) * n_grid,
                                vmem_limit_bytes=V7X_VMEM_LIMIT)


def _rms_lanes(x, g_row):
    ms = jnp.mean(x * x, axis=-1, keepdims=True)
    return x * lax.rsqrt(ms + EPS) * g_row


def _bias_kernel(rb_ref, tall_ref, tmall_ref):
    h = pl.program_id(0)

    def table(d):
        val = jnp.full(d.shape, rb_ref[0, h] * LOG2E, F32)
        for j in range(1, REL_BUCKETS):
            val = jnp.where(d >= _EDGES[j], rb_ref[j, h] * LOG2E, val)
        return val

    far = rb_ref[REL_BUCKETS - 1, h] * LOG2E
    c = lax.broadcasted_iota(jnp.int32, (TQ, TQ), 0)
    r = lax.broadcasted_iota(jnp.int32, (TQ, TQ), 1)
    d0 = r - c
    tall_ref[0, 0] = jnp.where(d0 >= 0, table(d0), NEG)
    tall_ref[0, 1] = table(d0 + TQ)
    tall_ref[0, 2] = jnp.full((TQ, TQ), far, F32)
    m = lax.broadcasted_iota(jnp.int32, (N_META, TQ), 0)
    r2 = lax.broadcasted_iota(jnp.int32, (N_META, TQ), 1)
    tmall_ref[0, 0] = table(r2 + N_META - m)
    tmall_ref[0, 1] = jnp.full((N_META, TQ), far, F32)


def _bias_tiles(rel_bias):
    nh = rel_bias.shape[1]
    return pl.pallas_call(
        _bias_kernel,
        grid=(nh,),
        in_specs=[pl.BlockSpec(memory_space=pltpu.SMEM)],
        out_specs=[pl.BlockSpec((1, 3, TQ, TQ), lambda h: (h, 0, 0, 0)),
                   pl.BlockSpec((1, 2, N_META, TQ), lambda h: (h, 0, 0, 0))],
        out_shape=[jax.ShapeDtypeStruct((nh, 3, TQ, TQ), F32),
                   jax.ShapeDtypeStruct((nh, 2, N_META, TQ), F32)],
        compiler_params=_params(1),
        name="bias_tiles",
    )(rel_bias)


def _ffn_kernel(x_ref, g_ref, wg_ref, wu_ref, wd_ref, o_ref):
    x = x_ref[...]
    xn = _rms_lanes(x, g_ref[...]).astype(BF16)
    y = None
    for c in range(D_FF // FF_CHUNK):
        sl = slice(c * FF_CHUNK, (c + 1) * FF_CHUNK)
        g = jnp.dot(xn, wg_ref[:, sl], preferred_element_type=F32)
        u = jnp.dot(xn, wu_ref[:, sl], preferred_element_type=F32)
        a = (g * jax.nn.sigmoid(g) * u).astype(BF16)
        part = jnp.dot(a, wd_ref[sl, :], preferred_element_type=F32)
        y = part if y is None else y + part
    o_ref[...] = x + 0.5 * y


def _ffn(h2d, g_row, wg, wu, wd, tm):
    m, d = h2d.shape
    return pl.pallas_call(
        _ffn_kernel,
        grid=(m // tm,),
        in_specs=[pl.BlockSpec((tm, d), lambda i: (i, 0)),
                  _const_spec((1, d)),
                  _const_spec(wg.shape), _const_spec(wu.shape), _const_spec(wd.shape)],
        out_specs=pl.BlockSpec((tm, d), lambda i: (i, 0)),
        out_shape=jax.ShapeDtypeStruct((m, d), F32),
        compiler_params=_params(1),
        name="ffn",
    )(h2d, g_row, wg, wu, wd)


def _inproj_kernel(h_ref, g_ref, wt_ref, wr_ref, wukt_ref, gq_ref, gk_ref, gql_ref, gcr_ref, gcc_ref,
                   qat_ref, ka_ref, vat_ref, qlt_ref, c_ref, ct_ref, iqt_ref, ik_ref, iwt_ref, *, tm, tw):
    xn = _rms_lanes(h_ref[0], g_ref[...]).astype(BF16)
    subtiles = [(t, slice(t * tw, (t + 1) * tw)) for t in range(tm // tw)]

    def proj_t(r0, r1):
        return lax.dot_general(wt_ref[r0:r1, :], xn, (((1,), (1,)), ((), ())), preferred_element_type=F32)

    def proj_r(c0, c1):
        return jnp.dot(xn, wr_ref[:, c0:c1], preferred_element_type=F32)

    ka = proj_r(_R_KA, _R_C)
    for hm in range(2 * A_HEADS):
        cols = slice(hm * A_DK, (hm + 1) * A_DK)
        ka_ref[0, :, cols] = _rms_lanes(ka[:, cols], gk_ref[...]).astype(BF16)

    qa = proj_t(_T_QA, _T_VA).reshape(2 * A_HEADS, A_DK, tm)
    qa = qa * lax.rsqrt(jnp.mean(qa * qa, axis=1, keepdims=True) + EPS)
    qat_ref[0] = (qa.reshape(2 * A_HEADS * A_DK, tm) * gq_ref[...]).astype(BF16)

    qb = proj_t(_T_QB, _T_C).astype(BF16)
    for h in range(B_HEADS):
        ql = jnp.dot(wukt_ref[h], qb[h * B_DQ:(h + 1) * B_DQ], preferred_element_type=F32)
        ql = (ql * lax.rsqrt(jnp.mean(ql * ql, axis=0, keepdims=True) + EPS) * gql_ref[...]).astype(BF16)
        for t, sl in subtiles:
            qlt_ref[0, t, :, h * tw:(h + 1) * tw] = ql[:, sl]

    c_ref[0] = _rms_lanes(proj_r(_R_C, _R_IK), gcr_ref[...]).astype(BF16)

    ct = proj_t(_T_C, _T_IQ)
    ct = (ct * lax.rsqrt(jnp.mean(ct * ct, axis=0, keepdims=True) + EPS) * gcc_ref[...]).astype(BF16)
    for t, sl in subtiles:
        ct_ref[0, t] = ct[:, sl]

    va = proj_t(_T_VA, _T_QB).astype(BF16)
    for t, sl in subtiles:
        vat_ref[0, t] = va[:, sl]

    iq = proj_t(_T_IQ, _T_END)
    for h in range(IDX_HEADS):
        iqh = iq[h * IDX_DIM:(h + 1) * IDX_DIM].astype(BF16)
        for t, sl in subtiles:
            iqt_ref[0, t, :, h * tw:(h + 1) * tw] = iqh[:, sl]
    iw0 = _T_IW - _T_IQ
    iwt_ref[0] = iq[iw0:iw0 + IDX_HEADS] * (IDX_HEADS ** -0.5 * IDX_DIM ** -0.5)

    ik_ref[0] = proj_r(_R_IK, _R_END)[:, 0:IDX_DIM].astype(BF16)


def _inproj(h3d, g_row, wt, wr, wukt, gq, gk, gql, gcr, gcc, tm, tw):
    b, s, d = h3d.shape
    nt = s // tw
    kern = functools.partial(_inproj_kernel, tm=tm, tw=tw)
    out_shape = [
        jax.ShapeDtypeStruct((b, 2 * A_HEADS * A_DK, s), BF16),
        jax.ShapeDtypeStruct((b, s, 2 * A_HEADS * A_DK), BF16),
        jax.ShapeDtypeStruct((b, nt, A_HEADS * A_DV, tw), BF16),
        jax.ShapeDtypeStruct((b, nt, B_DC, B_HEADS * tw), BF16),
        jax.ShapeDtypeStruct((b, s, B_DC), BF16),
        jax.ShapeDtypeStruct((b, nt, B_DC, tw), BF16),
        jax.ShapeDtypeStruct((b, nt, IDX_DIM, IDX_HEADS * tw), BF16),
        jax.ShapeDtypeStruct((b, s, IDX_DIM), BF16),
        jax.ShapeDtypeStruct((b, IDX_HEADS, s), F32),
    ]
    k = tm // tw
    out_specs = [
        pl.BlockSpec((1, 2 * A_HEADS * A_DK, tm), lambda bi, i: (bi, 0, i)),
        pl.BlockSpec((1, tm, 2 * A_HEADS * A_DK), lambda bi, i: (bi, i, 0)),
        pl.BlockSpec((1, k, A_HEADS * A_DV, tw), lambda bi, i: (bi, i, 0, 0)),
        pl.BlockSpec((1, k, B_DC, B_HEADS * tw), lambda bi, i: (bi, i, 0, 0)),
        pl.BlockSpec((1, tm, B_DC), lambda bi, i: (bi, i, 0)),
        pl.BlockSpec((1, k, B_DC, tw), lambda bi, i: (bi, i, 0, 0)),
        pl.BlockSpec((1, k, IDX_DIM, IDX_HEADS * tw), lambda bi, i: (bi, i, 0, 0)),
        pl.BlockSpec((1, tm, IDX_DIM), lambda bi, i: (bi, i, 0)),
        pl.BlockSpec((1, IDX_HEADS, tm), lambda bi, i: (bi, 0, i)),
    ]
    consts = (g_row, wt, wr, wukt, gq, gk, gql, gcr, gcc)
    return pl.pallas_call(
        kern,
        grid=(b, s // tm),
        in_specs=[pl.BlockSpec((1, tm, d), lambda bi, i: (bi, i, 0))] + [_const_spec(a.shape) for a in consts],
        out_specs=out_specs,
        out_shape=out_shape,
        compiler_params=_params(2),
        name="inproj",
    )(h3d, *consts)


def _key_to_f32(u):
    bits = u ^ (jnp.right_shift(u, 31) & 0x7FFFFFFF)
    return lax.bitcast_convert_type(bits, F32)


def _online(s_ref, lanes, m_ref, l_ref, acc_ref, v_lhs, first=False):
    if first:
        m_new = jnp.max(s_ref[...], axis=0, keepdims=True)
        m_ref[:, lanes] = m_new
        p = jnp.exp2(s_ref[...] - m_new)
        l_ref[:, lanes] = jnp.sum(p, axis=0, keepdims=True)
        acc_ref[:, lanes] = jnp.dot(v_lhs, p.astype(BF16), preferred_element_type=F32)
        return
    m_old = m_ref[:, lanes]
    m_new = jnp.maximum(m_old, jnp.max(s_ref[...], axis=0, keepdims=True))
    m_ref[:, lanes] = m_new
    alpha = jnp.exp2(m_old - m_new)
    p = jnp.exp2(s_ref[...] - m_new)
    l_ref[:, lanes] = alpha * l_ref[:, lanes] + jnp.sum(p, axis=0, keepdims=True)
    acc_ref[:, lanes] = alpha * acc_ref[:, lanes] + jnp.dot(v_lhs, p.astype(BF16),
                                                           preferred_element_type=F32)


def _attn_kernel(qat_ref, qlt_ref, iqt_ref, iwt_ref, h_ref,
                 ka_ref, vat_ref, c_ref, ct_ref, ik_ref,
                 kam_ref, vatm_ref, cm_ref, ctm_ref, ikm_ref,
                 tall_ref, tmall_ref, lamp_ref, subln_ref, wuvt_ref, wout_ref,
                 o_ref,
                 st_ref, stm_ref, hi_ref, him_ref, b1_ref, b1m_ref, b0_ref, b0m_ref,
                 qbd_ref, sa_ref, sd_ref, m_ref, l_ref, acc_ref, m2_ref, l2_ref, acc2_ref, ot_ref):
    i = pl.program_id(1)
    n_tiles = i + 1
    midx = jnp.minimum(i, 1)

    zq = jnp.zeros((A_DK, TQ), BF16)
    for h in range(A_HEADS):
        q0 = qat_ref[0, (2 * h) * A_DK:(2 * h + 1) * A_DK, :]
        q1 = qat_ref[0, (2 * h + 1) * A_DK:(2 * h + 2) * A_DK, :]
        qbd_ref[h, 0:A_DK, 0:TQ] = q0
        qbd_ref[h, 0:A_DK, TQ:A_LANES] = zq
        qbd_ref[h, A_DK:2 * A_DK, 0:TQ] = zq
        qbd_ref[h, A_DK:2 * A_DK, TQ:A_LANES] = q1

    def a_tile(k_get, v_get, bias_get, first=False):
        rows = v_get(0).shape[1]
        for h in range(A_HEADS):
            bias = bias_get(h)
            sa_ref[h, 0:rows, :] = (jnp.dot(k_get(h), qbd_ref[h], preferred_element_type=F32)
                                    + jnp.concatenate([bias, bias], axis=1))
        for h in range(A_HEADS):
            _online(sa_ref.at[h, 0:rows, :], slice(h * A_LANES, (h + 1) * A_LANES),
                    m_ref, l_ref, acc_ref, v_get(h), first)

    qrel = i * TQ + lax.broadcasted_iota(jnp.int32, (1, TQ), 1)

    def idx_scores(ik_tile):
        d = jnp.dot(ik_tile, iqt_ref[0, 0], preferred_element_type=F32)
        acc = jnp.zeros((ik_tile.shape[0], TQ), F32)
        for h in range(IDX_HEADS):
            acc = acc + iwt_ref[0, h:h + 1, :] * jnp.maximum(d[:, h * TQ:(h + 1) * TQ], 0.0)
        return acc

    def store_digits(st, hi_dst, b1_dst, b0_dst):
        bits = lax.bitcast_convert_type(st, jnp.int32)
        key = bits ^ (jnp.right_shift(bits, 31) & 0x7FFFFFFF)
        hi_dst[...] = lax.bitcast_convert_type(bits & jnp.int32(-65536), F32).astype(BF16)
        b1_dst[...] = (jnp.right_shift(key, 8) & 255).astype(F32).astype(BF16)
        b0_dst[...] = (key & 255).astype(F32).astype(BF16)

    mrow = lax.broadcasted_iota(jnp.int32, (N_META, TQ), 0)
    trow = lax.broadcasted_iota(jnp.int32, (TQ, TQ), 0)

    a_tile(lambda h: kam_ref[0, 0:N_META, h * 2 * A_DK:(h + 1) * 2 * A_DK],
           lambda h: vatm_ref[0, 0, h * A_DV:(h + 1) * A_DV, 0:N_META],
           lambda h: tmall_ref[h, midx], first=True)
    sm = idx_scores(ikm_ref[0, 0:N_META, :])
    stm_ref[...] = sm
    store_digits(sm, him_ref, b1m_ref, b0m_ref)

    n_pairs = jnp.right_shift(n_tiles, 1)
    has_tail = (n_tiles & 1) == 1
    cat = lambda parts, axis: parts[0] if len(parts) == 1 else jnp.concatenate(parts, axis=axis)

    def key_rows(js):
        return pl.ds(pl.multiple_of(js[0] * TQ, TQ), len(js) * TQ)

    def bias_rows(js):
        kidx = [jnp.minimum(i - j, 2) for j in js]
        return lambda h: cat([tall_ref[h, k] for k in kidx], 0)

    def ab_tiles(js):
        rows = key_rows(js)
        a_tile(lambda h: ka_ref[0, rows, h * 2 * A_DK:(h + 1) * 2 * A_DK],
               lambda h: cat([vat_ref[0, j, h * A_DV:(h + 1) * A_DV, :] for j in js], 1),
               bias_rows(js))
        sc = idx_scores(ik_ref[0, rows, :])
        for t, j in enumerate(js):
            sx = jnp.where(j * TQ + trow <= qrel, sc[t * TQ:(t + 1) * TQ], -jnp.inf)
            st_ref[j] = sx
            store_digits(sx, hi_ref.at[j], b1_ref.at[j], b0_ref.at[j])

    def ab_body(p, carry):
        ab_tiles([2 * p, 2 * p + 1])
        return carry

    lax.fori_loop(0, n_pairs, ab_body, 0)

    @pl.when(has_tail)
    def _():
        ab_tiles([n_tiles - 1])

    lp = lamp_ref[...]
    lam = (jnp.exp(jnp.sum(lp[0:1] * lp[1:2], axis=-1, keepdims=True))
           - jnp.exp(jnp.sum(lp[2:3] * lp[3:4], axis=-1, keepdims=True)) + LAM_INIT)
    for h in range(A_HEADS):
        l0 = slice(h * A_LANES, h * A_LANES + TQ)
        l1 = slice(h * A_LANES + TQ, (h + 1) * A_LANES)
        o = acc_ref[:, l0] / l_ref[:, l0] - lam * (acc_ref[:, l1] / l_ref[:, l1])
        o = o * lax.rsqrt(jnp.mean(o * o, axis=0, keepdims=True) + EPS) * subln_ref[...]
        ot_ref[h * A_DV:(h + 1) * A_DV, :] = o.astype(BF16)

    kf = jnp.minimum(IDX_TOPK_MAX, qrel + (N_META + 1)).astype(F32)
    one, zero = jnp.full((), 1, BF16), jnp.full((), 0, BF16)

    def pk_count(meta_ref, tile_ref, pred, t_row):
        tb = jnp.broadcast_to(t_row.astype(BF16), (PACK, TQ))[None]

        def part(x):
            x3 = x.reshape(x.shape[0] // PACK, PACK, TQ)
            ind = jnp.where(pred(x3, tb), one, zero)
            acc = ind[0]
            for k in range(1, ind.shape[0]):
                acc = acc + ind[k]
            return acc.astype(F32)

        p = lax.fori_loop(0, n_tiles, lambda j, p: p + part(tile_ref[j]), part(meta_ref[...]))
        return jnp.sum(p, axis=0, keepdims=True)

    def pk_keep(meta_ref, tile_ref, sel_meta_ref, sel_tile_ref, t_row):
        tb = jnp.broadcast_to(t_row.astype(BF16), (1, TQ))
        meta_ref[...] = jnp.where(sel_meta_ref[...] == tb, meta_ref[...], -one)

        def body(j, carry):
            tile_ref[j] = jnp.where(sel_tile_ref[j] == tb, tile_ref[j], -one)
            return carry

        lax.fori_loop(0, n_tiles, body, 0)

    ge = lambda x, t: x >= t
    gt = lambda x, t: x > t

    def top_half_value(w):
        k = w - 32768
        b16 = (k ^ (jnp.right_shift(k, 15) & 0x7FFF)) & 0xFFFF
        return lax.bitcast_convert_type(jnp.left_shift(b16, 16), F32)

    def bisect(n_bits, to_value, meta_ref, tile_ref, need):
        def body(b, w):
            cand = w | jnp.left_shift(jnp.int32(1), n_bits - 1 - b)
            return jnp.where(pk_count(meta_ref, tile_ref, ge, to_value(cand)) >= need, cand, w)

        return lax.fori_loop(0, n_bits, body, jnp.zeros((1, TQ), jnp.int32))

    as_f32 = lambda d: d.astype(F32)
    w_hi = bisect(16, top_half_value, him_ref, hi_ref, kf)
    need1 = kf - pk_count(him_ref, hi_ref, gt, top_half_value(w_hi))
    pk_keep(b1m_ref, b1_ref, him_ref, hi_ref, top_half_value(w_hi))
    d1 = bisect(8, as_f32, b1m_ref, b1_ref, need1)
    need0 = need1 - pk_count(b1m_ref, b1_ref, gt, as_f32(d1))
    pk_keep(b0m_ref, b0_ref, b1m_ref, b1_ref, as_f32(d1))
    d0 = bisect(8, as_f32, b0m_ref, b0_ref, need0)
    u = jnp.left_shift(w_hi - 32768, 16) | jnp.left_shift(d1, 8) | d0
    thr = _key_to_f32(u)

    def count(ind_meta, ind_tile):
        p = ind_meta(stm_ref[...]).reshape(N_META // 8, 8, TQ).sum(axis=0)
        p = lax.fori_loop(0, n_tiles,
                          lambda j, p: p + ind_tile(st_ref[j], j).reshape(TQ // 8, 8, TQ).sum(axis=0), p)
        return jnp.sum(p, axis=0, keepdims=True)

    def count_ge(t):
        ind = lambda x: jnp.where(x >= t, 1.0, 0.0)
        return count(ind, lambda x, j: ind(x))

    def count_gt(t):
        ind = lambda x: jnp.where(x > t, 1.0, 0.0)
        return count(ind, lambda x, j: ind(x))

    need = kf - count_gt(thr)
    overflow = jnp.max(jnp.where(count_ge(thr) > kf, 1.0, 0.0)) > 0.0

    @pl.when(overflow)
    def _():
        def count_ties_before(xc):
            return count(lambda x: jnp.where((x == thr) & (mrow < xc), 1.0, 0.0),
                         lambda x, j: jnp.where((x == thr) & (j * TQ + N_META + trow < xc), 1.0, 0.0))

        def pos_body(b, xs):
            cand = xs | jnp.left_shift(jnp.int32(1), 11 - b)
            return jnp.where(count_ties_before(cand) < need, cand, xs)

        last = lax.fori_loop(0, 12, pos_body, jnp.zeros((1, TQ), jnp.int32))
        sm = stm_ref[...]
        stm_ref[...] = jnp.where((sm == thr) & (mrow > last), -jnp.inf, sm)

        def drop_body(j, carry):
            sx = st_ref[j]
            st_ref[j] = jnp.where((sx == thr) & (j * TQ + N_META + trow > last), -jnp.inf, sx)
            return carry

        lax.fori_loop(0, n_tiles, drop_body, 0)

    def d_tile(st_tile, c_tile, ct_tile, bias_get, first=False):
        sel = jnp.where(st_tile >= thr, 0.0, NEG)
        rows = c_tile.shape[0]
        for h in range(B_HEADS):
            sd_ref[h, 0:rows, :] = (jnp.dot(c_tile, qlt_ref[0, 0, :, h * TQ:(h + 1) * TQ],
                                            preferred_element_type=F32)
                                    + (bias_get(A_HEADS + h) + sel))
        for h in range(B_HEADS):
            _online(sd_ref.at[h, 0:rows, :], slice(h * TQ, (h + 1) * TQ), m2_ref, l2_ref, acc2_ref,
                    ct_tile, first)

    d_tile(stm_ref[...], cm_ref[0, 0:N_META, :], ctm_ref[0, 0, :, 0:N_META], lambda h: tmall_ref[h, midx],
           first=True)

    def d_tiles(js):
        d_tile(cat([st_ref[j] for j in js], 0), c_ref[0, key_rows(js), :],
               cat([ct_ref[0, j] for j in js], 1), bias_rows(js))

    def d_body(p, carry):
        d_tiles([2 * p, 2 * p + 1])
        return carry

    lax.fori_loop(0, n_pairs, d_body, 0)

    @pl.when(has_tail)
    def _():
        d_tiles([n_tiles - 1])

    for h in range(B_HEADS):
        lanes = slice(h * TQ, (h + 1) * TQ)
        olat = (acc2_ref[:, lanes] / l2_ref[:, lanes]).astype(BF16)
        ob = jnp.dot(wuvt_ref[h], olat, preferred_element_type=F32)
        r0 = A_HEADS * A_DV + h * B_DV
        ot_ref[r0:r0 + B_DV, :] = ob.astype(BF16)

    y = lax.dot_general(ot_ref[...], wout_ref[...], (((0,), (0,)), ((), ())),
                        preferred_element_type=F32)
    o_ref[0] = h_ref[0] + y


def _attention(q_side, h3d, k_side, meta_side, tall, tmall, lamp, subln, wuvt, wout):
    qat, qlt, iqt, iwt = q_side
    ka, vat, c, ct, ik = k_side
    kam, vatm, cm, ctm, ikm = meta_side
    b, s, d = h3d.shape
    nt = s // TQ
    in_specs = [
        pl.BlockSpec((1, qat.shape[1], TQ), lambda bi, i: (bi, 0, i)),
        pl.BlockSpec((1, 1) + qlt.shape[2:], lambda bi, i: (bi, i, 0, 0)),
        pl.BlockSpec((1, 1) + iqt.shape[2:], lambda bi, i: (bi, i, 0, 0)),
        pl.BlockSpec((1, iwt.shape[1], TQ), lambda bi, i: (bi, 0, i)),
        pl.BlockSpec((1, TQ, d), lambda bi, i: (bi, i, 0)),
        pl.BlockSpec((1,) + ka.shape[1:], lambda bi, i: (bi, 0, 0)),
        pl.BlockSpec((1,) + vat.shape[1:], lambda bi, i: (bi, 0, 0, 0)),
        pl.BlockSpec((1,) + c.shape[1:], lambda bi, i: (bi, 0, 0)),
        pl.BlockSpec((1,) + ct.shape[1:], lambda bi, i: (bi, 0, 0, 0)),
        pl.BlockSpec((1,) + ik.shape[1:], lambda bi, i: (bi, 0, 0)),
    ] + [_const_spec(a.shape) for a in (kam, vatm, cm, ctm, ikm, tall, tmall, lamp, subln, wuvt, wout)]
    scratch = [
        pltpu.VMEM((nt, TQ, TQ), F32),
        pltpu.VMEM((N_META, TQ), F32),
        pltpu.VMEM((nt, TQ, TQ), BF16), pltpu.VMEM((N_META, TQ), BF16),
        pltpu.VMEM((nt, TQ, TQ), BF16), pltpu.VMEM((N_META, TQ), BF16),
        pltpu.VMEM((nt, TQ, TQ), BF16), pltpu.VMEM((N_META, TQ), BF16),
        pltpu.VMEM((A_HEADS, 2 * A_DK, A_LANES), BF16),
        pltpu.VMEM((A_HEADS, 2 * TQ, A_LANES), F32),
        pltpu.VMEM((B_HEADS, 2 * TQ, TQ), F32),
        pltpu.VMEM((1, A_HEADS * A_LANES), F32),
        pltpu.VMEM((1, A_HEADS * A_LANES), F32),
        pltpu.VMEM((A_DV, A_HEADS * A_LANES), F32),
        pltpu.VMEM((1, B_HEADS * TQ), F32),
        pltpu.VMEM((1, B_HEADS * TQ), F32),
        pltpu.VMEM((B_DC, B_HEADS * TQ), F32),
        pltpu.VMEM((A_HEADS * A_DV + B_HEADS * B_DV, TQ), BF16),
    ]
    return pl.pallas_call(
        _attn_kernel,
        grid=(b, nt),
        in_specs=in_specs,
        out_specs=pl.BlockSpec((1, TQ, d), lambda bi, i: (bi, i, 0)),
        out_shape=jax.ShapeDtypeStruct((b, s, d), F32),
        scratch_shapes=scratch,
        compiler_params=_params(2),
        name="attention",
    )(qat, qlt, iqt, iwt, h3d, ka, vat, c, ct, ik, kam, vatm, cm, ctm, ikm,
      tall, tmall, lamp, subln, wuvt, wout)


def kernel(x, meta_tokens, rel_bias, ffn1_norm, ffn1_w_gate, ffn1_w_up, ffn1_w_down, mix_norm, w_in, a_q_norm, a_k_norm, a_lambda_q1, a_lambda_k1, a_lambda_q2, a_lambda_k2, a_subln, b_kv_norm, b_w_uk, b_q_norm, b_w_uv, w_out, ffn2_norm, ffn2_w_gate, ffn2_w_up, ffn2_w_down):
    bsz, seq, d = x.shape
    assert d == D_MODEL and seq % PROJ_TM == 0 and (bsz * seq) % FFN_TM == 0
    assert min(IDX_TOPK_MAX, (seq + N_META) // 4) == IDX_TOPK_MAX
    assert ffn1_norm.shape[0] == 1, "single layer"
    lyr = 0
    row = lambda v: v.reshape(1, -1).astype(F32)
    col = lambda v: v.reshape(-1, 1).astype(F32)

    w = w_in[lyr]
    o_qa, o_ka, o_va, o_qb, o_c, o_iq, o_ik, o_iw = np.cumsum(
        [0, 2 * A_HEADS * A_DK, 2 * A_HEADS * A_DK, A_HEADS * A_DV, B_HEADS * B_DQ, B_DC,
         IDX_HEADS * IDX_DIM, IDX_DIM])[:8]
    w_qa, w_ka, w_va, w_qb = w[:, o_qa:o_ka], w[:, o_ka:o_va], w[:, o_va:o_qb], w[:, o_qb:o_c]
    w_c, w_iq, w_ik, w_iw = w[:, o_c:o_iq], w[:, o_iq:o_ik], w[:, o_ik:o_iw], w[:, o_iw:o_iw + IDX_HEADS]
    wt = jnp.concatenate([w_qa, w_va, w_qb, w_c, w_iq, w_iw,
                          jnp.zeros((d, _T_END - _T_IW - IDX_HEADS), w.dtype)], axis=1).T.astype(BF16)
    wr = jnp.concatenate([w_ka, w_c, w_ik, jnp.zeros((d, _R_END - _R_IK - IDX_DIM), w.dtype)],
                         axis=1).astype(BF16)
    wukt = jnp.transpose(b_w_uk[lyr], (0, 2, 1)).astype(BF16)
    wuvt = jnp.transpose(b_w_uv[lyr], (0, 2, 1)).astype(BF16)
    gq = col(jnp.tile(a_q_norm[lyr], 2 * A_HEADS)) * (A_SCALE * LOG2E)
    gk = row(a_k_norm[lyr])
    gql = col(b_q_norm[lyr]) * (B_SCALE * LOG2E)
    gcr, gcc = row(b_kv_norm[lyr]), col(b_kv_norm[lyr])
    subln = col(a_subln[lyr]) * (1.0 - LAM_INIT)
    lamp = jnp.stack([a_lambda_q1[lyr], a_lambda_k1[lyr], a_lambda_q2[lyr], a_lambda_k2[lyr]]).astype(F32)
    proj_consts = (row(mix_norm[lyr]), wt, wr, wukt, gq, gk, gql, gcr, gcc)

    tall, tmall = _bias_tiles(rel_bias.astype(F32))

    ffn1 = (row(ffn1_norm[lyr]), ffn1_w_gate[lyr].astype(BF16), ffn1_w_up[lyr].astype(BF16),
            ffn1_w_down[lyr].astype(BF16))
    h1 = _ffn(x.reshape(bsz * seq, d), *ffn1, tm=FFN_TM).reshape(bsz, seq, d)
    meta_pad = jnp.pad(meta_tokens.astype(x.dtype), ((0, META_PAD - N_META), (0, 0)))
    h1m = _ffn(meta_pad, *ffn1, tm=META_PAD).reshape(1, META_PAD, d)

    qat, ka, vat, qlt, c, ct, iqt, ik, iwt = _inproj(h1, *proj_consts, tm=PROJ_TM, tw=TQ)
    _, kam, vatm, _, cm, ctm, _, ikm, _ = _inproj(h1m, *proj_consts, tm=META_PAD, tw=META_PAD)

    h2 = _attention((qat, qlt, iqt, iwt), h1, (ka, vat, c, ct, ik), (kam, vatm, cm, ctm, ikm),
                    tall, tmall, lamp, subln, wuvt, w_out[lyr].astype(BF16))

    ffn2 = (row(ffn2_norm[lyr]), ffn2_w_gate[lyr].astype(BF16), ffn2_w_up[lyr].astype(BF16),
            ffn2_w_down[lyr].astype(BF16))
    return _ffn(h2.reshape(bsz * seq, d), *ffn2, tm=FFN_TM).reshape(bsz, seq, d)
```

```python
import functools
import math

import jax
import jax.numpy as jnp
import numpy as np
from jax import lax
from jax.experimental import pallas as pl
from jax.experimental.pallas import tpu as pltpu

F32 = jnp.float32
BF16 = jnp.bfloat16

D_MODEL = 1024
N_META = 16
A_HEADS = 4
A_DK = 64
A_DV = 128
B_HEADS = 4
B_DQ = 128
B_DC = 256
B_DV = 128
IDX_HEADS = 8
IDX_DIM = 64
IDX_TOPK_MAX = 256
REL_BUCKETS = 32
REL_MAX_DIST = 128
D_FF = 2816
EPS = 1e-6
LOG2E = math.log2(math.e)
A_SCALE = A_DK ** -0.5
B_SCALE = B_DC ** -0.5
LAM_INIT = 0.8 - 0.6 * math.exp(-0.3 * 0)

NEG = -1e30
INT_MIN = -(2 ** 31)

TQ = 256
META_PAD = 128
FF_CHUNK = 1408
FFN_TM = 512
PROJ_TM = 512
V7X_VMEM_LIMIT = 56 * 1024 * 1024

KEY_TILES_PER_STEP = 4
A_LANES = 2 * TQ
PACK = 16
assert N_META % PACK == 0
_T_QA, _T_VA, _T_QB, _T_C, _T_IQ, _T_IW, _T_END = 0, 512, 1024, 1536, 1792, 2304, 2320
_R_KA, _R_C, _R_IK, _R_END = 0, 512, 768, 896


def _bucket_edges():
    max_exact = REL_BUCKETS // 2
    n = np.arange(0, 4 * REL_MAX_DIST)
    nf = np.maximum(n, 1).astype(np.float64)
    large = max_exact + (np.log(nf / max_exact) / math.log(REL_MAX_DIST / max_exact)
                         * (REL_BUCKETS - max_exact)).astype(np.int64)
    large = np.minimum(large, REL_BUCKETS - 1)
    bucket = np.where(n < max_exact, n, large)
    return [int(n[np.argmax(bucket >= j)]) for j in range(REL_BUCKETS)]


_EDGES = _bucket_edges()
assert _EDGES[-1] <= TQ, "tiles two or more away from the diagonal must sit in the last bucket"


def _const_spec(shape):
    nd = len(shape)
    return pl.BlockSpec(shape, lambda *_: (0,) * nd, pipeline_mode=pl.Buffered(1))


def _params(n_grid):
    return pltpu.CompilerParams(dimension_semantics=("arbitrary",) * n_grid,
                                vmem_limit_bytes=V7X_VMEM_LIMIT)


def _rms_lanes(x, g_row):
    ms = jnp.mean(x * x, axis=-1, keepdims=True)
    return x * lax.rsqrt(ms + EPS) * g_row


def _bias_kernel(rb_ref, tall_ref, tmall_ref):
    h = pl.program_id(0)

    def table(d):
        val = jnp.full(d.shape, rb_ref[0, h] * LOG2E, F32)
        for j in range(1, REL_BUCKETS):
            val = jnp.where(d >= _EDGES[j], rb_ref[j, h] * LOG2E, val)
        return val

    far = rb_ref[REL_BUCKETS - 1, h] * LOG2E
    c = lax.broadcasted_iota(jnp.int32, (TQ, TQ), 0)
    r = lax.broadcasted_iota(jnp.int32, (TQ, TQ), 1)
    d0 = r - c
    tall_ref[0, 0] = jnp.where(d0 >= 0, table(d0), NEG)
    tall_ref[0, 1] = table(d0 + TQ)
    tall_ref[0, 2] = jnp.full((TQ, TQ), far, F32)
    m = lax.broadcasted_iota(jnp.int32, (N_META, TQ), 0)
    r2 = lax.broadcasted_iota(jnp.int32, (N_META, TQ), 1)
    tmall_ref[0, 0] = table(r2 + N_META - m)
    tmall_ref[0, 1] = jnp.full((N_META, TQ), far, F32)


def _bias_tiles(rel_bias):
    nh = rel_bias.shape[1]
    return pl.pallas_call(
        _bias_kernel,
        grid=(nh,),
        in_specs=[pl.BlockSpec(memory_space=pltpu.SMEM)],
        out_specs=[pl.BlockSpec((1, 3, TQ, TQ), lambda h: (h, 0, 0, 0)),
                   pl.BlockSpec((1, 2, N_META, TQ), lambda h: (h, 0, 0, 0))],
        out_shape=[jax.ShapeDtypeStruct((nh, 3, TQ, TQ), F32),
                   jax.ShapeDtypeStruct((nh, 2, N_META, TQ), F32)],
        compiler_params=_params(1),
        name="bias_tiles",
    )(rel_bias)


def _ffn_kernel(x_ref, g_ref, wg_ref, wu_ref, wd_ref, o_ref):
    x = x_ref[...]
    xn = _rms_lanes(x, g_ref[...]).astype(BF16)
    y = None
    for c in range(D_FF // FF_CHUNK):
        sl = slice(c * FF_CHUNK, (c + 1) * FF_CHUNK)
        g = jnp.dot(xn, wg_ref[:, sl], preferred_element_type=F32)
        u = jnp.dot(xn, wu_ref[:, sl], preferred_element_type=F32)
        a = (g * jax.nn.sigmoid(g) * u).astype(BF16)
        part = jnp.dot(a, wd_ref[sl, :], preferred_element_type=F32)
        y = part if y is None else y + part
    o_ref[...] = x + 0.5 * y


def _ffn(h2d, g_row, wg, wu, wd, tm):
    m, d = h2d.shape
    return pl.pallas_call(
        _ffn_kernel,
        grid=(m // tm,),
        in_specs=[pl.BlockSpec((tm, d), lambda i: (i, 0)),
                  _const_spec((1, d)),
                  _const_spec(wg.shape), _const_spec(wu.shape), _const_spec(wd.shape)],
        out_specs=pl.BlockSpec((tm, d), lambda i: (i, 0)),
        out_shape=jax.ShapeDtypeStruct((m, d), F32),
        compiler_params=_params(1),
        name="ffn",
    )(h2d, g_row, wg, wu, wd)


def _inproj_kernel(h_ref, g_ref, wt_ref, wr_ref, wukt_ref, gq_ref, gk_ref, gql_ref, gcr_ref, gcc_ref,
                   qat_ref, ka_ref, vat_ref, qlt_ref, c_ref, ct_ref, iqt_ref, ik_ref, iwt_ref, *, tm, tw):
    xn = _rms_lanes(h_ref[0], g_ref[...]).astype(BF16)
    subtiles = [(t, slice(t * tw, (t + 1) * tw)) for t in range(tm // tw)]

    def proj_t(r0, r1):
        return lax.dot_general(wt_ref[r0:r1, :], xn, (((1,), (1,)), ((), ())), preferred_element_type=F32)

    def proj_r(c0, c1):
        return jnp.dot(xn, wr_ref[:, c0:c1], preferred_element_type=F32)

    ka = proj_r(_R_KA, _R_C)
    for hm in range(2 * A_HEADS):
        cols = slice(hm * A_DK, (hm + 1) * A_DK)
        ka_ref[0, :, cols] = _rms_lanes(ka[:, cols], gk_ref[...]).astype(BF16)

    qa = proj_t(_T_QA, _T_VA).reshape(2 * A_HEADS, A_DK, tm)
    qa = qa * lax.rsqrt(jnp.mean(qa * qa, axis=1, keepdims=True) + EPS)
    qat_ref[0] = (qa.reshape(2 * A_HEADS * A_DK, tm) * gq_ref[...]).astype(BF16)

    qb = proj_t(_T_QB, _T_C).astype(BF16)
    for h in range(B_HEADS):
        ql = jnp.dot(wukt_ref[h], qb[h * B_DQ:(h + 1) * B_DQ], preferred_element_type=F32)
        ql = (ql * lax.rsqrt(jnp.mean(ql * ql, axis=0, keepdims=True) + EPS) * gql_ref[...]).astype(BF16)
        for t, sl in subtiles:
            qlt_ref[0, t, :, h * tw:(h + 1) * tw] = ql[:, sl]

    c_ref[0] = _rms_lanes(proj_r(_R_C, _R_IK), gcr_ref[...]).astype(BF16)

    ct = proj_t(_T_C, _T_IQ)
    ct = (ct * lax.rsqrt(jnp.mean(ct * ct, axis=0, keepdims=True) + EPS) * gcc_ref[...]).astype(BF16)
    for t, sl in subtiles:
        ct_ref[0, t] = ct[:, sl]

    va = proj_t(_T_VA, _T_QB).astype(BF16)
    for t, sl in subtiles:
        vat_ref[0, t] = va[:, sl]

    iq = proj_t(_T_IQ, _T_END)
    for h in range(IDX_HEADS):
        iqh = iq[h * IDX_DIM:(h + 1) * IDX_DIM].astype(BF16)
        for t, sl in subtiles:
            iqt_ref[0, t, :, h * tw:(h + 1) * tw] = iqh[:, sl]
    iw0 = _T_IW - _T_IQ
    iwt_ref[0] = iq[iw0:iw0 + IDX_HEADS] * (IDX_HEADS ** -0.5 * IDX_DIM ** -0.5)

    ik_ref[0] = proj_r(_R_IK, _R_END)[:, 0:IDX_DIM].astype(BF16)


def _inproj(h3d, g_row, wt, wr, wukt, gq, gk, gql, gcr, gcc, tm, tw):
    b, s, d = h3d.shape
    nt = s // tw
    kern = functools.partial(_inproj_kernel, tm=tm, tw=tw)
    out_shape = [
        jax.ShapeDtypeStruct((b, 2 * A_HEADS * A_DK, s), BF16),
        jax.ShapeDtypeStruct((b, s, 2 * A_HEADS * A_DK), BF16),
        jax.ShapeDtypeStruct((b, nt, A_HEADS * A_DV, tw), BF16),
        jax.ShapeDtypeStruct((b, nt, B_DC, B_HEADS * tw), BF16),
        jax.ShapeDtypeStruct((b, s, B_DC), BF16),
        jax.ShapeDtypeStruct((b, nt, B_DC, tw), BF16),
        jax.ShapeDtypeStruct((b, nt, IDX_DIM, IDX_HEADS * tw), BF16),
        jax.ShapeDtypeStruct((b, s, IDX_DIM), BF16),
        jax.ShapeDtypeStruct((b, IDX_HEADS, s), F32),
    ]
    k = tm // tw
    out_specs = [
        pl.BlockSpec((1, 2 * A_HEADS * A_DK, tm), lambda bi, i: (bi, 0, i)),
        pl.BlockSpec((1, tm, 2 * A_HEADS * A_DK), lambda bi, i: (bi, i, 0)),
        pl.BlockSpec((1, k, A_HEADS * A_DV, tw), lambda bi, i: (bi, i, 0, 0)),
        pl.BlockSpec((1, k, B_DC, B_HEADS * tw), lambda bi, i: (bi, i, 0, 0)),
        pl.BlockSpec((1, tm, B_DC), lambda bi, i: (bi, i, 0)),
        pl.BlockSpec((1, k, B_DC, tw), lambda bi, i: (bi, i, 0, 0)),
        pl.BlockSpec((1, k, IDX_DIM, IDX_HEADS * tw), lambda bi, i: (bi, i, 0, 0)),
        pl.BlockSpec((1, tm, IDX_DIM), lambda bi, i: (bi, i, 0)),
        pl.BlockSpec((1, IDX_HEADS, tm), lambda bi, i: (bi, 0, i)),
    ]
    consts = (g_row, wt, wr, wukt, gq, gk, gql, gcr, gcc)
    return pl.pallas_call(
        kern,
        grid=(b, s // tm),
        in_specs=[pl.BlockSpec((1, tm, d), lambda bi, i: (bi, i, 0))] + [_const_spec(a.shape) for a in consts],
        out_specs=out_specs,
        out_shape=out_shape,
        compiler_params=_params(2),
        name="inproj",
    )(h3d, *consts)


def _key_to_f32(u):
    bits = u ^ (jnp.right_shift(u, 31) & 0x7FFFFFFF)
    return lax.bitcast_convert_type(bits, F32)


def _online(s_ref, lanes, m_ref, l_ref, acc_ref, v_lhs, first=False):
    if first:
        m_new = jnp.max(s_ref[...], axis=0, keepdims=True)
        m_ref[:, lanes] = m_new
        p = jnp.exp2(s_ref[...] - m_new)
        l_ref[:, lanes] = jnp.sum(p, axis=0, keepdims=True)
        acc_ref[:, lanes] = jnp.dot(v_lhs, p.astype(BF16), preferred_element_type=F32)
        return
    m_old = m_ref[:, lanes]
    m_new = jnp.maximum(m_old, jnp.max(s_ref[...], axis=0, keepdims=True))
    m_ref[:, lanes] = m_new
    alpha = jnp.exp2(m_old - m_new)
    p = jnp.exp2(s_ref[...] - m_new)
    l_ref[:, lanes] = alpha * l_ref[:, lanes] + jnp.sum(p, axis=0, keepdims=True)
    acc_ref[:, lanes] = alpha * acc_ref[:, lanes] + jnp.dot(v_lhs, p.astype(BF16),
                                                           preferred_element_type=F32)


def _attn_kernel(qat_ref, qlt_ref, iqt_ref, iwt_ref, h_ref,
                 ka_ref, vat_ref, c_ref, ct_ref, ik_ref,
                 kam_ref, vatm_ref, cm_ref, ctm_ref, ikm_ref,
                 tall_ref, tmall_ref, lamp_ref, subln_ref, wuvt_ref, wout_ref,
                 o_ref,
                 st_ref, stm_ref, hi_ref, him_ref, b1_ref, b1m_ref, b0_ref, b0m_ref,
                 qbd_ref, sa_ref, sd_ref, m_ref, l_ref, acc_ref, m2_ref, l2_ref, acc2_ref, ot_ref):
    i = pl.program_id(1)
    n_tiles = i + 1
    midx = jnp.minimum(i, 1)

    zq = jnp.zeros((A_DK, TQ), BF16)
    for h in range(A_HEADS):
        q0 = qat_ref[0, (2 * h) * A_DK:(2 * h + 1) * A_DK, :]
        q1 = qat_ref[0, (2 * h + 1) * A_DK:(2 * h + 2) * A_DK, :]
        qbd_ref[h, 0:A_DK, 0:TQ] = q0
        qbd_ref[h, 0:A_DK, TQ:A_LANES] = zq
        qbd_ref[h, A_DK:2 * A_DK, 0:TQ] = zq
        qbd_ref[h, A_DK:2 * A_DK, TQ:A_LANES] = q1

    def a_tile(k_get, v_get, bias_get, first=False):
        rows = v_get(0).shape[1]
        for h in range(A_HEADS):
            bias = bias_get(h)
            sa_ref[h, 0:rows, :] = (jnp.dot(k_get(h), qbd_ref[h], preferred_element_type=F32)
                                    + jnp.concatenate([bias, bias], axis=1))
        for h in range(A_HEADS):
            _online(sa_ref.at[h, 0:rows, :], slice(h * A_LANES, (h + 1) * A_LANES),
                    m_ref, l_ref, acc_ref, v_get(h), first)

    qrel = i * TQ + lax.broadcasted_iota(jnp.int32, (1, TQ), 1)

    def idx_scores(ik_tile):
        d = jnp.dot(ik_tile, iqt_ref[0, 0], preferred_element_type=F32)
        acc = jnp.zeros((ik_tile.shape[0], TQ), F32)
        for h in range(IDX_HEADS):
            acc = acc + iwt_ref[0, h:h + 1, :] * jnp.maximum(d[:, h * TQ:(h + 1) * TQ], 0.0)
        return acc

    def store_digits(st, hi_dst, b1_dst, b0_dst):
        bits = lax.bitcast_convert_type(st, jnp.int32)
        key = bits ^ (jnp.right_shift(bits, 31) & 0x7FFFFFFF)
        hi_dst[...] = lax.bitcast_convert_type(bits & jnp.int32(-65536), F32).astype(BF16)
        b1_dst[...] = (jnp.right_shift(key, 8) & 255).astype(F32).astype(BF16)
        b0_dst[...] = (key & 255).astype(F32).astype(BF16)

    mrow = lax.broadcasted_iota(jnp.int32, (N_META, TQ), 0)
    trow = lax.broadcasted_iota(jnp.int32, (TQ, TQ), 0)

    a_tile(lambda h: kam_ref[0, 0:N_META, h * 2 * A_DK:(h + 1) * 2 * A_DK],
           lambda h: vatm_ref[0, 0, h * A_DV:(h + 1) * A_DV, 0:N_META],
           lambda h: tmall_ref[h, midx], first=True)
    sm = idx_scores(ikm_ref[0, 0:N_META, :])
    stm_ref[...] = sm
    store_digits(sm, him_ref, b1m_ref, b0m_ref)

    assert KEY_TILES_PER_STEP == 4
    n_quads = jnp.right_shift(n_tiles, 2)
    has_pair = (n_tiles & 2) != 0
    has_tail = (n_tiles & 1) != 0
    cat = lambda parts, axis: parts[0] if len(parts) == 1 else jnp.concatenate(parts, axis=axis)

    def walk_tiles(visit):
        def body(q, carry):
            visit([KEY_TILES_PER_STEP * q + t for t in range(KEY_TILES_PER_STEP)])
            return carry

        lax.fori_loop(0, n_quads, body, 0)

        @pl.when(has_pair)
        def _():
            visit([KEY_TILES_PER_STEP * n_quads, KEY_TILES_PER_STEP * n_quads + 1])

        @pl.when(has_tail)
        def _():
            visit([n_tiles - 1])

    def key_rows(js):
        return pl.ds(pl.multiple_of(js[0] * TQ, TQ), len(js) * TQ)

    def bias_rows(js):
        kidx = [jnp.minimum(i - j, 2) for j in js]
        return lambda h: cat([tall_ref[h, k] for k in kidx], 0)

    def ab_tiles(js):
        rows = key_rows(js)
        a_tile(lambda h: ka_ref[0, rows, h * 2 * A_DK:(h + 1) * 2 * A_DK],
               lambda h: cat([vat_ref[0, j, h * A_DV:(h + 1) * A_DV, :] for j in js], 1),
               bias_rows(js))
        for t0 in range(0, len(js), 2):
            jj = js[t0:t0 + 2]
            sc = idx_scores(ik_ref[0, key_rows(jj), :])
            for t, j in enumerate(jj):
                sx = jnp.where(j * TQ + trow <= qrel, sc[t * TQ:(t + 1) * TQ], -jnp.inf)
                st_ref[j] = sx
                store_digits(sx, hi_ref.at[j], b1_ref.at[j], b0_ref.at[j])

    walk_tiles(ab_tiles)

    lp = lamp_ref[...]
    lam = (jnp.exp(jnp.sum(lp[0:1] * lp[1:2], axis=-1, keepdims=True))
           - jnp.exp(jnp.sum(lp[2:3] * lp[3:4], axis=-1, keepdims=True)) + LAM_INIT)
    for h in range(A_HEADS):
        l0 = slice(h * A_LANES, h * A_LANES + TQ)
        l1 = slice(h * A_LANES + TQ, (h + 1) * A_LANES)
        o = acc_ref[:, l0] * (1.0 / l_ref[:, l0]) - acc_ref[:, l1] * (lam / l_ref[:, l1])
        o = o * lax.rsqrt(jnp.mean(o * o, axis=0, keepdims=True) + EPS) * subln_ref[...]
        ot_ref[h * A_DV:(h + 1) * A_DV, :] = o.astype(BF16)

    kf = jnp.minimum(IDX_TOPK_MAX, qrel + (N_META + 1)).astype(F32)
    one, zero = jnp.full((), 1, BF16), jnp.full((), 0, BF16)

    def pk_count(meta_ref, tile_ref, pred, t_row):
        tb = jnp.broadcast_to(t_row.astype(BF16), (PACK, TQ))[None]

        def part(x):
            x3 = x.reshape(x.shape[0] // PACK, PACK, TQ)
            ind = jnp.where(pred(x3, tb), one, zero)
            acc = ind[0]
            for k in range(1, ind.shape[0]):
                acc = acc + ind[k]
            return acc.astype(F32)

        p = lax.fori_loop(0, n_tiles, lambda j, p: p + part(tile_ref[j]), part(meta_ref[...]))
        return jnp.sum(p, axis=0, keepdims=True)

    def pk_keep(meta_ref, tile_ref, sel_meta_ref, sel_tile_ref, t_row):
        tb = jnp.broadcast_to(t_row.astype(BF16), (1, TQ))
        meta_ref[...] = jnp.where(sel_meta_ref[...] == tb, meta_ref[...], -one)

        def body(j, carry):
            tile_ref[j] = jnp.where(sel_tile_ref[j] == tb, tile_ref[j], -one)
            return carry

        lax.fori_loop(0, n_tiles, body, 0)

    ge = lambda x, t: x >= t
    gt = lambda x, t: x > t

    def top_half_value(w):
        k = w - 32768
        b16 = (k ^ (jnp.right_shift(k, 15) & 0x7FFF)) & 0xFFFF
        return lax.bitcast_convert_type(jnp.left_shift(b16, 16), F32)

    def bisect(n_bits, to_value, meta_ref, tile_ref, need):
        def body(b, w):
            cand = w | jnp.left_shift(jnp.int32(1), n_bits - 1 - b)
            return jnp.where(pk_count(meta_ref, tile_ref, ge, to_value(cand)) >= need, cand, w)

        return lax.fori_loop(0, n_bits, body, jnp.zeros((1, TQ), jnp.int32))

    as_f32 = lambda d: d.astype(F32)
    w_hi = bisect(16, top_half_value, him_ref, hi_ref, kf)
    need1 = kf - pk_count(him_ref, hi_ref, gt, top_half_value(w_hi))
    pk_keep(b1m_ref, b1_ref, him_ref, hi_ref, top_half_value(w_hi))
    d1 = bisect(8, as_f32, b1m_ref, b1_ref, need1)
    need0 = need1 - pk_count(b1m_ref, b1_ref, gt, as_f32(d1))
    pk_keep(b0m_ref, b0_ref, b1m_ref, b1_ref, as_f32(d1))
    d0 = bisect(8, as_f32, b0m_ref, b0_ref, need0)
    u = jnp.left_shift(w_hi - 32768, 16) | jnp.left_shift(d1, 8) | d0
    thr = _key_to_f32(u)

    def count(ind_meta, ind_tile):
        p = ind_meta(stm_ref[...]).reshape(N_META // 8, 8, TQ).sum(axis=0)
        p = lax.fori_loop(0, n_tiles,
                          lambda j, p: p + ind_tile(st_ref[j], j).reshape(TQ // 8, 8, TQ).sum(axis=0), p)
        return jnp.sum(p, axis=0, keepdims=True)

    def count_ge(t):
        ind = lambda x: jnp.where(x >= t, 1.0, 0.0)
        return count(ind, lambda x, j: ind(x))

    def count_gt(t):
        ind = lambda x: jnp.where(x > t, 1.0, 0.0)
        return count(ind, lambda x, j: ind(x))

    need = kf - count_gt(thr)
    overflow = jnp.max(jnp.where(count_ge(thr) > kf, 1.0, 0.0)) > 0.0

    @pl.when(overflow)
    def _():
        def count_ties_before(xc):
            return count(lambda x: jnp.where((x == thr) & (mrow < xc), 1.0, 0.0),
                         lambda x, j: jnp.where((x == thr) & (j * TQ + N_META + trow < xc), 1.0, 0.0))

        def pos_body(b, xs):
            cand = xs | jnp.left_shift(jnp.int32(1), 11 - b)
            return jnp.where(count_ties_before(cand) < need, cand, xs)

        last = lax.fori_loop(0, 12, pos_body, jnp.zeros((1, TQ), jnp.int32))
        sm = stm_ref[...]
        stm_ref[...] = jnp.where((sm == thr) & (mrow > last), -jnp.inf, sm)

        def drop_body(j, carry):
            sx = st_ref[j]
            st_ref[j] = jnp.where((sx == thr) & (j * TQ + N_META + trow > last), -jnp.inf, sx)
            return carry

        lax.fori_loop(0, n_tiles, drop_body, 0)

    def d_tile(st_tile, c_tile, ct_tile, bias_get, first=False):
        sel = jnp.where(st_tile >= thr, 0.0, NEG)
        rows = c_tile.shape[0]
        for h in range(B_HEADS):
            sd_ref[h, 0:rows, :] = (jnp.dot(c_tile, qlt_ref[0, 0, :, h * TQ:(h + 1) * TQ],
                                            preferred_element_type=F32)
                                    + (bias_get(A_HEADS + h) + sel))
        for h in range(B_HEADS):
            _online(sd_ref.at[h, 0:rows, :], slice(h * TQ, (h + 1) * TQ), m2_ref, l2_ref, acc2_ref,
                    ct_tile, first)

    d_tile(stm_ref[...], cm_ref[0, 0:N_META, :], ctm_ref[0, 0, :, 0:N_META], lambda h: tmall_ref[h, midx],
           first=True)

    def d_tiles(js):
        d_tile(cat([st_ref[j] for j in js], 0), c_ref[0, key_rows(js), :],
               cat([ct_ref[0, j] for j in js], 1), bias_rows(js))

    walk_tiles(d_tiles)

    for h in range(B_HEADS):
        lanes = slice(h * TQ, (h + 1) * TQ)
        olat = (acc2_ref[:, lanes] * (1.0 / l2_ref[:, lanes])).astype(BF16)
        ob = jnp.dot(wuvt_ref[h], olat, preferred_element_type=F32)
        r0 = A_HEADS * A_DV + h * B_DV
        ot_ref[r0:r0 + B_DV, :] = ob.astype(BF16)

    y = lax.dot_general(ot_ref[...], wout_ref[...], (((0,), (0,)), ((), ())),
                        preferred_element_type=F32)
    o_ref[0] = h_ref[0] + y


def _attention(q_side, h3d, k_side, meta_side, tall, tmall, lamp, subln, wuvt, wout):
    qat, qlt, iqt, iwt = q_side
    ka, vat, c, ct, ik = k_side
    kam, vatm, cm, ctm, ikm = meta_side
    b, s, d = h3d.shape
    nt = s // TQ
    in_specs = [
        pl.BlockSpec((1, qat.shape[1], TQ), lambda bi, i: (bi, 0, i)),
        pl.BlockSpec((1, 1) + qlt.shape[2:], lambda bi, i: (bi, i, 0, 0)),
        pl.BlockSpec((1, 1) + iqt.shape[2:], lambda bi, i: (bi, i, 0, 0)),
        pl.BlockSpec((1, iwt.shape[1], TQ), lambda bi, i: (bi, 0, i)),
        pl.BlockSpec((1, TQ, d), lambda bi, i: (bi, i, 0)),
        pl.BlockSpec((1,) + ka.shape[1:], lambda bi, i: (bi, 0, 0)),
        pl.BlockSpec((1,) + vat.shape[1:], lambda bi, i: (bi, 0, 0, 0)),
        pl.BlockSpec((1,) + c.shape[1:], lambda bi, i: (bi, 0, 0)),
        pl.BlockSpec((1,) + ct.shape[1:], lambda bi, i: (bi, 0, 0, 0)),
        pl.BlockSpec((1,) + ik.shape[1:], lambda bi, i: (bi, 0, 0)),
    ] + [_const_spec(a.shape) for a in (kam, vatm, cm, ctm, ikm, tall, tmall, lamp, subln, wuvt, wout)]
    scratch = [
        pltpu.VMEM((nt, TQ, TQ), F32),
        pltpu.VMEM((N_META, TQ), F32),
        pltpu.VMEM((nt, TQ, TQ), BF16), pltpu.VMEM((N_META, TQ), BF16),
        pltpu.VMEM((nt, TQ, TQ), BF16), pltpu.VMEM((N_META, TQ), BF16),
        pltpu.VMEM((nt, TQ, TQ), BF16), pltpu.VMEM((N_META, TQ), BF16),
        pltpu.VMEM((A_HEADS, 2 * A_DK, A_LANES), BF16),
        pltpu.VMEM((A_HEADS, KEY_TILES_PER_STEP * TQ, A_LANES), F32),
        pltpu.VMEM((B_HEADS, KEY_TILES_PER_STEP * TQ, TQ), F32),
        pltpu.VMEM((1, A_HEADS * A_LANES), F32),
        pltpu.VMEM((1, A_HEADS * A_LANES), F32),
        pltpu.VMEM((A_DV, A_HEADS * A_LANES), F32),
        pltpu.VMEM((1, B_HEADS * TQ), F32),
        pltpu.VMEM((1, B_HEADS * TQ), F32),
        pltpu.VMEM((B_DC, B_HEADS * TQ), F32),
        pltpu.VMEM((A_HEADS * A_DV + B_HEADS * B_DV, TQ), BF16),
    ]
    return pl.pallas_call(
        _attn_kernel,
        grid=(b, nt),
        in_specs=in_specs,
        out_specs=pl.BlockSpec((1, TQ, d), lambda bi, i: (bi, i, 0)),
        out_shape=jax.ShapeDtypeStruct((b, s, d), F32),
        scratch_shapes=scratch,
        compiler_params=_params(2),
        name="attention",
    )(qat, qlt, iqt, iwt, h3d, ka, vat, c, ct, ik, kam, vatm, cm, ctm, ikm,
      tall, tmall, lamp, subln, wuvt, wout)


def kernel(x, meta_tokens, rel_bias, ffn1_norm, ffn1_w_gate, ffn1_w_up, ffn1_w_down, mix_norm, w_in, a_q_norm, a_k_norm, a_lambda_q1, a_lambda_k1, a_lambda_q2, a_lambda_k2, a_subln, b_kv_norm, b_w_uk, b_q_norm, b_w_uv, w_out, ffn2_norm, ffn2_w_gate, ffn2_w_up, ffn2_w_down):
    bsz, seq, d = x.shape
    assert d == D_MODEL and seq % PROJ_TM == 0 and (bsz * seq) % FFN_TM == 0
    assert min(IDX_TOPK_MAX, (seq + N_META) // 4) == IDX_TOPK_MAX
    assert ffn1_norm.shape[0] == 1, "single layer"
    lyr = 0
    row = lambda v: v.reshape(1, -1).astype(F32)
    col = lambda v: v.reshape(-1, 1).astype(F32)

    w = w_in[lyr]
    o_qa, o_ka, o_va, o_qb, o_c, o_iq, o_ik, o_iw = np.cumsum(
        [0, 2 * A_HEADS * A_DK, 2 * A_HEADS * A_DK, A_HEADS * A_DV, B_HEADS * B_DQ, B_DC,
         IDX_HEADS * IDX_DIM, IDX_DIM])[:8]
    w_qa, w_ka, w_va, w_qb = w[:, o_qa:o_ka], w[:, o_ka:o_va], w[:, o_va:o_qb], w[:, o_qb:o_c]
    w_c, w_iq, w_ik, w_iw = w[:, o_c:o_iq], w[:, o_iq:o_ik], w[:, o_ik:o_iw], w[:, o_iw:o_iw + IDX_HEADS]
    wt = jnp.concatenate([w_qa, w_va, w_qb, w_c, w_iq, w_iw,
                          jnp.zeros((d, _T_END - _T_IW - IDX_HEADS), w.dtype)], axis=1).T.astype(BF16)
    wr = jnp.concatenate([w_ka, w_c, w_ik, jnp.zeros((d, _R_END - _R_IK - IDX_DIM), w.dtype)],
                         axis=1).astype(BF16)
    wukt = jnp.transpose(b_w_uk[lyr], (0, 2, 1)).astype(BF16)
    wuvt = jnp.transpose(b_w_uv[lyr], (0, 2, 1)).astype(BF16)
    gq = col(jnp.tile(a_q_norm[lyr], 2 * A_HEADS)) * (A_SCALE * LOG2E)
    gk = row(a_k_norm[lyr])
    gql = col(b_q_norm[lyr]) * (B_SCALE * LOG2E)
    gcr, gcc = row(b_kv_norm[lyr]), col(b_kv_norm[lyr])
    subln = col(a_subln[lyr]) * (1.0 - LAM_INIT)
    lamp = jnp.stack([a_lambda_q1[lyr], a_lambda_k1[lyr], a_lambda_q2[lyr], a_lambda_k2[lyr]]).astype(F32)
    proj_consts = (row(mix_norm[lyr]), wt, wr, wukt, gq, gk, gql, gcr, gcc)

    tall, tmall = _bias_tiles(rel_bias.astype(F32))

    ffn1 = (row(ffn1_norm[lyr]), ffn1_w_gate[lyr].astype(BF16), ffn1_w_up[lyr].astype(BF16),
            ffn1_w_down[lyr].astype(BF16))
    h1 = _ffn(x.reshape(bsz * seq, d), *ffn1, tm=FFN_TM).reshape(bsz, seq, d)
    meta_pad = jnp.pad(meta_tokens.astype(x.dtype), ((0, META_PAD - N_META), (0, 0)))
    h1m = _ffn(meta_pad, *ffn1, tm=META_PAD).reshape(1, META_PAD, d)

    qat, ka, vat, qlt, c, ct, iqt, ik, iwt = _inproj(h1, *proj_consts, tm=PROJ_TM, tw=TQ)
    _, kam, vatm, _, cm, ctm, _, ikm, _ = _inproj(h1m, *proj_consts, tm=META_PAD, tw=META_PAD)

    h2 = _attention((qat, qlt, iqt, iwt), h1, (ka, vat, c, ct, ik), (kam, vatm, cm, ctm, ikm),
                    tall, tmall, lamp, subln, wuvt, w_out[lyr].astype(BF16))

    ffn2 = (row(ffn2_norm[lyr]), ffn2_w_gate[lyr].astype(BF16), ffn2_w_up[lyr].astype(BF16),
            ffn2_w_down[lyr].astype(BF16))
    return _ffn(h2.reshape(bsz * seq, d), *ffn2, tm=FFN_TM).reshape(bsz, seq, d)
```

```python
import functools
import math

import jax
import jax.numpy as jnp
import numpy as np
from jax import lax
from jax.experimental import pallas as pl
from jax.experimental.pallas import tpu as pltpu

F32 = jnp.float32
BF16 = jnp.bfloat16

D_MODEL = 1024
N_META = 16
A_HEADS = 4
A_DK = 64
A_DV = 128
B_HEADS = 4
B_DQ = 128
B_DC = 256
B_DV = 128
IDX_HEADS = 8
IDX_DIM = 64
IDX_TOPK_MAX = 256
REL_BUCKETS = 32
REL_MAX_DIST = 128
D_FF = 2816
EPS = 1e-6
LOG2E = math.log2(math.e)
A_SCALE = A_DK ** -0.5
B_SCALE = B_DC ** -0.5
LAM_INIT = 0.8 - 0.6 * math.exp(-0.3 * 0)

NEG = -1e30
INT_MIN = -(2 ** 31)

TQ = 256
META_PAD = 128
FF_CHUNK = 1408
FFN_TM = 512
PROJ_TM = 512
V7X_VMEM_LIMIT = 56 * 1024 * 1024

KEY_TILES_PER_STEP = 4
A_LANES = 2 * TQ
PACK = 16
assert N_META % PACK == 0
_T_QA, _T_VA, _T_QB, _T_C, _T_IQ, _T_IW, _T_END = 0, 512, 1024, 1536, 1792, 2304, 2320
_R_KA, _R_C, _R_IK, _R_END = 0, 512, 768, 896


def _bucket_edges():
    max_exact = REL_BUCKETS // 2
    n = np.arange(0, 4 * REL_MAX_DIST)
    nf = np.maximum(n, 1).astype(np.float64)
    large = max_exact + (np.log(nf / max_exact) / math.log(REL_MAX_DIST / max_exact)
                         * (REL_BUCKETS - max_exact)).astype(np.int64)
    large = np.minimum(large, REL_BUCKETS - 1)
    bucket = np.where(n < max_exact, n, large)
    return [int(n[np.argmax(bucket >= j)]) for j in range(REL_BUCKETS)]


_EDGES = _bucket_edges()
assert _EDGES[-1] <= TQ, "tiles two or more away from the diagonal must sit in the last bucket"


def _const_spec(shape):
    nd = len(shape)
    return pl.BlockSpec(shape, lambda *_: (0,) * nd, pipeline_mode=pl.Buffered(1))


def _params(n_grid):
    return pltpu.CompilerParams(dimension_semantics=("arbitrary",) * n_grid,
                                vmem_limit_bytes=V7X_VMEM_LIMIT)


def _rms_lanes(x, g_row):
    ms = jnp.mean(x * x, axis=-1, keepdims=True)
    return x * lax.rsqrt(ms + EPS) * g_row


def _bias_kernel(rb_ref, tall_ref, tmall_ref):
    h = pl.program_id(0)

    def table(d):
        val = jnp.full(d.shape, rb_ref[0, h] * LOG2E, F32)
        for j in range(1, REL_BUCKETS):
            val = jnp.where(d >= _EDGES[j], rb_ref[j, h] * LOG2E, val)
        return val

    far = rb_ref[REL_BUCKETS - 1, h] * LOG2E
    c = lax.broadcasted_iota(jnp.int32, (TQ, TQ), 0)
    r = lax.broadcasted_iota(jnp.int32, (TQ, TQ), 1)
    d0 = r - c
    tall_ref[0, 0] = jnp.where(d0 >= 0, table(d0), NEG)
    tall_ref[0, 1] = table(d0 + TQ)
    tall_ref[0, 2] = jnp.full((TQ, TQ), far, F32)
    m = lax.broadcasted_iota(jnp.int32, (N_META, TQ), 0)
    r2 = lax.broadcasted_iota(jnp.int32, (N_META, TQ), 1)
    tmall_ref[0, 0] = table(r2 + N_META - m)
    tmall_ref[0, 1] = jnp.full((N_META, TQ), far, F32)


def _bias_tiles(rel_bias):
    nh = rel_bias.shape[1]
    return pl.pallas_call(
        _bias_kernel,
        grid=(nh,),
        in_specs=[pl.BlockSpec(memory_space=pltpu.SMEM)],
        out_specs=[pl.BlockSpec((1, 3, TQ, TQ), lambda h: (h, 0, 0, 0)),
                   pl.BlockSpec((1, 2, N_META, TQ), lambda h: (h, 0, 0, 0))],
        out_shape=[jax.ShapeDtypeStruct((nh, 3, TQ, TQ), F32),
                   jax.ShapeDtypeStruct((nh, 2, N_META, TQ), F32)],
        compiler_params=_params(1),
        name="bias_tiles",
    )(rel_bias)


def _ffn_kernel(x_ref, g_ref, wg_ref, wu_ref, wd_ref, o_ref):
    x = x_ref[...]
    xn = _rms_lanes(x, g_ref[...]).astype(BF16)
    y = None
    for c in range(D_FF // FF_CHUNK):
        sl = slice(c * FF_CHUNK, (c + 1) * FF_CHUNK)
        g = jnp.dot(xn, wg_ref[:, sl], preferred_element_type=F32)
        u = jnp.dot(xn, wu_ref[:, sl], preferred_element_type=F32)
        a = (g * jax.nn.sigmoid(g) * u).astype(BF16)
        part = jnp.dot(a, wd_ref[sl, :], preferred_element_type=F32)
        y = part if y is None else y + part
    o_ref[...] = x + 0.5 * y


def _ffn(h2d, g_row, wg, wu, wd, tm):
    m, d = h2d.shape
    return pl.pallas_call(
        _ffn_kernel,
        grid=(m // tm,),
        in_specs=[pl.BlockSpec((tm, d), lambda i: (i, 0)),
                  _const_spec((1, d)),
                  _const_spec(wg.shape), _const_spec(wu.shape), _const_spec(wd.shape)],
        out_specs=pl.BlockSpec((tm, d), lambda i: (i, 0)),
        out_shape=jax.ShapeDtypeStruct((m, d), F32),
        compiler_params=_params(1),
        name="ffn",
    )(h2d, g_row, wg, wu, wd)


def _inproj_kernel(h_ref, g_ref, wt_ref, wr_ref, wukt_ref, gq_ref, gk_ref, gql_ref, gcr_ref, gcc_ref,
                   qat_ref, ka_ref, vat_ref, qlt_ref, c_ref, ct_ref, iqt_ref, ik_ref, iwt_ref, *, tm, tw):
    xn = _rms_lanes(h_ref[0], g_ref[...]).astype(BF16)
    subtiles = [(t, slice(t * tw, (t + 1) * tw)) for t in range(tm // tw)]

    def proj_t(r0, r1):
        return lax.dot_general(wt_ref[r0:r1, :], xn, (((1,), (1,)), ((), ())), preferred_element_type=F32)

    def proj_r(c0, c1):
        return jnp.dot(xn, wr_ref[:, c0:c1], preferred_element_type=F32)

    ka = proj_r(_R_KA, _R_C)
    for hm in range(2 * A_HEADS):
        cols = slice(hm * A_DK, (hm + 1) * A_DK)
        ka_ref[0, :, cols] = _rms_lanes(ka[:, cols], gk_ref[...]).astype(BF16)

    qa = proj_t(_T_QA, _T_VA).reshape(2 * A_HEADS, A_DK, tm)
    qa = qa * lax.rsqrt(jnp.mean(qa * qa, axis=1, keepdims=True) + EPS)
    qat_ref[0] = (qa.reshape(2 * A_HEADS * A_DK, tm) * gq_ref[...]).astype(BF16)

    qb = proj_t(_T_QB, _T_C).astype(BF16)
    for h in range(B_HEADS):
        ql = jnp.dot(wukt_ref[h], qb[h * B_DQ:(h + 1) * B_DQ], preferred_element_type=F32)
        ql = (ql * lax.rsqrt(jnp.mean(ql * ql, axis=0, keepdims=True) + EPS) * gql_ref[...]).astype(BF16)
        for t, sl in subtiles:
            qlt_ref[0, t, :, h * tw:(h + 1) * tw] = ql[:, sl]

    c_ref[0] = _rms_lanes(proj_r(_R_C, _R_IK), gcr_ref[...]).astype(BF16)

    ct = proj_t(_T_C, _T_IQ)
    ct = (ct * lax.rsqrt(jnp.mean(ct * ct, axis=0, keepdims=True) + EPS) * gcc_ref[...]).astype(BF16)
    for t, sl in subtiles:
        ct_ref[0, t] = ct[:, sl]

    va = proj_t(_T_VA, _T_QB).astype(BF16)
    for t, sl in subtiles:
        vat_ref[0, t] = va[:, sl]

    iq = proj_t(_T_IQ, _T_END)
    for h in range(IDX_HEADS):
        iqh = iq[h * IDX_DIM:(h + 1) * IDX_DIM].astype(BF16)
        for t, sl in subtiles:
            iqt_ref[0, t, :, h * tw:(h + 1) * tw] = iqh[:, sl]
    iw0 = _T_IW - _T_IQ
    iwt_ref[0] = iq[iw0:iw0 + IDX_HEADS] * (IDX_HEADS ** -0.5 * IDX_DIM ** -0.5)

    ik_ref[0] = proj_r(_R_IK, _R_END)[:, 0:IDX_DIM].astype(BF16)


def _inproj(h3d, g_row, wt, wr, wukt, gq, gk, gql, gcr, gcc, tm, tw):
    b, s, d = h3d.shape
    nt = s // tw
    kern = functools.partial(_inproj_kernel, tm=tm, tw=tw)
    out_shape = [
        jax.ShapeDtypeStruct((b, 2 * A_HEADS * A_DK, s), BF16),
        jax.ShapeDtypeStruct((b, s, 2 * A_HEADS * A_DK), BF16),
        jax.ShapeDtypeStruct((b, nt, A_HEADS * A_DV, tw), BF16),
        jax.ShapeDtypeStruct((b, nt, B_DC, B_HEADS * tw), BF16),
        jax.ShapeDtypeStruct((b, s, B_DC), BF16),
        jax.ShapeDtypeStruct((b, nt, B_DC, tw), BF16),
        jax.ShapeDtypeStruct((b, nt, IDX_DIM, IDX_HEADS * tw), BF16),
        jax.ShapeDtypeStruct((b, s, IDX_DIM), BF16),
        jax.ShapeDtypeStruct((b, IDX_HEADS, s), F32),
    ]
    k = tm // tw
    out_specs = [
        pl.BlockSpec((1, 2 * A_HEADS * A_DK, tm), lambda bi, i: (bi, 0, i)),
        pl.BlockSpec((1, tm, 2 * A_HEADS * A_DK), lambda bi, i: (bi, i, 0)),
        pl.BlockSpec((1, k, A_HEADS * A_DV, tw), lambda bi, i: (bi, i, 0, 0)),
        pl.BlockSpec((1, k, B_DC, B_HEADS * tw), lambda bi, i: (bi, i, 0, 0)),
        pl.BlockSpec((1, tm, B_DC), lambda bi, i: (bi, i, 0)),
        pl.BlockSpec((1, k, B_DC, tw), lambda bi, i: (bi, i, 0, 0)),
        pl.BlockSpec((1, k, IDX_DIM, IDX_HEADS * tw), lambda bi, i: (bi, i, 0, 0)),
        pl.BlockSpec((1, tm, IDX_DIM), lambda bi, i: (bi, i, 0)),
        pl.BlockSpec((1, IDX_HEADS, tm), lambda bi, i: (bi, 0, i)),
    ]
    consts = (g_row, wt, wr, wukt, gq, gk, gql, gcr, gcc)
    return pl.pallas_call(
        kern,
        grid=(b, s // tm),
        in_specs=[pl.BlockSpec((1, tm, d), lambda bi, i: (bi, i, 0))] + [_const_spec(a.shape) for a in consts],
        out_specs=out_specs,
        out_shape=out_shape,
        compiler_params=_params(2),
        name="inproj",
    )(h3d, *consts)


def _key_to_f32(u):
    bits = u ^ (jnp.right_shift(u, 31) & 0x7FFFFFFF)
    return lax.bitcast_convert_type(bits, F32)


def _online(s_ref, lanes, m_ref, l_ref, acc_ref, v_lhs, first=False):
    if first:
        m_new = jnp.max(s_ref[...], axis=0, keepdims=True)
        m_ref[:, lanes] = m_new
        p = jnp.exp2(s_ref[...] - m_new)
        l_ref[:, lanes] = jnp.sum(p, axis=0, keepdims=True)
        acc_ref[:, lanes] = jnp.dot(v_lhs, p.astype(BF16), preferred_element_type=F32)
        return
    m_old = m_ref[:, lanes]
    m_new = jnp.maximum(m_old, jnp.max(s_ref[...], axis=0, keepdims=True))
    m_ref[:, lanes] = m_new
    alpha = jnp.exp2(m_old - m_new)
    p = jnp.exp2(s_ref[...] - m_new)
    l_ref[:, lanes] = alpha * l_ref[:, lanes] + jnp.sum(p, axis=0, keepdims=True)
    acc_ref[:, lanes] = alpha * acc_ref[:, lanes] + jnp.dot(v_lhs, p.astype(BF16),
                                                           preferred_element_type=F32)


def _attn_kernel(qat_ref, qlt_ref, iqt_ref, iwt_ref, h_ref,
                 ka_ref, vat_ref, c_ref, ct_ref, ik_ref,
                 kam_ref, vatm_ref, cm_ref, ctm_ref, ikm_ref,
                 tall_ref, tmall_ref, lamp_ref, subln_ref, wuvt_ref, wout_ref,
                 o_ref,
                 st_ref, stm_ref, hi_ref, him_ref, b1_ref, b1m_ref, b0_ref, b0m_ref,
                 key_ref, cnt_ref, qbd_ref, sa_ref, sd_ref, m_ref, l_ref, acc_ref, m2_ref, l2_ref, acc2_ref, ot_ref):
    i = pl.program_id(1)
    n_tiles = i + 1
    midx = jnp.minimum(i, 1)

    zq = jnp.zeros((A_DK, TQ), BF16)
    for h in range(A_HEADS):
        q0 = qat_ref[0, (2 * h) * A_DK:(2 * h + 1) * A_DK, :]
        q1 = qat_ref[0, (2 * h + 1) * A_DK:(2 * h + 2) * A_DK, :]
        qbd_ref[h, 0:A_DK, 0:TQ] = q0
        qbd_ref[h, 0:A_DK, TQ:A_LANES] = zq
        qbd_ref[h, A_DK:2 * A_DK, 0:TQ] = zq
        qbd_ref[h, A_DK:2 * A_DK, TQ:A_LANES] = q1

    def a_tile(k_get, v_get, bias_get, first=False):
        rows = v_get(0).shape[1]
        for h in range(A_HEADS):
            bias = bias_get(h)
            sa_ref[h, 0:rows, :] = (jnp.dot(k_get(h), qbd_ref[h], preferred_element_type=F32)
                                    + jnp.concatenate([bias, bias], axis=1))
        for h in range(A_HEADS):
            _online(sa_ref.at[h, 0:rows, :], slice(h * A_LANES, (h + 1) * A_LANES),
                    m_ref, l_ref, acc_ref, v_get(h), first)

    qrel = i * TQ + lax.broadcasted_iota(jnp.int32, (1, TQ), 1)

    def idx_scores(ik_tile):
        d = jnp.dot(ik_tile, iqt_ref[0, 0], preferred_element_type=F32)
        acc = jnp.zeros((ik_tile.shape[0], TQ), F32)
        for h in range(IDX_HEADS):
            acc = acc + iwt_ref[0, h:h + 1, :] * jnp.maximum(d[:, h * TQ:(h + 1) * TQ], 0.0)
        return acc

    def store_digits(st, hi_dst, b1_dst, b0_dst):
        bits = lax.bitcast_convert_type(st, jnp.int32)
        key = bits ^ (jnp.right_shift(bits, 31) & 0x7FFFFFFF)
        hi_dst[...] = lax.bitcast_convert_type(bits & jnp.int32(-65536), F32).astype(BF16)
        b1_dst[...] = (jnp.right_shift(key, 8) & 255).astype(F32).astype(BF16)
        b0_dst[...] = (key & 255).astype(F32).astype(BF16)

    mrow = lax.broadcasted_iota(jnp.int32, (N_META, TQ), 0)
    trow = lax.broadcasted_iota(jnp.int32, (TQ, TQ), 0)

    a_tile(lambda h: kam_ref[0, 0:N_META, h * 2 * A_DK:(h + 1) * 2 * A_DK],
           lambda h: vatm_ref[0, 0, h * A_DV:(h + 1) * A_DV, 0:N_META],
           lambda h: tmall_ref[h, midx], first=True)
    sm = idx_scores(ikm_ref[0, 0:N_META, :])
    stm_ref[...] = sm
    store_digits(sm, him_ref, b1m_ref, b0m_ref)

    assert KEY_TILES_PER_STEP == 4
    n_quads = jnp.right_shift(n_tiles, 2)
    has_pair = (n_tiles & 2) != 0
    has_tail = (n_tiles & 1) != 0
    cat = lambda parts, axis: parts[0] if len(parts) == 1 else jnp.concatenate(parts, axis=axis)

    def walk_tiles(visit):
        def body(q, carry):
            visit([KEY_TILES_PER_STEP * q + t for t in range(KEY_TILES_PER_STEP)])
            return carry

        lax.fori_loop(0, n_quads, body, 0)

        @pl.when(has_pair)
        def _():
            visit([KEY_TILES_PER_STEP * n_quads, KEY_TILES_PER_STEP * n_quads + 1])

        @pl.when(has_tail)
        def _():
            visit([n_tiles - 1])

    def key_rows(js):
        return pl.ds(pl.multiple_of(js[0] * TQ, TQ), len(js) * TQ)

    def bias_rows(js):
        kidx = [jnp.minimum(i - j, 2) for j in js]
        return lambda h: cat([tall_ref[h, k] for k in kidx], 0)

    def ab_tiles(js):
        rows = key_rows(js)
        a_tile(lambda h: ka_ref[0, rows, h * 2 * A_DK:(h + 1) * 2 * A_DK],
               lambda h: cat([vat_ref[0, j, h * A_DV:(h + 1) * A_DV, :] for j in js], 1),
               bias_rows(js))
        for t0 in range(0, len(js), 2):
            jj = js[t0:t0 + 2]
            sc = idx_scores(ik_ref[0, key_rows(jj), :])
            for t, j in enumerate(jj):
                sx = jnp.where(j * TQ + trow <= qrel, sc[t * TQ:(t + 1) * TQ], -jnp.inf)
                st_ref[j] = sx
                store_digits(sx, hi_ref.at[j], b1_ref.at[j], b0_ref.at[j])

    walk_tiles(ab_tiles)

    lp = lamp_ref[...]
    lam = (jnp.exp(jnp.sum(lp[0:1] * lp[1:2], axis=-1, keepdims=True))
           - jnp.exp(jnp.sum(lp[2:3] * lp[3:4], axis=-1, keepdims=True)) + LAM_INIT)
    for h in range(A_HEADS):
        l0 = slice(h * A_LANES, h * A_LANES + TQ)
        l1 = slice(h * A_LANES + TQ, (h + 1) * A_LANES)
        o = acc_ref[:, l0] * (1.0 / l_ref[:, l0]) - acc_ref[:, l1] * (lam / l_ref[:, l1])
        o = o * lax.rsqrt(jnp.mean(o * o, axis=0, keepdims=True) + EPS) * subln_ref[...]
        ot_ref[h * A_DV:(h + 1) * A_DV, :] = o.astype(BF16)

    kf = jnp.minimum(IDX_TOPK_MAX, qrel + (N_META + 1)).astype(F32)
    one, zero = jnp.full((), 1, BF16), jnp.full((), 0, BF16)

    ge = lambda x, t: x >= t
    gt = lambda x, t: x > t
    as_f32 = lambda d: d.astype(F32)

    def top_half_value(w):
        k = w - 32768
        b16 = (k ^ (jnp.right_shift(k, 15) & 0x7FFF)) & 0xFFFF
        return lax.bitcast_convert_type(jnp.left_shift(b16, 16), F32)

    def find_threshold(n):
        def pk_count(meta_ref, tile_ref, pred, t_row):
            tb = jnp.broadcast_to(t_row.astype(BF16), (PACK, TQ))[None]
            acc = None
            for x in [meta_ref[...]] + [tile_ref[j] for j in range(n)]:
                x3 = x.reshape(x.shape[0] // PACK, PACK, TQ)
                ind = jnp.where(pred(x3, tb), one, zero)
                for k in range(ind.shape[0]):
                    acc = ind[k] if acc is None else acc + ind[k]
            return jnp.sum(acc.astype(F32), axis=0, keepdims=True)

        def pk_keep(meta_ref, tile_ref, sel_meta_ref, sel_tile_ref, t_row):
            tb = jnp.broadcast_to(t_row.astype(BF16), (1, TQ))
            meta_ref[...] = jnp.where(sel_meta_ref[...] == tb, meta_ref[...], -one)
            for j in range(n):
                tile_ref[j] = jnp.where(sel_tile_ref[j] == tb, tile_ref[j], -one)

        def bisect(n_bits, to_value, meta_ref, tile_ref, need):
            def body(b, w):
                cand = w | jnp.left_shift(jnp.int32(1), n_bits - 1 - b)
                return jnp.where(pk_count(meta_ref, tile_ref, ge, to_value(cand)) >= need, cand, w)

            return lax.fori_loop(0, n_bits, body, jnp.zeros((1, TQ), jnp.int32))

        w_hi = bisect(16, top_half_value, him_ref, hi_ref, kf)
        need1 = kf - pk_count(him_ref, hi_ref, gt, top_half_value(w_hi))
        pk_keep(b1m_ref, b1_ref, him_ref, hi_ref, top_half_value(w_hi))
        d1 = bisect(8, as_f32, b1m_ref, b1_ref, need1)
        need0 = need1 - pk_count(b1m_ref, b1_ref, gt, as_f32(d1))
        pk_keep(b0m_ref, b0_ref, b1m_ref, b1_ref, as_f32(d1))
        d0 = bisect(8, as_f32, b0m_ref, b0_ref, need0)
        return jnp.left_shift(w_hi - 32768, 16) | jnp.left_shift(d1, 8) | d0

    assert PACK * st_ref.shape[0] + N_META // PACK <= 256, "bf16 counters must stay exact"
    for n in range(1, st_ref.shape[0] + 1):
        @pl.when(n_tiles == n)
        def _(n=n):
            key_ref[...] = find_threshold(n)

    def count(ind_meta, ind_tile):
        p = ind_meta(stm_ref[...]).reshape(N_META // 8, 8, TQ).sum(axis=0)
        p = lax.fori_loop(0, n_tiles,
                          lambda j, p: p + ind_tile(st_ref[j], j).reshape(TQ // 8, 8, TQ).sum(axis=0), p)
        return jnp.sum(p, axis=0, keepdims=True)

    def count_ge(t):
        ind = lambda x: jnp.where(x >= t, 1.0, 0.0)
        return count(ind, lambda x, j: ind(x))

    def count_gt(t):
        ind = lambda x: jnp.where(x > t, 1.0, 0.0)
        return count(ind, lambda x, j: ind(x))

    hint = _key_to_f32(key_ref[...])
    cnt_ref[0:1, :] = count_gt(hint)
    cnt_ref[1:2, :] = count_ge(hint)
    mismatch = jnp.max(jnp.where((cnt_ref[0:1, :] >= kf) | (cnt_ref[1:2, :] < kf), 1.0, 0.0)) > 0.0

    @pl.when(mismatch)
    def _():
        def bit_body(b, u):
            cand = u + jnp.left_shift(jnp.int32(1), 31 - b)
            return jnp.where(count_ge(_key_to_f32(cand)) >= kf, cand, u)

        u = lax.fori_loop(0, 32, bit_body, jnp.full((1, TQ), INT_MIN, jnp.int32))
        key_ref[...] = u
        cnt_ref[0:1, :] = count_gt(_key_to_f32(u))
        cnt_ref[1:2, :] = count_ge(_key_to_f32(u))

    thr = _key_to_f32(key_ref[...])

    need = kf - cnt_ref[0:1, :]
    overflow = jnp.max(jnp.where(cnt_ref[1:2, :] > kf, 1.0, 0.0)) > 0.0

    @pl.when(overflow)
    def _():
        def count_ties_before(xc):
            return count(lambda x: jnp.where((x == thr) & (mrow < xc), 1.0, 0.0),
                         lambda x, j: jnp.where((x == thr) & (j * TQ + N_META + trow < xc), 1.0, 0.0))

        def pos_body(b, xs):
            cand = xs | jnp.left_shift(jnp.int32(1), 11 - b)
            return jnp.where(count_ties_before(cand) < need, cand, xs)

        last = lax.fori_loop(0, 12, pos_body, jnp.zeros((1, TQ), jnp.int32))
        sm = stm_ref[...]
        stm_ref[...] = jnp.where((sm == thr) & (mrow > last), -jnp.inf, sm)

        def drop_body(j, carry):
            sx = st_ref[j]
            st_ref[j] = jnp.where((sx == thr) & (j * TQ + N_META + trow > last), -jnp.inf, sx)
            return carry

        lax.fori_loop(0, n_tiles, drop_body, 0)

    def d_tile(st_tile, c_tile, ct_tile, bias_get, first=False):
        sel = jnp.where(st_tile >= thr, 0.0, NEG)
        rows = c_tile.shape[0]
        for h in range(B_HEADS):
            sd_ref[h, 0:rows, :] = (jnp.dot(c_tile, qlt_ref[0, 0, :, h * TQ:(h + 1) * TQ],
                                            preferred_element_type=F32)
                                    + (bias_get(A_HEADS + h) + sel))
        for h in range(B_HEADS):
            _online(sd_ref.at[h, 0:rows, :], slice(h * TQ, (h + 1) * TQ), m2_ref, l2_ref, acc2_ref,
                    ct_tile, first)

    d_tile(stm_ref[...], cm_ref[0, 0:N_META, :], ctm_ref[0, 0, :, 0:N_META], lambda h: tmall_ref[h, midx],
           first=True)

    def d_tiles(js):
        d_tile(cat([st_ref[j] for j in js], 0), c_ref[0, key_rows(js), :],
               cat([ct_ref[0, j] for j in js], 1), bias_rows(js))

    walk_tiles(d_tiles)

    for h in range(B_HEADS):
        lanes = slice(h * TQ, (h + 1) * TQ)
        olat = (acc2_ref[:, lanes] * (1.0 / l2_ref[:, lanes])).astype(BF16)
        ob = jnp.dot(wuvt_ref[h], olat, preferred_element_type=F32)
        r0 = A_HEADS * A_DV + h * B_DV
        ot_ref[r0:r0 + B_DV, :] = ob.astype(BF16)

    y = lax.dot_general(ot_ref[...], wout_ref[...], (((0,), (0,)), ((), ())),
                        preferred_element_type=F32)
    o_ref[0] = h_ref[0] + y


def _attention(q_side, h3d, k_side, meta_side, tall, tmall, lamp, subln, wuvt, wout):
    qat, qlt, iqt, iwt = q_side
    ka, vat, c, ct, ik = k_side
    kam, vatm, cm, ctm, ikm = meta_side
    b, s, d = h3d.shape
    nt = s // TQ
    in_specs = [
        pl.BlockSpec((1, qat.shape[1], TQ), lambda bi, i: (bi, 0, i)),
        pl.BlockSpec((1, 1) + qlt.shape[2:], lambda bi, i: (bi, i, 0, 0)),
        pl.BlockSpec((1, 1) + iqt.shape[2:], lambda bi, i: (bi, i, 0, 0)),
        pl.BlockSpec((1, iwt.shape[1], TQ), lambda bi, i: (bi, 0, i)),
        pl.BlockSpec((1, TQ, d), lambda bi, i: (bi, i, 0)),
        pl.BlockSpec((1,) + ka.shape[1:], lambda bi, i: (bi, 0, 0)),
        pl.BlockSpec((1,) + vat.shape[1:], lambda bi, i: (bi, 0, 0, 0)),
        pl.BlockSpec((1,) + c.shape[1:], lambda bi, i: (bi, 0, 0)),
        pl.BlockSpec((1,) + ct.shape[1:], lambda bi, i: (bi, 0, 0, 0)),
        pl.BlockSpec((1,) + ik.shape[1:], lambda bi, i: (bi, 0, 0)),
    ] + [_const_spec(a.shape) for a in (kam, vatm, cm, ctm, ikm, tall, tmall, lamp, subln, wuvt, wout)]
    scratch = [
        pltpu.VMEM((nt, TQ, TQ), F32),
        pltpu.VMEM((N_META, TQ), F32),
        pltpu.VMEM((nt, TQ, TQ), BF16), pltpu.VMEM((N_META, TQ), BF16),
        pltpu.VMEM((nt, TQ, TQ), BF16), pltpu.VMEM((N_META, TQ), BF16),
        pltpu.VMEM((nt, TQ, TQ), BF16), pltpu.VMEM((N_META, TQ), BF16),
        pltpu.VMEM((1, TQ), jnp.int32),
        pltpu.VMEM((8, TQ), F32),
        pltpu.VMEM((A_HEADS, 2 * A_DK, A_LANES), BF16),
        pltpu.VMEM((A_HEADS, KEY_TILES_PER_STEP * TQ, A_LANES), F32),
        pltpu.VMEM((B_HEADS, KEY_TILES_PER_STEP * TQ, TQ), F32),
        pltpu.VMEM((1, A_HEADS * A_LANES), F32),
        pltpu.VMEM((1, A_HEADS * A_LANES), F32),
        pltpu.VMEM((A_DV, A_HEADS * A_LANES), F32),
        pltpu.VMEM((1, B_HEADS * TQ), F32),
        pltpu.VMEM((1, B_HEADS * TQ), F32),
        pltpu.VMEM((B_DC, B_HEADS * TQ), F32),
        pltpu.VMEM((A_HEADS * A_DV + B_HEADS * B_DV, TQ), BF16),
    ]
    return pl.pallas_call(
        _attn_kernel,
        grid=(b, nt),
        in_specs=in_specs,
        out_specs=pl.BlockSpec((1, TQ, d), lambda bi, i: (bi, i, 0)),
        out_shape=jax.ShapeDtypeStruct((b, s, d), F32),
        scratch_shapes=scratch,
        compiler_params=_params(2),
        name="attention",
    )(qat, qlt, iqt, iwt, h3d, ka, vat, c, ct, ik, kam, vatm, cm, ctm, ikm,
      tall, tmall, lamp, subln, wuvt, wout)


def kernel(x, meta_tokens, rel_bias, ffn1_norm, ffn1_w_gate, ffn1_w_up, ffn1_w_down, mix_norm, w_in, a_q_norm, a_k_norm, a_lambda_q1, a_lambda_k1, a_lambda_q2, a_lambda_k2, a_subln, b_kv_norm, b_w_uk, b_q_norm, b_w_uv, w_out, ffn2_norm, ffn2_w_gate, ffn2_w_up, ffn2_w_down):
    bsz, seq, d = x.shape
    assert d == D_MODEL and seq % PROJ_TM == 0 and (bsz * seq) % FFN_TM == 0
    assert min(IDX_TOPK_MAX, (seq + N_META) // 4) == IDX_TOPK_MAX
    assert ffn1_norm.shape[0] == 1, "single layer"
    lyr = 0
    row = lambda v: v.reshape(1, -1).astype(F32)
    col = lambda v: v.reshape(-1, 1).astype(F32)

    w = w_in[lyr]
    o_qa, o_ka, o_va, o_qb, o_c, o_iq, o_ik, o_iw = np.cumsum(
        [0, 2 * A_HEADS * A_DK, 2 * A_HEADS * A_DK, A_HEADS * A_DV, B_HEADS * B_DQ, B_DC,
         IDX_HEADS * IDX_DIM, IDX_DIM])[:8]
    w_qa, w_ka, w_va, w_qb = w[:, o_qa:o_ka], w[:, o_ka:o_va], w[:, o_va:o_qb], w[:, o_qb:o_c]
    w_c, w_iq, w_ik, w_iw = w[:, o_c:o_iq], w[:, o_iq:o_ik], w[:, o_ik:o_iw], w[:, o_iw:o_iw + IDX_HEADS]
    wt = jnp.concatenate([w_qa, w_va, w_qb, w_c, w_iq, w_iw,
                          jnp.zeros((d, _T_END - _T_IW - IDX_HEADS), w.dtype)], axis=1).T.astype(BF16)
    wr = jnp.concatenate([w_ka, w_c, w_ik, jnp.zeros((d, _R_END - _R_IK - IDX_DIM), w.dtype)],
                         axis=1).astype(BF16)
    wukt = jnp.transpose(b_w_uk[lyr], (0, 2, 1)).astype(BF16)
    wuvt = jnp.transpose(b_w_uv[lyr], (0, 2, 1)).astype(BF16)
    gq = col(jnp.tile(a_q_norm[lyr], 2 * A_HEADS)) * (A_SCALE * LOG2E)
    gk = row(a_k_norm[lyr])
    gql = col(b_q_norm[lyr]) * (B_SCALE * LOG2E)
    gcr, gcc = row(b_kv_norm[lyr]), col(b_kv_norm[lyr])
    subln = col(a_subln[lyr]) * (1.0 - LAM_INIT)
    lamp = jnp.stack([a_lambda_q1[lyr], a_lambda_k1[lyr], a_lambda_q2[lyr], a_lambda_k2[lyr]]).astype(F32)
    proj_consts = (row(mix_norm[lyr]), wt, wr, wukt, gq, gk, gql, gcr, gcc)

    tall, tmall = _bias_tiles(rel_bias.astype(F32))

    ffn1 = (row(ffn1_norm[lyr]), ffn1_w_gate[lyr].astype(BF16), ffn1_w_up[lyr].astype(BF16),
            ffn1_w_down[lyr].astype(BF16))
    h1 = _ffn(x.reshape(bsz * seq, d), *ffn1, tm=FFN_TM).reshape(bsz, seq, d)
    meta_pad = jnp.pad(meta_tokens.astype(x.dtype), ((0, META_PAD - N_META), (0, 0)))
    h1m = _ffn(meta_pad, *ffn1, tm=META_PAD).reshape(1, META_PAD, d)

    qat, ka, vat, qlt, c, ct, iqt, ik, iwt = _inproj(h1, *proj_consts, tm=PROJ_TM, tw=TQ)
    _, kam, vatm, _, cm, ctm, _, ikm, _ = _inproj(h1m, *proj_consts, tm=META_PAD, tw=META_PAD)

    h2 = _attention((qat, qlt, iqt, iwt), h1, (ka, vat, c, ct, ik), (kam, vatm, cm, ctm, ikm),
                    tall, tmall, lamp, subln, wuvt, w_out[lyr].astype(BF16))

    ffn2 = (row(ffn2_norm[lyr]), ffn2_w_gate[lyr].astype(BF16), ffn2_w_up[lyr].astype(BF16),
            ffn2_w_down[lyr].astype(BF16))
    return _ffn(h2.reshape(bsz * seq, d), *ffn2, tm=FFN_TM).reshape(bsz, seq, d)
```

```python
import functools
import math

import jax
import jax.numpy as jnp
import numpy as np
from jax import lax
from jax.experimental import pallas as pl
from jax.experimental.pallas import tpu as pltpu

F32 = jnp.float32
BF16 = jnp.bfloat16

D_MODEL = 1024
N_META = 16
A_HEADS = 4
A_DK = 64
A_DV = 128
B_HEADS = 4
B_DQ = 128
B_DC = 256
B_DV = 128
IDX_HEADS = 8
IDX_DIM = 64
IDX_TOPK_MAX = 256
REL_BUCKETS = 32
REL_MAX_DIST = 128
D_FF = 2816
EPS = 1e-6
LOG2E = math.log2(math.e)
A_SCALE = A_DK ** -0.5
B_SCALE = B_DC ** -0.5
LAM_INIT = 0.8 - 0.6 * math.exp(-0.3 * 0)

NEG = -1e30
INT_MIN = -(2 ** 31)

TQ = 256
META_PAD = 128
FF_CHUNK = 1408
FFN_TM = 512
PROJ_TM = 512
V7X_VMEM_LIMIT = 56 * 1024 * 1024

KEY_TILES_PER_STEP = 4
A_LANES = 2 * TQ
PACK = 16
assert N_META % PACK == 0
_T_QA, _T_VA, _T_QB, _T_C, _T_IQ, _T_IW, _T_END = 0, 512, 1024, 1536, 1792, 2304, 2320
_R_KA, _R_C, _R_IK, _R_END = 0, 512, 768, 896


def _bucket_edges():
    max_exact = REL_BUCKETS // 2
    n = np.arange(0, 4 * REL_MAX_DIST)
    nf = np.maximum(n, 1).astype(np.float64)
    large = max_exact + (np.log(nf / max_exact) / math.log(REL_MAX_DIST / max_exact)
                         * (REL_BUCKETS - max_exact)).astype(np.int64)
    large = np.minimum(large, REL_BUCKETS - 1)
    bucket = np.where(n < max_exact, n, large)
    return [int(n[np.argmax(bucket >= j)]) for j in range(REL_BUCKETS)]


_EDGES = _bucket_edges()
assert _EDGES[-1] <= TQ, "tiles two or more away from the diagonal must sit in the last bucket"


def _const_spec(shape):
    nd = len(shape)
    return pl.BlockSpec(shape, lambda *_: (0,) * nd, pipeline_mode=pl.Buffered(1))


def _params(n_grid):
    return pltpu.CompilerParams(dimension_semantics=("arbitrary",) * n_grid,
                                vmem_limit_bytes=V7X_VMEM_LIMIT)


def _rms_lanes(x, g_row):
    ms = jnp.mean(x * x, axis=-1, keepdims=True)
    return x * lax.rsqrt(ms + EPS) * g_row


def _bias_kernel(rb_ref, tall_ref, tmall_ref):
    h = pl.program_id(0)

    def table(d):
        val = jnp.full(d.shape, rb_ref[0, h] * LOG2E, F32)
        for j in range(1, REL_BUCKETS):
            val = jnp.where(d >= _EDGES[j], rb_ref[j, h] * LOG2E, val)
        return val

    far = rb_ref[REL_BUCKETS - 1, h] * LOG2E
    c = lax.broadcasted_iota(jnp.int32, (TQ, TQ), 0)
    r = lax.broadcasted_iota(jnp.int32, (TQ, TQ), 1)
    d0 = r - c
    tall_ref[0, 0] = jnp.where(d0 >= 0, table(d0), NEG)
    tall_ref[0, 1] = table(d0 + TQ)
    tall_ref[0, 2] = jnp.full((TQ, TQ), far, F32)
    m = lax.broadcasted_iota(jnp.int32, (N_META, TQ), 0)
    r2 = lax.broadcasted_iota(jnp.int32, (N_META, TQ), 1)
    tmall_ref[0, 0] = table(r2 + N_META - m)
    tmall_ref[0, 1] = jnp.full((N_META, TQ), far, F32)


def _bias_tiles(rel_bias):
    nh = rel_bias.shape[1]
    return pl.pallas_call(
        _bias_kernel,
        grid=(nh,),
        in_specs=[pl.BlockSpec(memory_space=pltpu.SMEM)],
        out_specs=[pl.BlockSpec((1, 3, TQ, TQ), lambda h: (h, 0, 0, 0)),
                   pl.BlockSpec((1, 2, N_META, TQ), lambda h: (h, 0, 0, 0))],
        out_shape=[jax.ShapeDtypeStruct((nh, 3, TQ, TQ), F32),
                   jax.ShapeDtypeStruct((nh, 2, N_META, TQ), F32)],
        compiler_params=_params(1),
        name="bias_tiles",
    )(rel_bias)


def _ffn_kernel(x_ref, g_ref, wg_ref, wu_ref, wd_ref, o_ref):
    x = x_ref[...]
    xn = _rms_lanes(x, g_ref[...]).astype(BF16)
    y = None
    for c in range(D_FF // FF_CHUNK):
        sl = slice(c * FF_CHUNK, (c + 1) * FF_CHUNK)
        g = jnp.dot(xn, wg_ref[:, sl], preferred_element_type=F32)
        u = jnp.dot(xn, wu_ref[:, sl], preferred_element_type=F32)
        a = (g * jax.nn.sigmoid(g) * u).astype(BF16)
        part = jnp.dot(a, wd_ref[sl, :], preferred_element_type=F32)
        y = part if y is None else y + part
    o_ref[...] = x + 0.5 * y


def _ffn(h2d, g_row, wg, wu, wd, tm):
    m, d = h2d.shape
    return pl.pallas_call(
        _ffn_kernel,
        grid=(m // tm,),
        in_specs=[pl.BlockSpec((tm, d), lambda i: (i, 0)),
                  _const_spec((1, d)),
                  _const_spec(wg.shape), _const_spec(wu.shape), _const_spec(wd.shape)],
        out_specs=pl.BlockSpec((tm, d), lambda i: (i, 0)),
        out_shape=jax.ShapeDtypeStruct((m, d), F32),
        compiler_params=_params(1),
        name="ffn",
    )(h2d, g_row, wg, wu, wd)


def _inproj_kernel(h_ref, g_ref, wt_ref, wr_ref, wukt_ref, gq_ref, gk_ref, gql_ref, gcr_ref, gcc_ref,
                   qat_ref, ka_ref, vat_ref, qlt_ref, c_ref, ct_ref, iqt_ref, ik_ref, iwt_ref, *, tm, tw):
    xn = _rms_lanes(h_ref[0], g_ref[...]).astype(BF16)
    subtiles = [(t, slice(t * tw, (t + 1) * tw)) for t in range(tm // tw)]

    def proj_t(r0, r1):
        return lax.dot_general(wt_ref[r0:r1, :], xn, (((1,), (1,)), ((), ())), preferred_element_type=F32)

    def proj_r(c0, c1):
        return jnp.dot(xn, wr_ref[:, c0:c1], preferred_element_type=F32)

    ka = proj_r(_R_KA, _R_C)
    for hm in range(2 * A_HEADS):
        cols = slice(hm * A_DK, (hm + 1) * A_DK)
        ka_ref[0, :, cols] = _rms_lanes(ka[:, cols], gk_ref[...]).astype(BF16)

    qa = proj_t(_T_QA, _T_VA).reshape(2 * A_HEADS, A_DK, tm)
    qa = qa * lax.rsqrt(jnp.mean(qa * qa, axis=1, keepdims=True) + EPS)
    qat_ref[0] = (qa.reshape(2 * A_HEADS * A_DK, tm) * gq_ref[...]).astype(BF16)

    qb = proj_t(_T_QB, _T_C).astype(BF16)
    for h in range(B_HEADS):
        ql = jnp.dot(wukt_ref[h], qb[h * B_DQ:(h + 1) * B_DQ], preferred_element_type=F32)
        ql = (ql * lax.rsqrt(jnp.mean(ql * ql, axis=0, keepdims=True) + EPS) * gql_ref[...]).astype(BF16)
        for t, sl in subtiles:
            qlt_ref[0, t, :, h * tw:(h + 1) * tw] = ql[:, sl]

    c_ref[0] = _rms_lanes(proj_r(_R_C, _R_IK), gcr_ref[...]).astype(BF16)

    ct = proj_t(_T_C, _T_IQ)
    ct = (ct * lax.rsqrt(jnp.mean(ct * ct, axis=0, keepdims=True) + EPS) * gcc_ref[...]).astype(BF16)
    for t, sl in subtiles:
        ct_ref[0, t] = ct[:, sl]

    va = proj_t(_T_VA, _T_QB).astype(BF16)
    for t, sl in subtiles:
        vat_ref[0, t] = va[:, sl]

    iq = proj_t(_T_IQ, _T_END)
    for h in range(IDX_HEADS):
        iqh = iq[h * IDX_DIM:(h + 1) * IDX_DIM].astype(BF16)
        for t, sl in subtiles:
            iqt_ref[0, t, :, h * tw:(h + 1) * tw] = iqh[:, sl]
    iw0 = _T_IW - _T_IQ
    iwt_ref[0] = iq[iw0:iw0 + IDX_HEADS] * (IDX_HEADS ** -0.5 * IDX_DIM ** -0.5)

    ik_ref[0] = proj_r(_R_IK, _R_END)[:, 0:IDX_DIM].astype(BF16)


def _inproj(h3d, g_row, wt, wr, wukt, gq, gk, gql, gcr, gcc, tm, tw):
    b, s, d = h3d.shape
    nt = s // tw
    kern = functools.partial(_inproj_kernel, tm=tm, tw=tw)
    out_shape = [
        jax.ShapeDtypeStruct((b, 2 * A_HEADS * A_DK, s), BF16),
        jax.ShapeDtypeStruct((b, s, 2 * A_HEADS * A_DK), BF16),
        jax.ShapeDtypeStruct((b, nt, A_HEADS * A_DV, tw), BF16),
        jax.ShapeDtypeStruct((b, nt, B_DC, B_HEADS * tw), BF16),
        jax.ShapeDtypeStruct((b, s, B_DC), BF16),
        jax.ShapeDtypeStruct((b, nt, B_DC, tw), BF16),
        jax.ShapeDtypeStruct((b, nt, IDX_DIM, IDX_HEADS * tw), BF16),
        jax.ShapeDtypeStruct((b, s, IDX_DIM), BF16),
        jax.ShapeDtypeStruct((b, IDX_HEADS, s), F32),
    ]
    k = tm // tw
    out_specs = [
        pl.BlockSpec((1, 2 * A_HEADS * A_DK, tm), lambda bi, i: (bi, 0, i)),
        pl.BlockSpec((1, tm, 2 * A_HEADS * A_DK), lambda bi, i: (bi, i, 0)),
        pl.BlockSpec((1, k, A_HEADS * A_DV, tw), lambda bi, i: (bi, i, 0, 0)),
        pl.BlockSpec((1, k, B_DC, B_HEADS * tw), lambda bi, i: (bi, i, 0, 0)),
        pl.BlockSpec((1, tm, B_DC), lambda bi, i: (bi, i, 0)),
        pl.BlockSpec((1, k, B_DC, tw), lambda bi, i: (bi, i, 0, 0)),
        pl.BlockSpec((1, k, IDX_DIM, IDX_HEADS * tw), lambda bi, i: (bi, i, 0, 0)),
        pl.BlockSpec((1, tm, IDX_DIM), lambda bi, i: (bi, i, 0)),
        pl.BlockSpec((1, IDX_HEADS, tm), lambda bi, i: (bi, 0, i)),
    ]
    consts = (g_row, wt, wr, wukt, gq, gk, gql, gcr, gcc)
    return pl.pallas_call(
        kern,
        grid=(b, s // tm),
        in_specs=[pl.BlockSpec((1, tm, d), lambda bi, i: (bi, i, 0))] + [_const_spec(a.shape) for a in consts],
        out_specs=out_specs,
        out_shape=out_shape,
        compiler_params=_params(2),
        name="inproj",
    )(h3d, *consts)


def _key_to_f32(u):
    bits = u ^ (jnp.right_shift(u, 31) & 0x7FFFFFFF)
    return lax.bitcast_convert_type(bits, F32)


def _online(s_ref, lanes, m_ref, l_ref, acc_ref, v_lhs, first=False):
    if first:
        m_new = jnp.max(s_ref[...], axis=0, keepdims=True)
        m_ref[:, lanes] = m_new
        p = jnp.exp2(s_ref[...] - m_new)
        l_ref[:, lanes] = jnp.sum(p, axis=0, keepdims=True)
        acc_ref[:, lanes] = jnp.dot(v_lhs, p.astype(BF16), preferred_element_type=F32)
        return
    m_old = m_ref[:, lanes]
    m_new = jnp.maximum(m_old, jnp.max(s_ref[...], axis=0, keepdims=True))
    m_ref[:, lanes] = m_new
    alpha = jnp.exp2(m_old - m_new)
    p = jnp.exp2(s_ref[...] - m_new)
    l_ref[:, lanes] = alpha * l_ref[:, lanes] + jnp.sum(p, axis=0, keepdims=True)
    acc_ref[:, lanes] = alpha * acc_ref[:, lanes] + jnp.dot(v_lhs, p.astype(BF16),
                                                           preferred_element_type=F32)


def _attn_kernel(qat_ref, qlt_ref, iqt_ref, iwt_ref, h_ref,
                 ka_ref, vat_ref, c_ref, ct_ref, ik_ref,
                 kam_ref, vatm_ref, cm_ref, ctm_ref, ikm_ref,
                 tall_ref, tmall_ref, lamp_ref, subln_ref, wuvt_ref, wout_ref,
                 o_ref,
                 st_ref, stm_ref, hi_ref, him_ref, b1_ref, b1m_ref, b0_ref, b0m_ref,
                 key_ref, cnt_ref, qbd_ref, sa_ref, sd_ref, m_ref, l_ref, acc_ref, m2_ref, l2_ref, acc2_ref, ot_ref):
    i = pl.program_id(1)
    n_tiles = i + 1
    midx = jnp.minimum(i, 1)

    zq = jnp.zeros((A_DK, TQ), BF16)
    for h in range(A_HEADS):
        q0 = qat_ref[0, (2 * h) * A_DK:(2 * h + 1) * A_DK, :]
        q1 = qat_ref[0, (2 * h + 1) * A_DK:(2 * h + 2) * A_DK, :]
        qbd_ref[h, 0:A_DK, 0:TQ] = q0
        qbd_ref[h, 0:A_DK, TQ:A_LANES] = zq
        qbd_ref[h, A_DK:2 * A_DK, 0:TQ] = zq
        qbd_ref[h, A_DK:2 * A_DK, TQ:A_LANES] = q1

    def a_tile(k_get, v_get, bias_get, first=False):
        rows = v_get(0).shape[1]
        for h in range(A_HEADS):
            bias = bias_get(h)
            sa_ref[h, 0:rows, :] = (jnp.dot(k_get(h), qbd_ref[h], preferred_element_type=F32)
                                    + jnp.concatenate([bias, bias], axis=1))
        for h in range(A_HEADS):
            _online(sa_ref.at[h, 0:rows, :], slice(h * A_LANES, (h + 1) * A_LANES),
                    m_ref, l_ref, acc_ref, v_get(h), first)

    qrel = i * TQ + lax.broadcasted_iota(jnp.int32, (1, TQ), 1)

    def idx_scores(ik_tile):
        d = jnp.dot(ik_tile, iqt_ref[0, 0], preferred_element_type=F32)
        acc = jnp.zeros((ik_tile.shape[0], TQ), F32)
        for h in range(IDX_HEADS):
            acc = acc + iwt_ref[0, h:h + 1, :] * jnp.maximum(d[:, h * TQ:(h + 1) * TQ], 0.0)
        return acc

    def store_digits(st, hi_dst, b1_dst, b0_dst):
        bits = lax.bitcast_convert_type(st, jnp.int32)
        key = bits ^ (jnp.right_shift(bits, 31) & 0x7FFFFFFF)
        hi_dst[...] = lax.bitcast_convert_type(bits & jnp.int32(-65536), F32).astype(BF16)
        b1_dst[...] = (jnp.right_shift(key, 8) & 255).astype(F32).astype(BF16)
        b0_dst[...] = (key & 255).astype(F32).astype(BF16)

    mrow = lax.broadcasted_iota(jnp.int32, (N_META, TQ), 0)
    trow = lax.broadcasted_iota(jnp.int32, (TQ, TQ), 0)

    a_tile(lambda h: kam_ref[0, 0:N_META, h * 2 * A_DK:(h + 1) * 2 * A_DK],
           lambda h: vatm_ref[0, 0, h * A_DV:(h + 1) * A_DV, 0:N_META],
           lambda h: tmall_ref[h, midx], first=True)
    sm = idx_scores(ikm_ref[0, 0:N_META, :])
    stm_ref[...] = sm
    store_digits(sm, him_ref, b1m_ref, b0m_ref)

    assert KEY_TILES_PER_STEP == 4
    n_quads = jnp.right_shift(n_tiles, 2)
    has_pair = (n_tiles & 2) != 0
    has_tail = (n_tiles & 1) != 0
    cat = lambda parts, axis: parts[0] if len(parts) == 1 else jnp.concatenate(parts, axis=axis)

    def walk_tiles(visit):
        def body(q, carry):
            visit([KEY_TILES_PER_STEP * q + t for t in range(KEY_TILES_PER_STEP)])
            return carry

        lax.fori_loop(0, n_quads, body, 0)

        @pl.when(has_pair)
        def _():
            visit([KEY_TILES_PER_STEP * n_quads, KEY_TILES_PER_STEP * n_quads + 1])

        @pl.when(has_tail)
        def _():
            visit([n_tiles - 1])

    def key_rows(js):
        return pl.ds(pl.multiple_of(js[0] * TQ, TQ), len(js) * TQ)

    def bias_rows(js):
        kidx = [jnp.minimum(i - j, 2) for j in js]
        return lambda h: cat([tall_ref[h, k] for k in kidx], 0)

    def ab_tiles(js):
        rows = key_rows(js)
        a_tile(lambda h: ka_ref[0, rows, h * 2 * A_DK:(h + 1) * 2 * A_DK],
               lambda h: cat([vat_ref[0, j, h * A_DV:(h + 1) * A_DV, :] for j in js], 1),
               bias_rows(js))
        for t0 in range(0, len(js), 2):
            jj = js[t0:t0 + 2]
            sc = idx_scores(ik_ref[0, key_rows(jj), :])
            for t, j in enumerate(jj):
                sx = jnp.where(j * TQ + trow <= qrel, sc[t * TQ:(t + 1) * TQ], -jnp.inf)
                st_ref[j] = sx
                store_digits(sx, hi_ref.at[j], b1_ref.at[j], b0_ref.at[j])

    walk_tiles(ab_tiles)

    lp = lamp_ref[...]
    lam = (jnp.exp(jnp.sum(lp[0:1] * lp[1:2], axis=-1, keepdims=True))
           - jnp.exp(jnp.sum(lp[2:3] * lp[3:4], axis=-1, keepdims=True)) + LAM_INIT)
    for h in range(A_HEADS):
        l0 = slice(h * A_LANES, h * A_LANES + TQ)
        l1 = slice(h * A_LANES + TQ, (h + 1) * A_LANES)
        o = acc_ref[:, l0] * (1.0 / l_ref[:, l0]) - acc_ref[:, l1] * (lam / l_ref[:, l1])
        o = o * lax.rsqrt(jnp.mean(o * o, axis=0, keepdims=True) + EPS) * subln_ref[...]
        ot_ref[h * A_DV:(h + 1) * A_DV, :] = o.astype(BF16)

    kf = jnp.minimum(IDX_TOPK_MAX, qrel + (N_META + 1)).astype(F32)
    one, zero = jnp.full((), 1, BF16), jnp.full((), 0, BF16)

    ge = lambda x, t: x >= t
    gt = lambda x, t: x > t
    as_f32 = lambda d: d.astype(F32)

    def top_half_value(w):
        k = w - 32768
        b16 = (k ^ (jnp.right_shift(k, 15) & 0x7FFF)) & 0xFFFF
        return lax.bitcast_convert_type(jnp.left_shift(b16, 16), F32)

    def find_threshold(n):
        def pk_count(meta_ref, tile_ref, pred, t_row):
            tb = jnp.broadcast_to(t_row.astype(BF16), (PACK, TQ))[None]
            acc = None
            for x in [meta_ref[...]] + [tile_ref[j] for j in range(n)]:
                x3 = x.reshape(x.shape[0] // PACK, PACK, TQ)
                ind = jnp.where(pred(x3, tb), one, zero)
                for k in range(ind.shape[0]):
                    acc = ind[k] if acc is None else acc + ind[k]
            return jnp.sum(acc.astype(F32), axis=0, keepdims=True)

        def pk_keep(meta_ref, tile_ref, sel_meta_ref, sel_tile_ref, t_row):
            tb = jnp.broadcast_to(t_row.astype(BF16), (1, TQ))
            meta_ref[...] = jnp.where(sel_meta_ref[...] == tb, meta_ref[...], -one)
            for j in range(n):
                tile_ref[j] = jnp.where(sel_tile_ref[j] == tb, tile_ref[j], -one)

        def bisect(n_bits, to_value, meta_ref, tile_ref, need):
            def body(b, w):
                cand = w | jnp.left_shift(jnp.int32(1), n_bits - 1 - b)
                return jnp.where(pk_count(meta_ref, tile_ref, ge, to_value(cand)) >= need, cand, w)

            return lax.fori_loop(0, n_bits, body, jnp.zeros((1, TQ), jnp.int32))

        w_hi = bisect(16, top_half_value, him_ref, hi_ref, kf)
        need1 = kf - pk_count(him_ref, hi_ref, gt, top_half_value(w_hi))
        pk_keep(b1m_ref, b1_ref, him_ref, hi_ref, top_half_value(w_hi))
        d1 = bisect(8, as_f32, b1m_ref, b1_ref, need1)
        need0 = need1 - pk_count(b1m_ref, b1_ref, gt, as_f32(d1))
        pk_keep(b0m_ref, b0_ref, b1m_ref, b1_ref, as_f32(d1))
        d0 = bisect(8, as_f32, b0m_ref, b0_ref, need0)
        return jnp.left_shift(w_hi - 32768, 16) | jnp.left_shift(d1, 8) | d0

    assert PACK * st_ref.shape[0] + N_META // PACK <= 256, "bf16 counters must stay exact"
    for n in range(1, st_ref.shape[0] + 1):
        @pl.when(n_tiles == n)
        def _(n=n):
            key_ref[...] = find_threshold(n)

    def count(ind_meta, ind_tile):
        p = ind_meta(stm_ref[...]).reshape(N_META // 8, 8, TQ).sum(axis=0)
        p = lax.fori_loop(0, n_tiles,
                          lambda j, p: p + ind_tile(st_ref[j], j).reshape(TQ // 8, 8, TQ).sum(axis=0), p)
        return jnp.sum(p, axis=0, keepdims=True)

    def count_ge(t):
        ind = lambda x: jnp.where(x >= t, 1.0, 0.0)
        return count(ind, lambda x, j: ind(x))

    def count_gt(t):
        ind = lambda x: jnp.where(x > t, 1.0, 0.0)
        return count(ind, lambda x, j: ind(x))

    hint = _key_to_f32(key_ref[...])
    cnt_ref[0:1, :] = count_gt(hint)
    cnt_ref[1:2, :] = count_ge(hint)
    state = jnp.max(jnp.where((cnt_ref[0:1, :] >= kf) | (cnt_ref[1:2, :] < kf), 2.0,
                              jnp.where(cnt_ref[1:2, :] > kf, 1.0, 0.0)))

    @pl.when(state > 1.5)
    def _():
        def bit_body(b, u):
            cand = u + jnp.left_shift(jnp.int32(1), 31 - b)
            return jnp.where(count_ge(_key_to_f32(cand)) >= kf, cand, u)

        u = lax.fori_loop(0, 32, bit_body, jnp.full((1, TQ), INT_MIN, jnp.int32))
        key_ref[...] = u
        cnt_ref[0:1, :] = count_gt(_key_to_f32(u))
        cnt_ref[1:2, :] = count_ge(_key_to_f32(u))

    thr = _key_to_f32(key_ref[...])

    need = kf - cnt_ref[0:1, :]

    @pl.when(state > 0.5)
    def _():
        def count_ties_before(xc):
            return count(lambda x: jnp.where((x == thr) & (mrow < xc), 1.0, 0.0),
                         lambda x, j: jnp.where((x == thr) & (j * TQ + N_META + trow < xc), 1.0, 0.0))

        def pos_body(b, xs):
            cand = xs | jnp.left_shift(jnp.int32(1), 11 - b)
            return jnp.where(count_ties_before(cand) < need, cand, xs)

        last = lax.fori_loop(0, 12, pos_body, jnp.zeros((1, TQ), jnp.int32))
        sm = stm_ref[...]
        stm_ref[...] = jnp.where((sm == thr) & (mrow > last), -jnp.inf, sm)

        def drop_body(j, carry):
            sx = st_ref[j]
            st_ref[j] = jnp.where((sx == thr) & (j * TQ + N_META + trow > last), -jnp.inf, sx)
            return carry

        lax.fori_loop(0, n_tiles, drop_body, 0)

    def d_tile(st_tile, c_tile, ct_tile, bias_get, first=False):
        sel = jnp.where(st_tile >= thr, 0.0, NEG)
        rows = c_tile.shape[0]
        for h in range(B_HEADS):
            sd_ref[h, 0:rows, :] = (jnp.dot(c_tile, qlt_ref[0, 0, :, h * TQ:(h + 1) * TQ],
                                            preferred_element_type=F32)
                                    + (bias_get(A_HEADS + h) + sel))
        for h in range(B_HEADS):
            _online(sd_ref.at[h, 0:rows, :], slice(h * TQ, (h + 1) * TQ), m2_ref, l2_ref, acc2_ref,
                    ct_tile, first)

    d_tile(stm_ref[...], cm_ref[0, 0:N_META, :], ctm_ref[0, 0, :, 0:N_META], lambda h: tmall_ref[h, midx],
           first=True)

    def d_tiles(js):
        d_tile(cat([st_ref[j] for j in js], 0), c_ref[0, key_rows(js), :],
               cat([ct_ref[0, j] for j in js], 1), bias_rows(js))

    walk_tiles(d_tiles)

    for h in range(B_HEADS):
        lanes = slice(h * TQ, (h + 1) * TQ)
        olat = (acc2_ref[:, lanes] * (1.0 / l2_ref[:, lanes])).astype(BF16)
        ob = jnp.dot(wuvt_ref[h], olat, preferred_element_type=F32)
        r0 = A_HEADS * A_DV + h * B_DV
        ot_ref[r0:r0 + B_DV, :] = ob.astype(BF16)

    y = lax.dot_general(ot_ref[...], wout_ref[...], (((0,), (0,)), ((), ())),
                        preferred_element_type=F32)
    o_ref[0] = h_ref[0] + y


def _attention(q_side, h3d, k_side, meta_side, tall, tmall, lamp, subln, wuvt, wout):
    qat, qlt, iqt, iwt = q_side
    ka, vat, c, ct, ik = k_side
    kam, vatm, cm, ctm, ikm = meta_side
    b, s, d = h3d.shape
    nt = s // TQ
    in_specs = [
        pl.BlockSpec((1, qat.shape[1], TQ), lambda bi, i: (bi, 0, i)),
        pl.BlockSpec((1, 1) + qlt.shape[2:], lambda bi, i: (bi, i, 0, 0)),
        pl.BlockSpec((1, 1) + iqt.shape[2:], lambda bi, i: (bi, i, 0, 0)),
        pl.BlockSpec((1, iwt.shape[1], TQ), lambda bi, i: (bi, 0, i)),
        pl.BlockSpec((1, TQ, d), lambda bi, i: (bi, i, 0)),
        pl.BlockSpec((1,) + ka.shape[1:], lambda bi, i: (bi, 0, 0)),
        pl.BlockSpec((1,) + vat.shape[1:], lambda bi, i: (bi, 0, 0, 0)),
        pl.BlockSpec((1,) + c.shape[1:], lambda bi, i: (bi, 0, 0)),
        pl.BlockSpec((1,) + ct.shape[1:], lambda bi, i: (bi, 0, 0, 0)),
        pl.BlockSpec((1,) + ik.shape[1:], lambda bi, i: (bi, 0, 0)),
    ] + [_const_spec(a.shape) for a in (kam, vatm, cm, ctm, ikm, tall, tmall, lamp, subln, wuvt, wout)]
    scratch = [
        pltpu.VMEM((nt, TQ, TQ), F32),
        pltpu.VMEM((N_META, TQ), F32),
        pltpu.VMEM((nt, TQ, TQ), BF16), pltpu.VMEM((N_META, TQ), BF16),
        pltpu.VMEM((nt, TQ, TQ), BF16), pltpu.VMEM((N_META, TQ), BF16),
        pltpu.VMEM((nt, TQ, TQ), BF16), pltpu.VMEM((N_META, TQ), BF16),
        pltpu.VMEM((1, TQ), jnp.int32),
        pltpu.VMEM((8, TQ), F32),
        pltpu.VMEM((A_HEADS, 2 * A_DK, A_LANES), BF16),
        pltpu.VMEM((A_HEADS, KEY_TILES_PER_STEP * TQ, A_LANES), F32),
        pltpu.VMEM((B_HEADS, KEY_TILES_PER_STEP * TQ, TQ), F32),
        pltpu.VMEM((1, A_HEADS * A_LANES), F32),
        pltpu.VMEM((1, A_HEADS * A_LANES), F32),
        pltpu.VMEM((A_DV, A_HEADS * A_LANES), F32),
        pltpu.VMEM((1, B_HEADS * TQ), F32),
        pltpu.VMEM((1, B_HEADS * TQ), F32),
        pltpu.VMEM((B_DC, B_HEADS * TQ), F32),
        pltpu.VMEM((A_HEADS * A_DV + B_HEADS * B_DV, TQ), BF16),
    ]
    return pl.pallas_call(
        _attn_kernel,
        grid=(b, nt),
        in_specs=in_specs,
        out_specs=pl.BlockSpec((1, TQ, d), lambda bi, i: (bi, i, 0)),
        out_shape=jax.ShapeDtypeStruct((b, s, d), F32),
        scratch_shapes=scratch,
        compiler_params=_params(2),
        name="attention",
    )(qat, qlt, iqt, iwt, h3d, ka, vat, c, ct, ik, kam, vatm, cm, ctm, ikm,
      tall, tmall, lamp, subln, wuvt, wout)


def kernel(x, meta_tokens, rel_bias, ffn1_norm, ffn1_w_gate, ffn1_w_up, ffn1_w_down, mix_norm, w_in, a_q_norm, a_k_norm, a_lambda_q1, a_lambda_k1, a_lambda_q2, a_lambda_k2, a_subln, b_kv_norm, b_w_uk, b_q_norm, b_w_uv, w_out, ffn2_norm, ffn2_w_gate, ffn2_w_up, ffn2_w_down):
    bsz, seq, d = x.shape
    assert d == D_MODEL and seq % PROJ_TM == 0 and (bsz * seq) % FFN_TM == 0
    assert min(IDX_TOPK_MAX, (seq + N_META) // 4) == IDX_TOPK_MAX
    assert ffn1_norm.shape[0] == 1, "single layer"
    lyr = 0
    row = lambda v: v.reshape(1, -1).astype(F32)
    col = lambda v: v.reshape(-1, 1).astype(F32)

    w = w_in[lyr]
    o_qa, o_ka, o_va, o_qb, o_c, o_iq, o_ik, o_iw = np.cumsum(
        [0, 2 * A_HEADS * A_DK, 2 * A_HEADS * A_DK, A_HEADS * A_DV, B_HEADS * B_DQ, B_DC,
         IDX_HEADS * IDX_DIM, IDX_DIM])[:8]
    w_qa, w_ka, w_va, w_qb = w[:, o_qa:o_ka], w[:, o_ka:o_va], w[:, o_va:o_qb], w[:, o_qb:o_c]
    w_c, w_iq, w_ik, w_iw = w[:, o_c:o_iq], w[:, o_iq:o_ik], w[:, o_ik:o_iw], w[:, o_iw:o_iw + IDX_HEADS]
    wt = jnp.concatenate([w_qa, w_va, w_qb, w_c, w_iq, w_iw,
                          jnp.zeros((d, _T_END - _T_IW - IDX_HEADS), w.dtype)], axis=1).T.astype(BF16)
    wr = jnp.concatenate([w_ka, w_c, w_ik, jnp.zeros((d, _R_END - _R_IK - IDX_DIM), w.dtype)],
                         axis=1).astype(BF16)
    wukt = jnp.transpose(b_w_uk[lyr], (0, 2, 1)).astype(BF16)
    wuvt = jnp.transpose(b_w_uv[lyr], (0, 2, 1)).astype(BF16)
    gq = col(jnp.tile(a_q_norm[lyr], 2 * A_HEADS)) * (A_SCALE * LOG2E)
    gk = row(a_k_norm[lyr])
    gql = col(b_q_norm[lyr]) * (B_SCALE * LOG2E)
    gcr, gcc = row(b_kv_norm[lyr]), col(b_kv_norm[lyr])
    subln = col(a_subln[lyr]) * (1.0 - LAM_INIT)
    lamp = jnp.stack([a_lambda_q1[lyr], a_lambda_k1[lyr], a_lambda_q2[lyr], a_lambda_k2[lyr]]).astype(F32)
    proj_consts = (row(mix_norm[lyr]), wt, wr, wukt, gq, gk, gql, gcr, gcc)

    tall, tmall = _bias_tiles(rel_bias.astype(F32))

    ffn1 = (row(ffn1_norm[lyr]), ffn1_w_gate[lyr].astype(BF16), ffn1_w_up[lyr].astype(BF16),
            ffn1_w_down[lyr].astype(BF16))
    h1 = _ffn(x.reshape(bsz * seq, d), *ffn1, tm=FFN_TM).reshape(bsz, seq, d)
    meta_pad = jnp.pad(meta_tokens.astype(x.dtype), ((0, META_PAD - N_META), (0, 0)))
    h1m = _ffn(meta_pad, *ffn1, tm=META_PAD).reshape(1, META_PAD, d)

    qat, ka, vat, qlt, c, ct, iqt, ik, iwt = _inproj(h1, *proj_consts, tm=PROJ_TM, tw=TQ)
    _, kam, vatm, _, cm, ctm, _, ikm, _ = _inproj(h1m, *proj_consts, tm=META_PAD, tw=META_PAD)

    h2 = _attention((qat, qlt, iqt, iwt), h1, (ka, vat, c, ct, ik), (kam, vatm, cm, ctm, ikm),
                    tall, tmall, lamp, subln, wuvt, w_out[lyr].astype(BF16))

    ffn2 = (row(ffn2_norm[lyr]), ffn2_w_gate[lyr].astype(BF16), ffn2_w_up[lyr].astype(BF16),
            ffn2_w_down[lyr].astype(BF16))
    return _ffn(h2.reshape(bsz * seq, d), *ffn2, tm=FFN_TM).reshape(bsz, seq, d)
```

```python
import functools
import math

import jax
import jax.numpy as jnp
import numpy as np
from jax import lax
from jax.experimental import pallas as pl
from jax.experimental.pallas import tpu as pltpu

F32 = jnp.float32
BF16 = jnp.bfloat16

D_MODEL = 1024
N_META = 16
A_HEADS = 4
A_DK = 64
A_DV = 128
B_HEADS = 4
B_DQ = 128
B_DC = 256
B_DV = 128
IDX_HEADS = 8
IDX_DIM = 64
IDX_TOPK_MAX = 256
REL_BUCKETS = 32
REL_MAX_DIST = 128
D_FF = 2816
EPS = 1e-6
LOG2E = math.log2(math.e)
A_SCALE = A_DK ** -0.5
B_SCALE = B_DC ** -0.5
LAM_INIT = 0.8 - 0.6 * math.exp(-0.3 * 0)

NEG = -1e30
INT_MIN = -(2 ** 31)

TQ = 256
META_PAD = 128
FF_CHUNK = 1408
FFN_TM = 512
PROJ_TM = 512
V7X_VMEM_LIMIT = 56 * 1024 * 1024

KEY_TILES_PER_STEP = 4
A_LANES = 2 * TQ
PACK = 16
COUNT_CHAINS = 4
assert N_META % PACK == 0
_T_QA, _T_VA, _T_QB, _T_C, _T_IQ, _T_IW, _T_END = 0, 512, 1024, 1536, 1792, 2304, 2320
_R_KA, _R_C, _R_IK, _R_END = 0, 512, 768, 896


def _bucket_edges():
    max_exact = REL_BUCKETS // 2
    n = np.arange(0, 4 * REL_MAX_DIST)
    nf = np.maximum(n, 1).astype(np.float64)
    large = max_exact + (np.log(nf / max_exact) / math.log(REL_MAX_DIST / max_exact)
                         * (REL_BUCKETS - max_exact)).astype(np.int64)
    large = np.minimum(large, REL_BUCKETS - 1)
    bucket = np.where(n < max_exact, n, large)
    return [int(n[np.argmax(bucket >= j)]) for j in range(REL_BUCKETS)]


_EDGES = _bucket_edges()
assert _EDGES[-1] <= TQ, "tiles two or more away from the diagonal must sit in the last bucket"


def _const_spec(shape):
    nd = len(shape)
    return pl.BlockSpec(shape, lambda *_: (0,) * nd, pipeline_mode=pl.Buffered(1))


def _params(n_grid):
    return pltpu.CompilerParams(dimension_semantics=("arbitrary",) * n_grid,
                                vmem_limit_bytes=V7X_VMEM_LIMIT)


def _rms_lanes(x, g_row):
    ms = jnp.mean(x * x, axis=-1, keepdims=True)
    return x * lax.rsqrt(ms + EPS) * g_row


def _bias_kernel(rb_ref, tall_ref, tmall_ref):
    h = pl.program_id(0)

    def table(d):
        val = jnp.full(d.shape, rb_ref[0, h] * LOG2E, F32)
        for j in range(1, REL_BUCKETS):
            val = jnp.where(d >= _EDGES[j], rb_ref[j, h] * LOG2E, val)
        return val

    far = rb_ref[REL_BUCKETS - 1, h] * LOG2E
    c = lax.broadcasted_iota(jnp.int32, (TQ, TQ), 0)
    r = lax.broadcasted_iota(jnp.int32, (TQ, TQ), 1)
    d0 = r - c
    tall_ref[0, 0] = jnp.where(d0 >= 0, table(d0), NEG)
    tall_ref[0, 1] = table(d0 + TQ)
    tall_ref[0, 2] = jnp.full((TQ, TQ), far, F32)
    m = lax.broadcasted_iota(jnp.int32, (N_META, TQ), 0)
    r2 = lax.broadcasted_iota(jnp.int32, (N_META, TQ), 1)
    tmall_ref[0, 0] = table(r2 + N_META - m)
    tmall_ref[0, 1] = jnp.full((N_META, TQ), far, F32)


def _bias_tiles(rel_bias):
    nh = rel_bias.shape[1]
    return pl.pallas_call(
        _bias_kernel,
        grid=(nh,),
        in_specs=[pl.BlockSpec(memory_space=pltpu.SMEM)],
        out_specs=[pl.BlockSpec((1, 3, TQ, TQ), lambda h: (h, 0, 0, 0)),
                   pl.BlockSpec((1, 2, N_META, TQ), lambda h: (h, 0, 0, 0))],
        out_shape=[jax.ShapeDtypeStruct((nh, 3, TQ, TQ), F32),
                   jax.ShapeDtypeStruct((nh, 2, N_META, TQ), F32)],
        compiler_params=_params(1),
        name="bias_tiles",
    )(rel_bias)


def _ffn_kernel(x_ref, g_ref, wg_ref, wu_ref, wd_ref, o_ref):
    x = x_ref[...]
    xn = _rms_lanes(x, g_ref[...]).astype(BF16)
    y = None
    for c in range(D_FF // FF_CHUNK):
        sl = slice(c * FF_CHUNK, (c + 1) * FF_CHUNK)
        g = jnp.dot(xn, wg_ref[:, sl], preferred_element_type=F32)
        u = jnp.dot(xn, wu_ref[:, sl], preferred_element_type=F32)
        a = (g * jax.nn.sigmoid(g) * u).astype(BF16)
        part = jnp.dot(a, wd_ref[sl, :], preferred_element_type=F32)
        y = part if y is None else y + part
    o_ref[...] = x + 0.5 * y


def _ffn(h2d, g_row, wg, wu, wd, tm):
    m, d = h2d.shape
    return pl.pallas_call(
        _ffn_kernel,
        grid=(m // tm,),
        in_specs=[pl.BlockSpec((tm, d), lambda i: (i, 0)),
                  _const_spec((1, d)),
                  _const_spec(wg.shape), _const_spec(wu.shape), _const_spec(wd.shape)],
        out_specs=pl.BlockSpec((tm, d), lambda i: (i, 0)),
        out_shape=jax.ShapeDtypeStruct((m, d), F32),
        compiler_params=_params(1),
        name="ffn",
    )(h2d, g_row, wg, wu, wd)


def _inproj_kernel(h_ref, g_ref, wt_ref, wr_ref, wukt_ref, gq_ref, gk_ref, gql_ref, gcr_ref, gcc_ref,
                   qat_ref, ka_ref, vat_ref, qlt_ref, c_ref, ct_ref, iqt_ref, ik_ref, iwt_ref, *, tm, tw):
    xn = _rms_lanes(h_ref[0], g_ref[...]).astype(BF16)
    subtiles = [(t, slice(t * tw, (t + 1) * tw)) for t in range(tm // tw)]

    def proj_t(r0, r1):
        return lax.dot_general(wt_ref[r0:r1, :], xn, (((1,), (1,)), ((), ())), preferred_element_type=F32)

    def proj_r(c0, c1):
        return jnp.dot(xn, wr_ref[:, c0:c1], preferred_element_type=F32)

    ka = proj_r(_R_KA, _R_C)
    for hm in range(2 * A_HEADS):
        cols = slice(hm * A_DK, (hm + 1) * A_DK)
        ka_ref[0, :, cols] = _rms_lanes(ka[:, cols], gk_ref[...]).astype(BF16)

    qa = proj_t(_T_QA, _T_VA).reshape(2 * A_HEADS, A_DK, tm)
    qa = qa * lax.rsqrt(jnp.mean(qa * qa, axis=1, keepdims=True) + EPS)
    qat_ref[0] = (qa.reshape(2 * A_HEADS * A_DK, tm) * gq_ref[...]).astype(BF16)

    qb = proj_t(_T_QB, _T_C).astype(BF16)
    for h in range(B_HEADS):
        ql = jnp.dot(wukt_ref[h], qb[h * B_DQ:(h + 1) * B_DQ], preferred_element_type=F32)
        ql = (ql * lax.rsqrt(jnp.mean(ql * ql, axis=0, keepdims=True) + EPS) * gql_ref[...]).astype(BF16)
        for t, sl in subtiles:
            qlt_ref[0, t, :, h * tw:(h + 1) * tw] = ql[:, sl]

    c_ref[0] = _rms_lanes(proj_r(_R_C, _R_IK), gcr_ref[...]).astype(BF16)

    ct = proj_t(_T_C, _T_IQ)
    ct = (ct * lax.rsqrt(jnp.mean(ct * ct, axis=0, keepdims=True) + EPS) * gcc_ref[...]).astype(BF16)
    for t, sl in subtiles:
        ct_ref[0, t] = ct[:, sl]

    va = proj_t(_T_VA, _T_QB).astype(BF16)
    for t, sl in subtiles:
        vat_ref[0, t] = va[:, sl]

    iq = proj_t(_T_IQ, _T_END)
    for h in range(IDX_HEADS):
        iqh = iq[h * IDX_DIM:(h + 1) * IDX_DIM].astype(BF16)
        for t, sl in subtiles:
            iqt_ref[0, t, :, h * tw:(h + 1) * tw] = iqh[:, sl]
    iw0 = _T_IW - _T_IQ
    iwt_ref[0] = iq[iw0:iw0 + IDX_HEADS] * (IDX_HEADS ** -0.5 * IDX_DIM ** -0.5)

    ik_ref[0] = proj_r(_R_IK, _R_END)[:, 0:IDX_DIM].astype(BF16)


def _inproj(h3d, g_row, wt, wr, wukt, gq, gk, gql, gcr, gcc, tm, tw):
    b, s, d = h3d.shape
    nt = s // tw
    kern = functools.partial(_inproj_kernel, tm=tm, tw=tw)
    out_shape = [
        jax.ShapeDtypeStruct((b, 2 * A_HEADS * A_DK, s), BF16),
        jax.ShapeDtypeStruct((b, s, 2 * A_HEADS * A_DK), BF16),
        jax.ShapeDtypeStruct((b, nt, A_HEADS * A_DV, tw), BF16),
        jax.ShapeDtypeStruct((b, nt, B_DC, B_HEADS * tw), BF16),
        jax.ShapeDtypeStruct((b, s, B_DC), BF16),
        jax.ShapeDtypeStruct((b, nt, B_DC, tw), BF16),
        jax.ShapeDtypeStruct((b, nt, IDX_DIM, IDX_HEADS * tw), BF16),
        jax.ShapeDtypeStruct((b, s, IDX_DIM), BF16),
        jax.ShapeDtypeStruct((b, IDX_HEADS, s), F32),
    ]
    k = tm // tw
    out_specs = [
        pl.BlockSpec((1, 2 * A_HEADS * A_DK, tm), lambda bi, i: (bi, 0, i)),
        pl.BlockSpec((1, tm, 2 * A_HEADS * A_DK), lambda bi, i: (bi, i, 0)),
        pl.BlockSpec((1, k, A_HEADS * A_DV, tw), lambda bi, i: (bi, i, 0, 0)),
        pl.BlockSpec((1, k, B_DC, B_HEADS * tw), lambda bi, i: (bi, i, 0, 0)),
        pl.BlockSpec((1, tm, B_DC), lambda bi, i: (bi, i, 0)),
        pl.BlockSpec((1, k, B_DC, tw), lambda bi, i: (bi, i, 0, 0)),
        pl.BlockSpec((1, k, IDX_DIM, IDX_HEADS * tw), lambda bi, i: (bi, i, 0, 0)),
        pl.BlockSpec((1, tm, IDX_DIM), lambda bi, i: (bi, i, 0)),
        pl.BlockSpec((1, IDX_HEADS, tm), lambda bi, i: (bi, 0, i)),
    ]
    consts = (g_row, wt, wr, wukt, gq, gk, gql, gcr, gcc)
    return pl.pallas_call(
        kern,
        grid=(b, s // tm),
        in_specs=[pl.BlockSpec((1, tm, d), lambda bi, i: (bi, i, 0))] + [_const_spec(a.shape) for a in consts],
        out_specs=out_specs,
        out_shape=out_shape,
        compiler_params=_params(2),
        name="inproj",
    )(h3d, *consts)


def _key_to_f32(u):
    bits = u ^ (jnp.right_shift(u, 31) & 0x7FFFFFFF)
    return lax.bitcast_convert_type(bits, F32)


def _online(s_ref, lanes, m_ref, l_ref, acc_ref, v_lhs, first=False):
    if first:
        m_new = jnp.max(s_ref[...], axis=0, keepdims=True)
        m_ref[:, lanes] = m_new
        p = jnp.exp2(s_ref[...] - m_new)
        l_ref[:, lanes] = jnp.sum(p, axis=0, keepdims=True)
        acc_ref[:, lanes] = jnp.dot(v_lhs, p.astype(BF16), preferred_element_type=F32)
        return
    m_old = m_ref[:, lanes]
    m_new = jnp.maximum(m_old, jnp.max(s_ref[...], axis=0, keepdims=True))
    m_ref[:, lanes] = m_new
    alpha = jnp.exp2(m_old - m_new)
    p = jnp.exp2(s_ref[...] - m_new)
    l_ref[:, lanes] = alpha * l_ref[:, lanes] + jnp.sum(p, axis=0, keepdims=True)
    acc_ref[:, lanes] = alpha * acc_ref[:, lanes] + jnp.dot(v_lhs, p.astype(BF16),
                                                           preferred_element_type=F32)


def _attn_kernel(qat_ref, qlt_ref, iqt_ref, iwt_ref, h_ref,
                 ka_ref, vat_ref, c_ref, ct_ref, ik_ref,
                 kam_ref, vatm_ref, cm_ref, ctm_ref, ikm_ref,
                 tall_ref, tmall_ref, lamp_ref, subln_ref, wuvt_ref, wout_ref,
                 o_ref,
                 st_ref, stm_ref, hi_ref, him_ref, b1_ref, b1m_ref, b0_ref, b0m_ref,
                 key_ref, cnt_ref, qbd_ref, sa_ref, sd_ref, m_ref, l_ref, acc_ref, m2_ref, l2_ref, acc2_ref, ot_ref):
    i = pl.program_id(1)
    n_tiles = i + 1
    midx = jnp.minimum(i, 1)

    zq = jnp.zeros((A_DK, TQ), BF16)
    for h in range(A_HEADS):
        q0 = qat_ref[0, (2 * h) * A_DK:(2 * h + 1) * A_DK, :]
        q1 = qat_ref[0, (2 * h + 1) * A_DK:(2 * h + 2) * A_DK, :]
        qbd_ref[h, 0:A_DK, 0:TQ] = q0
        qbd_ref[h, 0:A_DK, TQ:A_LANES] = zq
        qbd_ref[h, A_DK:2 * A_DK, 0:TQ] = zq
        qbd_ref[h, A_DK:2 * A_DK, TQ:A_LANES] = q1

    def a_tile(k_get, v_get, bias_get, first=False):
        rows = v_get(0).shape[1]
        for h in range(A_HEADS):
            bias = bias_get(h)
            sa_ref[h, 0:rows, :] = (jnp.dot(k_get(h), qbd_ref[h], preferred_element_type=F32)
                                    + jnp.concatenate([bias, bias], axis=1))
        for h in range(A_HEADS):
            _online(sa_ref.at[h, 0:rows, :], slice(h * A_LANES, (h + 1) * A_LANES),
                    m_ref, l_ref, acc_ref, v_get(h), first)

    qrel = i * TQ + lax.broadcasted_iota(jnp.int32, (1, TQ), 1)

    def idx_scores(ik_tile):
        d = jnp.dot(ik_tile, iqt_ref[0, 0], preferred_element_type=F32)
        acc = jnp.zeros((ik_tile.shape[0], TQ), F32)
        for h in range(IDX_HEADS):
            acc = acc + iwt_ref[0, h:h + 1, :] * jnp.maximum(d[:, h * TQ:(h + 1) * TQ], 0.0)
        return acc

    def store_digits(st, hi_dst, b1_dst, b0_dst):
        bits = lax.bitcast_convert_type(st, jnp.int32)
        key = bits ^ (jnp.right_shift(bits, 31) & 0x7FFFFFFF)
        hi_dst[...] = lax.bitcast_convert_type(bits & jnp.int32(-65536), F32).astype(BF16)
        b1_dst[...] = (jnp.right_shift(key, 8) & 255).astype(F32).astype(BF16)
        b0_dst[...] = (key & 255).astype(F32).astype(BF16)

    mrow = lax.broadcasted_iota(jnp.int32, (N_META, TQ), 0)
    trow = lax.broadcasted_iota(jnp.int32, (TQ, TQ), 0)

    a_tile(lambda h: kam_ref[0, 0:N_META, h * 2 * A_DK:(h + 1) * 2 * A_DK],
           lambda h: vatm_ref[0, 0, h * A_DV:(h + 1) * A_DV, 0:N_META],
           lambda h: tmall_ref[h, midx], first=True)
    sm = idx_scores(ikm_ref[0, 0:N_META, :])
    stm_ref[...] = sm
    store_digits(sm, him_ref, b1m_ref, b0m_ref)

    assert KEY_TILES_PER_STEP == 4
    n_quads = jnp.right_shift(n_tiles, 2)
    has_pair = (n_tiles & 2) != 0
    has_tail = (n_tiles & 1) != 0
    cat = lambda parts, axis: parts[0] if len(parts) == 1 else jnp.concatenate(parts, axis=axis)

    def walk_tiles(visit):
        def body(q, carry):
            visit([KEY_TILES_PER_STEP * q + t for t in range(KEY_TILES_PER_STEP)])
            return carry

        lax.fori_loop(0, n_quads, body, 0)

        @pl.when(has_pair)
        def _():
            visit([KEY_TILES_PER_STEP * n_quads, KEY_TILES_PER_STEP * n_quads + 1])

        @pl.when(has_tail)
        def _():
            visit([n_tiles - 1])

    def key_rows(js):
        return pl.ds(pl.multiple_of(js[0] * TQ, TQ), len(js) * TQ)

    def bias_rows(js):
        kidx = [jnp.minimum(i - j, 2) for j in js]
        return lambda h: cat([tall_ref[h, k] for k in kidx], 0)

    def ab_tiles(js):
        rows = key_rows(js)
        a_tile(lambda h: ka_ref[0, rows, h * 2 * A_DK:(h + 1) * 2 * A_DK],
               lambda h: cat([vat_ref[0, j, h * A_DV:(h + 1) * A_DV, :] for j in js], 1),
               bias_rows(js))
        for t0 in range(0, len(js), 2):
            jj = js[t0:t0 + 2]
            sc = idx_scores(ik_ref[0, key_rows(jj), :])
            for t, j in enumerate(jj):
                sx = jnp.where(j * TQ + trow <= qrel, sc[t * TQ:(t + 1) * TQ], -jnp.inf)
                st_ref[j] = sx
                store_digits(sx, hi_ref.at[j], b1_ref.at[j], b0_ref.at[j])

    walk_tiles(ab_tiles)

    lp = lamp_ref[...]
    lam = (jnp.exp(jnp.sum(lp[0:1] * lp[1:2], axis=-1, keepdims=True))
           - jnp.exp(jnp.sum(lp[2:3] * lp[3:4], axis=-1, keepdims=True)) + LAM_INIT)
    for h in range(A_HEADS):
        l0 = slice(h * A_LANES, h * A_LANES + TQ)
        l1 = slice(h * A_LANES + TQ, (h + 1) * A_LANES)
        o = acc_ref[:, l0] * (1.0 / l_ref[:, l0]) - acc_ref[:, l1] * (lam / l_ref[:, l1])
        o = o * lax.rsqrt(jnp.mean(o * o, axis=0, keepdims=True) + EPS) * subln_ref[...]
        ot_ref[h * A_DV:(h + 1) * A_DV, :] = o.astype(BF16)

    kf = jnp.minimum(IDX_TOPK_MAX, qrel + (N_META + 1)).astype(F32)
    one, zero = jnp.full((), 1, BF16), jnp.full((), 0, BF16)

    ge = lambda x, t: x >= t
    gt = lambda x, t: x > t
    as_f32 = lambda d: d.astype(F32)

    def top_half_value(w):
        k = w - 32768
        b16 = (k ^ (jnp.right_shift(k, 15) & 0x7FFF)) & 0xFFFF
        return lax.bitcast_convert_type(jnp.left_shift(b16, 16), F32)

    def find_threshold(n):
        def pk_count(meta_ref, tile_ref, pred, t_row):
            tb = jnp.broadcast_to(t_row.astype(BF16), (PACK, TQ))[None]
            accs = [None] * COUNT_CHAINS
            k = 0
            for x in [meta_ref[...]] + [tile_ref[j] for j in range(n)]:
                x3 = x.reshape(x.shape[0] // PACK, PACK, TQ)
                ind = jnp.where(pred(x3, tb), one, zero)
                for r in range(ind.shape[0]):
                    c = k % COUNT_CHAINS
                    accs[c] = ind[r] if accs[c] is None else accs[c] + ind[r]
                    k += 1
            acc = functools.reduce(lambda a, b: a + b, [a for a in accs if a is not None])
            return jnp.sum(acc.astype(F32), axis=0, keepdims=True)

        def pk_keep(meta_ref, tile_ref, sel_meta_ref, sel_tile_ref, t_row):
            tb = jnp.broadcast_to(t_row.astype(BF16), (1, TQ))
            meta_ref[...] = jnp.where(sel_meta_ref[...] == tb, meta_ref[...], -one)
            for j in range(n):
                tile_ref[j] = jnp.where(sel_tile_ref[j] == tb, tile_ref[j], -one)

        def bisect(n_bits, to_value, meta_ref, tile_ref, need):
            def body(b, w):
                cand = w | jnp.left_shift(jnp.int32(1), n_bits - 1 - b)
                return jnp.where(pk_count(meta_ref, tile_ref, ge, to_value(cand)) >= need, cand, w)

            return lax.fori_loop(0, n_bits, body, jnp.zeros((1, TQ), jnp.int32))

        w_hi = bisect(16, top_half_value, him_ref, hi_ref, kf)
        need1 = kf - pk_count(him_ref, hi_ref, gt, top_half_value(w_hi))
        pk_keep(b1m_ref, b1_ref, him_ref, hi_ref, top_half_value(w_hi))
        d1 = bisect(8, as_f32, b1m_ref, b1_ref, need1)
        need0 = need1 - pk_count(b1m_ref, b1_ref, gt, as_f32(d1))
        pk_keep(b0m_ref, b0_ref, b1m_ref, b1_ref, as_f32(d1))
        d0 = bisect(8, as_f32, b0m_ref, b0_ref, need0)
        return jnp.left_shift(w_hi - 32768, 16) | jnp.left_shift(d1, 8) | d0

    assert PACK * st_ref.shape[0] + N_META // PACK <= 256, "bf16 counters must stay exact"
    for n in range(1, st_ref.shape[0] + 1):
        @pl.when(n_tiles == n)
        def _(n=n):
            key_ref[...] = find_threshold(n)

    def count(ind_meta, ind_tile):
        p = ind_meta(stm_ref[...]).reshape(N_META // 8, 8, TQ).sum(axis=0)
        p = lax.fori_loop(0, n_tiles,
                          lambda j, p: p + ind_tile(st_ref[j], j).reshape(TQ // 8, 8, TQ).sum(axis=0), p)
        return jnp.sum(p, axis=0, keepdims=True)

    def count_ge(t):
        ind = lambda x: jnp.where(x >= t, 1.0, 0.0)
        return count(ind, lambda x, j: ind(x))

    def count_gt(t):
        ind = lambda x: jnp.where(x > t, 1.0, 0.0)
        return count(ind, lambda x, j: ind(x))

    hint = _key_to_f32(key_ref[...])
    cnt_ref[0:1, :] = count_gt(hint)
    cnt_ref[1:2, :] = count_ge(hint)
    state = jnp.max(jnp.where((cnt_ref[0:1, :] >= kf) | (cnt_ref[1:2, :] < kf), 2.0,
                              jnp.where(cnt_ref[1:2, :] > kf, 1.0, 0.0)))

    @pl.when(state > 1.5)
    def _():
        def bit_body(b, u):
            cand = u + jnp.left_shift(jnp.int32(1), 31 - b)
            return jnp.where(count_ge(_key_to_f32(cand)) >= kf, cand, u)

        u = lax.fori_loop(0, 32, bit_body, jnp.full((1, TQ), INT_MIN, jnp.int32))
        key_ref[...] = u
        cnt_ref[0:1, :] = count_gt(_key_to_f32(u))
        cnt_ref[1:2, :] = count_ge(_key_to_f32(u))

    thr = _key_to_f32(key_ref[...])

    need = kf - cnt_ref[0:1, :]

    @pl.when(state > 0.5)
    def _():
        def count_ties_before(xc):
            return count(lambda x: jnp.where((x == thr) & (mrow < xc), 1.0, 0.0),
                         lambda x, j: jnp.where((x == thr) & (j * TQ + N_META + trow < xc), 1.0, 0.0))

        def pos_body(b, xs):
            cand = xs | jnp.left_shift(jnp.int32(1), 11 - b)
            return jnp.where(count_ties_before(cand) < need, cand, xs)

        last = lax.fori_loop(0, 12, pos_body, jnp.zeros((1, TQ), jnp.int32))
        sm = stm_ref[...]
        stm_ref[...] = jnp.where((sm == thr) & (mrow > last), -jnp.inf, sm)

        def drop_body(j, carry):
            sx = st_ref[j]
            st_ref[j] = jnp.where((sx == thr) & (j * TQ + N_META + trow > last), -jnp.inf, sx)
            return carry

        lax.fori_loop(0, n_tiles, drop_body, 0)

    def d_tile(st_tile, c_tile, ct_tile, bias_get, first=False):
        sel = jnp.where(st_tile >= thr, 0.0, NEG)
        rows = c_tile.shape[0]
        for h in range(B_HEADS):
            sd_ref[h, 0:rows, :] = (jnp.dot(c_tile, qlt_ref[0, 0, :, h * TQ:(h + 1) * TQ],
                                            preferred_element_type=F32)
                                    + (bias_get(A_HEADS + h) + sel))
        for h in range(B_HEADS):
            _online(sd_ref.at[h, 0:rows, :], slice(h * TQ, (h + 1) * TQ), m2_ref, l2_ref, acc2_ref,
                    ct_tile, first)

    d_tile(stm_ref[...], cm_ref[0, 0:N_META, :], ctm_ref[0, 0, :, 0:N_META], lambda h: tmall_ref[h, midx],
           first=True)

    def d_tiles(js):
        d_tile(cat([st_ref[j] for j in js], 0), c_ref[0, key_rows(js), :],
               cat([ct_ref[0, j] for j in js], 1), bias_rows(js))

    walk_tiles(d_tiles)

    for h in range(B_HEADS):
        lanes = slice(h * TQ, (h + 1) * TQ)
        olat = (acc2_ref[:, lanes] * (1.0 / l2_ref[:, lanes])).astype(BF16)
        ob = jnp.dot(wuvt_ref[h], olat, preferred_element_type=F32)
        r0 = A_HEADS * A_DV + h * B_DV
        ot_ref[r0:r0 + B_DV, :] = ob.astype(BF16)

    y = lax.dot_general(ot_ref[...], wout_ref[...], (((0,), (0,)), ((), ())),
                        preferred_element_type=F32)
    o_ref[0] = h_ref[0] + y


def _attention(q_side, h3d, k_side, meta_side, tall, tmall, lamp, subln, wuvt, wout):
    qat, qlt, iqt, iwt = q_side
    ka, vat, c, ct, ik = k_side
    kam, vatm, cm, ctm, ikm = meta_side
    b, s, d = h3d.shape
    nt = s // TQ
    in_specs = [
        pl.BlockSpec((1, qat.shape[1], TQ), lambda bi, i: (bi, 0, i)),
        pl.BlockSpec((1, 1) + qlt.shape[2:], lambda bi, i: (bi, i, 0, 0)),
        pl.BlockSpec((1, 1) + iqt.shape[2:], lambda bi, i: (bi, i, 0, 0)),
        pl.BlockSpec((1, iwt.shape[1], TQ), lambda bi, i: (bi, 0, i)),
        pl.BlockSpec((1, TQ, d), lambda bi, i: (bi, i, 0)),
        pl.BlockSpec((1,) + ka.shape[1:], lambda bi, i: (bi, 0, 0)),
        pl.BlockSpec((1,) + vat.shape[1:], lambda bi, i: (bi, 0, 0, 0)),
        pl.BlockSpec((1,) + c.shape[1:], lambda bi, i: (bi, 0, 0)),
        pl.BlockSpec((1,) + ct.shape[1:], lambda bi, i: (bi, 0, 0, 0)),
        pl.BlockSpec((1,) + ik.shape[1:], lambda bi, i: (bi, 0, 0)),
    ] + [_const_spec(a.shape) for a in (kam, vatm, cm, ctm, ikm, tall, tmall, lamp, subln, wuvt, wout)]
    scratch = [
        pltpu.VMEM((nt, TQ, TQ), F32),
        pltpu.VMEM((N_META, TQ), F32),
        pltpu.VMEM((nt, TQ, TQ), BF16), pltpu.VMEM((N_META, TQ), BF16),
        pltpu.VMEM((nt, TQ, TQ), BF16), pltpu.VMEM((N_META, TQ), BF16),
        pltpu.VMEM((nt, TQ, TQ), BF16), pltpu.VMEM((N_META, TQ), BF16),
        pltpu.VMEM((1, TQ), jnp.int32),
        pltpu.VMEM((8, TQ), F32),
        pltpu.VMEM((A_HEADS, 2 * A_DK, A_LANES), BF16),
        pltpu.VMEM((A_HEADS, KEY_TILES_PER_STEP * TQ, A_LANES), F32),
        pltpu.VMEM((B_HEADS, KEY_TILES_PER_STEP * TQ, TQ), F32),
        pltpu.VMEM((1, A_HEADS * A_LANES), F32),
        pltpu.VMEM((1, A_HEADS * A_LANES), F32),
        pltpu.VMEM((A_DV, A_HEADS * A_LANES), F32),
        pltpu.VMEM((1, B_HEADS * TQ), F32),
        pltpu.VMEM((1, B_HEADS * TQ), F32),
        pltpu.VMEM((B_DC, B_HEADS * TQ), F32),
        pltpu.VMEM((A_HEADS * A_DV + B_HEADS * B_DV, TQ), BF16),
    ]
    return pl.pallas_call(
        _attn_kernel,
        grid=(b, nt),
        in_specs=in_specs,
        out_specs=pl.BlockSpec((1, TQ, d), lambda bi, i: (bi, i, 0)),
        out_shape=jax.ShapeDtypeStruct((b, s, d), F32),
        scratch_shapes=scratch,
        compiler_params=_params(2),
        name="attention",
    )(qat, qlt, iqt, iwt, h3d, ka, vat, c, ct, ik, kam, vatm, cm, ctm, ikm,
      tall, tmall, lamp, subln, wuvt, wout)


def kernel(x, meta_tokens, rel_bias, ffn1_norm, ffn1_w_gate, ffn1_w_up, ffn1_w_down, mix_norm, w_in, a_q_norm, a_k_norm, a_lambda_q1, a_lambda_k1, a_lambda_q2, a_lambda_k2, a_subln, b_kv_norm, b_w_uk, b_q_norm, b_w_uv, w_out, ffn2_norm, ffn2_w_gate, ffn2_w_up, ffn2_w_down):
    bsz, seq, d = x.shape
    assert d == D_MODEL and seq % PROJ_TM == 0 and (bsz * seq) % FFN_TM == 0
    assert min(IDX_TOPK_MAX, (seq + N_META) // 4) == IDX_TOPK_MAX
    assert ffn1_norm.shape[0] == 1, "single layer"
    lyr = 0
    row = lambda v: v.reshape(1, -1).astype(F32)
    col = lambda v: v.reshape(-1, 1).astype(F32)

    w = w_in[lyr]
    o_qa, o_ka, o_va, o_qb, o_c, o_iq, o_ik, o_iw = np.cumsum(
        [0, 2 * A_HEADS * A_DK, 2 * A_HEADS * A_DK, A_HEADS * A_DV, B_HEADS * B_DQ, B_DC,
         IDX_HEADS * IDX_DIM, IDX_DIM])[:8]
    w_qa, w_ka, w_va, w_qb = w[:, o_qa:o_ka], w[:, o_ka:o_va], w[:, o_va:o_qb], w[:, o_qb:o_c]
    w_c, w_iq, w_ik, w_iw = w[:, o_c:o_iq], w[:, o_iq:o_ik], w[:, o_ik:o_iw], w[:, o_iw:o_iw + IDX_HEADS]
    wt = jnp.concatenate([w_qa, w_va, w_qb, w_c, w_iq, w_iw,
                          jnp.zeros((d, _T_END - _T_IW - IDX_HEADS), w.dtype)], axis=1).T.astype(BF16)
    wr = jnp.concatenate([w_ka, w_c, w_ik, jnp.zeros((d, _R_END - _R_IK - IDX_DIM), w.dtype)],
                         axis=1).astype(BF16)
    wukt = jnp.transpose(b_w_uk[lyr], (0, 2, 1)).astype(BF16)
    wuvt = jnp.transpose(b_w_uv[lyr], (0, 2, 1)).astype(BF16)
    gq = col(jnp.tile(a_q_norm[lyr], 2 * A_HEADS)) * (A_SCALE * LOG2E)
    gk = row(a_k_norm[lyr])
    gql = col(b_q_norm[lyr]) * (B_SCALE * LOG2E)
    gcr, gcc = row(b_kv_norm[lyr]), col(b_kv_norm[lyr])
    subln = col(a_subln[lyr]) * (1.0 - LAM_INIT)
    lamp = jnp.stack([a_lambda_q1[lyr], a_lambda_k1[lyr], a_lambda_q2[lyr], a_lambda_k2[lyr]]).astype(F32)
    proj_consts = (row(mix_norm[lyr]), wt, wr, wukt, gq, gk, gql, gcr, gcc)

    tall, tmall = _bias_tiles(rel_bias.astype(F32))

    ffn1 = (row(ffn1_norm[lyr]), ffn1_w_gate[lyr].astype(BF16), ffn1_w_up[lyr].astype(BF16),
            ffn1_w_down[lyr].astype(BF16))
    h1 = _ffn(x.reshape(bsz * seq, d), *ffn1, tm=FFN_TM).reshape(bsz, seq, d)
    meta_pad = jnp.pad(meta_tokens.astype(x.dtype), ((0, META_PAD - N_META), (0, 0)))
    h1m = _ffn(meta_pad, *ffn1, tm=META_PAD).reshape(1, META_PAD, d)

    qat, ka, vat, qlt, c, ct, iqt, ik, iwt = _inproj(h1, *proj_consts, tm=PROJ_TM, tw=TQ)
    _, kam, vatm, _, cm, ctm, _, ikm, _ = _inproj(h1m, *proj_consts, tm=META_PAD, tw=META_PAD)

    h2 = _attention((qat, qlt, iqt, iwt), h1, (ka, vat, c, ct, ik), (kam, vatm, cm, ctm, ikm),
                    tall, tmall, lamp, subln, wuvt, w_out[lyr].astype(BF16))

    ffn2 = (row(ffn2_norm[lyr]), ffn2_w_gate[lyr].astype(BF16), ffn2_w_up[lyr].astype(BF16),
            ffn2_w_down[lyr].astype(BF16))
    return _ffn(h2.reshape(bsz * seq, d), *ffn2, tm=FFN_TM).reshape(bsz, seq, d)
```

```python
import functools
import math

import jax
import jax.numpy as jnp
import numpy as np
from jax import lax
from jax.experimental import pallas as pl
from jax.experimental.pallas import tpu as pltpu

F32 = jnp.float32
BF16 = jnp.bfloat16

D_MODEL = 1024
N_META = 16
A_HEADS = 4
A_DK = 64
A_DV = 128
B_HEADS = 4
B_DQ = 128
B_DC = 256
B_DV = 128
IDX_HEADS = 8
IDX_DIM = 64
IDX_TOPK_MAX = 256
REL_BUCKETS = 32
REL_MAX_DIST = 128
D_FF = 2816
EPS = 1e-6
LOG2E = math.log2(math.e)
A_SCALE = A_DK ** -0.5
B_SCALE = B_DC ** -0.5
LAM_INIT = 0.8 - 0.6 * math.exp(-0.3 * 0)

NEG = -1e30
INT_MIN = -(2 ** 31)

TQ = 256
META_PAD = 128
FF_CHUNK = D_FF
FFN_TM = 512
PROJ_TM = 512
V7X_VMEM_LIMIT = 56 * 1024 * 1024

KEY_TILES_PER_STEP = 4
A_LANES = 2 * TQ
PACK = 16
COUNT_CHAINS = 4
assert N_META % PACK == 0
_T_QA, _T_VA, _T_QB, _T_C, _T_IQ, _T_IW, _T_END = 0, 512, 1024, 1536, 1792, 2304, 2320
_R_KA, _R_C, _R_IK, _R_END = 0, 512, 768, 896


def _bucket_edges():
    max_exact = REL_BUCKETS // 2
    n = np.arange(0, 4 * REL_MAX_DIST)
    nf = np.maximum(n, 1).astype(np.float64)
    large = max_exact + (np.log(nf / max_exact) / math.log(REL_MAX_DIST / max_exact)
                         * (REL_BUCKETS - max_exact)).astype(np.int64)
    large = np.minimum(large, REL_BUCKETS - 1)
    bucket = np.where(n < max_exact, n, large)
    return [int(n[np.argmax(bucket >= j)]) for j in range(REL_BUCKETS)]


_EDGES = _bucket_edges()
assert _EDGES[-1] <= TQ, "tiles two or more away from the diagonal must sit in the last bucket"


def _const_spec(shape):
    nd = len(shape)
    return pl.BlockSpec(shape, lambda *_: (0,) * nd, pipeline_mode=pl.Buffered(1))


def _params(n_grid):
    return pltpu.CompilerParams(dimension_semantics=("arbitrary",) * n_grid,
                                vmem_limit_bytes=V7X_VMEM_LIMIT)


def _rms_lanes(x, g_row):
    ms = jnp.mean(x * x, axis=-1, keepdims=True)
    return x * lax.rsqrt(ms + EPS) * g_row


def _bias_kernel(rb_ref, tall_ref, tmall_ref):
    h = pl.program_id(0)

    def table(d):
        val = jnp.full(d.shape, rb_ref[0, h] * LOG2E, F32)
        for j in range(1, REL_BUCKETS):
            val = jnp.where(d >= _EDGES[j], rb_ref[j, h] * LOG2E, val)
        return val

    far = rb_ref[REL_BUCKETS - 1, h] * LOG2E
    c = lax.broadcasted_iota(jnp.int32, (TQ, TQ), 0)
    r = lax.broadcasted_iota(jnp.int32, (TQ, TQ), 1)
    d0 = r - c
    tall_ref[0, 0] = jnp.where(d0 >= 0, table(d0), NEG)
    tall_ref[0, 1] = table(d0 + TQ)
    tall_ref[0, 2] = jnp.full((TQ, TQ), far, F32)
    m = lax.broadcasted_iota(jnp.int32, (N_META, TQ), 0)
    r2 = lax.broadcasted_iota(jnp.int32, (N_META, TQ), 1)
    tmall_ref[0, 0] = table(r2 + N_META - m)
    tmall_ref[0, 1] = jnp.full((N_META, TQ), far, F32)


def _bias_tiles(rel_bias):
    nh = rel_bias.shape[1]
    return pl.pallas_call(
        _bias_kernel,
        grid=(nh,),
        in_specs=[pl.BlockSpec(memory_space=pltpu.SMEM)],
        out_specs=[pl.BlockSpec((1, 3, TQ, TQ), lambda h: (h, 0, 0, 0)),
                   pl.BlockSpec((1, 2, N_META, TQ), lambda h: (h, 0, 0, 0))],
        out_shape=[jax.ShapeDtypeStruct((nh, 3, TQ, TQ), F32),
                   jax.ShapeDtypeStruct((nh, 2, N_META, TQ), F32)],
        compiler_params=_params(1),
        name="bias_tiles",
    )(rel_bias)


def _ffn_kernel(x_ref, g_ref, wg_ref, wu_ref, wd_ref, o_ref):
    x = x_ref[...]
    xn = _rms_lanes(x, g_ref[...]).astype(BF16)
    y = None
    for c in range(D_FF // FF_CHUNK):
        sl = slice(c * FF_CHUNK, (c + 1) * FF_CHUNK)
        g = jnp.dot(xn, wg_ref[:, sl], preferred_element_type=F32)
        u = jnp.dot(xn, wu_ref[:, sl], preferred_element_type=F32)
        a = (g * jax.nn.sigmoid(g) * u).astype(BF16)
        part = jnp.dot(a, wd_ref[sl, :], preferred_element_type=F32)
        y = part if y is None else y + part
    o_ref[...] = x + 0.5 * y


def _ffn(h2d, g_row, wg, wu, wd, tm):
    m, d = h2d.shape
    return pl.pallas_call(
        _ffn_kernel,
        grid=(m // tm,),
        in_specs=[pl.BlockSpec((tm, d), lambda i: (i, 0)),
                  _const_spec((1, d)),
                  _const_spec(wg.shape), _const_spec(wu.shape), _const_spec(wd.shape)],
        out_specs=pl.BlockSpec((tm, d), lambda i: (i, 0)),
        out_shape=jax.ShapeDtypeStruct((m, d), F32),
        compiler_params=_params(1),
        name="ffn",
    )(h2d, g_row, wg, wu, wd)


def _inproj_kernel(h_ref, g_ref, wt_ref, wr_ref, wukt_ref, gq_ref, gk_ref, gql_ref, gcr_ref, gcc_ref,
                   qat_ref, ka_ref, vat_ref, qlt_ref, c_ref, ct_ref, iqt_ref, ik_ref, iwt_ref, *, tm, tw):
    xn = _rms_lanes(h_ref[0], g_ref[...]).astype(BF16)
    subtiles = [(t, slice(t * tw, (t + 1) * tw)) for t in range(tm // tw)]

    def proj_t(r0, r1):
        return lax.dot_general(wt_ref[r0:r1, :], xn, (((1,), (1,)), ((), ())), preferred_element_type=F32)

    def proj_r(c0, c1):
        return jnp.dot(xn, wr_ref[:, c0:c1], preferred_element_type=F32)

    ka = proj_r(_R_KA, _R_C)
    for hm in range(2 * A_HEADS):
        cols = slice(hm * A_DK, (hm + 1) * A_DK)
        ka_ref[0, :, cols] = _rms_lanes(ka[:, cols], gk_ref[...]).astype(BF16)

    qa = proj_t(_T_QA, _T_VA).reshape(2 * A_HEADS, A_DK, tm)
    qa = qa * lax.rsqrt(jnp.mean(qa * qa, axis=1, keepdims=True) + EPS)
    qat_ref[0] = (qa.reshape(2 * A_HEADS * A_DK, tm) * gq_ref[...]).astype(BF16)

    qb = proj_t(_T_QB, _T_C).astype(BF16)
    for h in range(B_HEADS):
        ql = jnp.dot(wukt_ref[h], qb[h * B_DQ:(h + 1) * B_DQ], preferred_element_type=F32)
        ql = (ql * lax.rsqrt(jnp.mean(ql * ql, axis=0, keepdims=True) + EPS) * gql_ref[...]).astype(BF16)
        for t, sl in subtiles:
            qlt_ref[0, t, :, h * tw:(h + 1) * tw] = ql[:, sl]

    c_ref[0] = _rms_lanes(proj_r(_R_C, _R_IK), gcr_ref[...]).astype(BF16)

    ct = proj_t(_T_C, _T_IQ)
    ct = (ct * lax.rsqrt(jnp.mean(ct * ct, axis=0, keepdims=True) + EPS) * gcc_ref[...]).astype(BF16)
    for t, sl in subtiles:
        ct_ref[0, t] = ct[:, sl]

    va = proj_t(_T_VA, _T_QB).astype(BF16)
    for t, sl in subtiles:
        vat_ref[0, t] = va[:, sl]

    iq = proj_t(_T_IQ, _T_END)
    for h in range(IDX_HEADS):
        iqh = iq[h * IDX_DIM:(h + 1) * IDX_DIM].astype(BF16)
        for t, sl in subtiles:
            iqt_ref[0, t, :, h * tw:(h + 1) * tw] = iqh[:, sl]
    iw0 = _T_IW - _T_IQ
    iwt_ref[0] = iq[iw0:iw0 + IDX_HEADS] * (IDX_HEADS ** -0.5 * IDX_DIM ** -0.5)

    ik_ref[0] = proj_r(_R_IK, _R_END)[:, 0:IDX_DIM].astype(BF16)


def _inproj(h3d, g_row, wt, wr, wukt, gq, gk, gql, gcr, gcc, tm, tw):
    b, s, d = h3d.shape
    nt = s // tw
    kern = functools.partial(_inproj_kernel, tm=tm, tw=tw)
    out_shape = [
        jax.ShapeDtypeStruct((b, 2 * A_HEADS * A_DK, s), BF16),
        jax.ShapeDtypeStruct((b, s, 2 * A_HEADS * A_DK), BF16),
        jax.ShapeDtypeStruct((b, nt, A_HEADS * A_DV, tw), BF16),
        jax.ShapeDtypeStruct((b, nt, B_DC, B_HEADS * tw), BF16),
        jax.ShapeDtypeStruct((b, s, B_DC), BF16),
        jax.ShapeDtypeStruct((b, nt, B_DC, tw), BF16),
        jax.ShapeDtypeStruct((b, nt, IDX_DIM, IDX_HEADS * tw), BF16),
        jax.ShapeDtypeStruct((b, s, IDX_DIM), BF16),
        jax.ShapeDtypeStruct((b, IDX_HEADS, s), F32),
    ]
    k = tm // tw
    out_specs = [
        pl.BlockSpec((1, 2 * A_HEADS * A_DK, tm), lambda bi, i: (bi, 0, i)),
        pl.BlockSpec((1, tm, 2 * A_HEADS * A_DK), lambda bi, i: (bi, i, 0)),
        pl.BlockSpec((1, k, A_HEADS * A_DV, tw), lambda bi, i: (bi, i, 0, 0)),
        pl.BlockSpec((1, k, B_DC, B_HEADS * tw), lambda bi, i: (bi, i, 0, 0)),
        pl.BlockSpec((1, tm, B_DC), lambda bi, i: (bi, i, 0)),
        pl.BlockSpec((1, k, B_DC, tw), lambda bi, i: (bi, i, 0, 0)),
        pl.BlockSpec((1, k, IDX_DIM, IDX_HEADS * tw), lambda bi, i: (bi, i, 0, 0)),
        pl.BlockSpec((1, tm, IDX_DIM), lambda bi, i: (bi, i, 0)),
        pl.BlockSpec((1, IDX_HEADS, tm), lambda bi, i: (bi, 0, i)),
    ]
    consts = (g_row, wt, wr, wukt, gq, gk, gql, gcr, gcc)
    return pl.pallas_call(
        kern,
        grid=(b, s // tm),
        in_specs=[pl.BlockSpec((1, tm, d), lambda bi, i: (bi, i, 0))] + [_const_spec(a.shape) for a in consts],
        out_specs=out_specs,
        out_shape=out_shape,
        compiler_params=_params(2),
        name="inproj",
    )(h3d, *consts)


def _key_to_f32(u):
    bits = u ^ (jnp.right_shift(u, 31) & 0x7FFFFFFF)
    return lax.bitcast_convert_type(bits, F32)


def _online(s_ref, lanes, m_ref, l_ref, acc_ref, v_lhs, first=False):
    if first:
        m_new = jnp.max(s_ref[...], axis=0, keepdims=True)
        m_ref[:, lanes] = m_new
        p = jnp.exp2(s_ref[...] - m_new)
        l_ref[:, lanes] = jnp.sum(p, axis=0, keepdims=True)
        acc_ref[:, lanes] = jnp.dot(v_lhs, p.astype(BF16), preferred_element_type=F32)
        return
    m_old = m_ref[:, lanes]
    m_new = jnp.maximum(m_old, jnp.max(s_ref[...], axis=0, keepdims=True))
    m_ref[:, lanes] = m_new
    alpha = jnp.exp2(m_old - m_new)
    p = jnp.exp2(s_ref[...] - m_new)
    l_ref[:, lanes] = alpha * l_ref[:, lanes] + jnp.sum(p, axis=0, keepdims=True)
    acc_ref[:, lanes] = alpha * acc_ref[:, lanes] + jnp.dot(v_lhs, p.astype(BF16),
                                                           preferred_element_type=F32)


def _attn_kernel(qat_ref, qlt_ref, iqt_ref, iwt_ref, h_ref,
                 ka_ref, vat_ref, c_ref, ct_ref, ik_ref,
                 kam_ref, vatm_ref, cm_ref, ctm_ref, ikm_ref,
                 tall_ref, tmall_ref, lamp_ref, subln_ref, wuvt_ref, wout_ref,
                 o_ref,
                 st_ref, stm_ref, hi_ref, him_ref, b1_ref, b1m_ref, b0_ref, b0m_ref,
                 key_ref, cnt_ref, qbd_ref, sa_ref, sd_ref, m_ref, l_ref, acc_ref, m2_ref, l2_ref, acc2_ref, ot_ref):
    i = pl.program_id(1)
    n_tiles = i + 1
    midx = jnp.minimum(i, 1)

    zq = jnp.zeros((A_DK, TQ), BF16)
    for h in range(A_HEADS):
        q0 = qat_ref[0, (2 * h) * A_DK:(2 * h + 1) * A_DK, :]
        q1 = qat_ref[0, (2 * h + 1) * A_DK:(2 * h + 2) * A_DK, :]
        qbd_ref[h, 0:A_DK, 0:TQ] = q0
        qbd_ref[h, 0:A_DK, TQ:A_LANES] = zq
        qbd_ref[h, A_DK:2 * A_DK, 0:TQ] = zq
        qbd_ref[h, A_DK:2 * A_DK, TQ:A_LANES] = q1

    def a_tile(k_get, v_get, bias_get, first=False):
        rows = v_get(0).shape[1]
        for h in range(A_HEADS):
            bias = bias_get(h)
            sa_ref[h, 0:rows, :] = (jnp.dot(k_get(h), qbd_ref[h], preferred_element_type=F32)
                                    + jnp.concatenate([bias, bias], axis=1))
        for h in range(A_HEADS):
            _online(sa_ref.at[h, 0:rows, :], slice(h * A_LANES, (h + 1) * A_LANES),
                    m_ref, l_ref, acc_ref, v_get(h), first)

    qrel = i * TQ + lax.broadcasted_iota(jnp.int32, (1, TQ), 1)

    def idx_scores(ik_tile):
        d = jnp.dot(ik_tile, iqt_ref[0, 0], preferred_element_type=F32)
        acc = jnp.zeros((ik_tile.shape[0], TQ), F32)
        for h in range(IDX_HEADS):
            acc = acc + iwt_ref[0, h:h + 1, :] * jnp.maximum(d[:, h * TQ:(h + 1) * TQ], 0.0)
        return acc

    def store_digits(st, hi_dst, b1_dst, b0_dst):
        bits = lax.bitcast_convert_type(st, jnp.int32)
        key = bits ^ (jnp.right_shift(bits, 31) & 0x7FFFFFFF)
        hi_dst[...] = lax.bitcast_convert_type(bits & jnp.int32(-65536), F32).astype(BF16)
        b1_dst[...] = (jnp.right_shift(key, 8) & 255).astype(F32).astype(BF16)
        b0_dst[...] = (key & 255).astype(F32).astype(BF16)

    mrow = lax.broadcasted_iota(jnp.int32, (N_META, TQ), 0)
    trow = lax.broadcasted_iota(jnp.int32, (TQ, TQ), 0)

    a_tile(lambda h: kam_ref[0, 0:N_META, h * 2 * A_DK:(h + 1) * 2 * A_DK],
           lambda h: vatm_ref[0, 0, h * A_DV:(h + 1) * A_DV, 0:N_META],
           lambda h: tmall_ref[h, midx], first=True)
    sm = idx_scores(ikm_ref[0, 0:N_META, :])
    stm_ref[...] = sm
    store_digits(sm, him_ref, b1m_ref, b0m_ref)

    assert KEY_TILES_PER_STEP == 4
    n_quads = jnp.right_shift(n_tiles, 2)
    has_pair = (n_tiles & 2) != 0
    has_tail = (n_tiles & 1) != 0
    cat = lambda parts, axis: parts[0] if len(parts) == 1 else jnp.concatenate(parts, axis=axis)

    def walk_tiles(visit):
        def body(q, carry):
            visit([KEY_TILES_PER_STEP * q + t for t in range(KEY_TILES_PER_STEP)])
            return carry

        lax.fori_loop(0, n_quads, body, 0)

        @pl.when(has_pair)
        def _():
            visit([KEY_TILES_PER_STEP * n_quads, KEY_TILES_PER_STEP * n_quads + 1])

        @pl.when(has_tail)
        def _():
            visit([n_tiles - 1])

    def key_rows(js):
        return pl.ds(pl.multiple_of(js[0] * TQ, TQ), len(js) * TQ)

    def bias_rows(js):
        kidx = [jnp.minimum(i - j, 2) for j in js]
        return lambda h: cat([tall_ref[h, k] for k in kidx], 0)

    def ab_tiles(js):
        rows = key_rows(js)
        a_tile(lambda h: ka_ref[0, rows, h * 2 * A_DK:(h + 1) * 2 * A_DK],
               lambda h: cat([vat_ref[0, j, h * A_DV:(h + 1) * A_DV, :] for j in js], 1),
               bias_rows(js))
        for t0 in range(0, len(js), 2):
            jj = js[t0:t0 + 2]
            sc = idx_scores(ik_ref[0, key_rows(jj), :])
            for t, j in enumerate(jj):
                sx = jnp.where(j * TQ + trow <= qrel, sc[t * TQ:(t + 1) * TQ], -jnp.inf)
                st_ref[j] = sx
                store_digits(sx, hi_ref.at[j], b1_ref.at[j], b0_ref.at[j])

    walk_tiles(ab_tiles)

    lp = lamp_ref[...]
    lam = (jnp.exp(jnp.sum(lp[0:1] * lp[1:2], axis=-1, keepdims=True))
           - jnp.exp(jnp.sum(lp[2:3] * lp[3:4], axis=-1, keepdims=True)) + LAM_INIT)
    for h in range(A_HEADS):
        l0 = slice(h * A_LANES, h * A_LANES + TQ)
        l1 = slice(h * A_LANES + TQ, (h + 1) * A_LANES)
        o = acc_ref[:, l0] * (1.0 / l_ref[:, l0]) - acc_ref[:, l1] * (lam / l_ref[:, l1])
        o = o * lax.rsqrt(jnp.mean(o * o, axis=0, keepdims=True) + EPS) * subln_ref[...]
        ot_ref[h * A_DV:(h + 1) * A_DV, :] = o.astype(BF16)

    kf = jnp.minimum(IDX_TOPK_MAX, qrel + (N_META + 1)).astype(F32)
    one, zero = jnp.full((), 1, BF16), jnp.full((), 0, BF16)

    ge = lambda x, t: x >= t
    gt = lambda x, t: x > t
    as_f32 = lambda d: d.astype(F32)

    def top_half_value(w):
        k = w - 32768
        b16 = (k ^ (jnp.right_shift(k, 15) & 0x7FFF)) & 0xFFFF
        return lax.bitcast_convert_type(jnp.left_shift(b16, 16), F32)

    def find_threshold(n):
        def pk_count(meta_ref, tile_ref, pred, t_row):
            tb = jnp.broadcast_to(t_row.astype(BF16), (PACK, TQ))[None]
            accs = [None] * COUNT_CHAINS
            k = 0
            for x in [meta_ref[...]] + [tile_ref[j] for j in range(n)]:
                x3 = x.reshape(x.shape[0] // PACK, PACK, TQ)
                ind = jnp.where(pred(x3, tb), one, zero)
                for r in range(ind.shape[0]):
                    c = k % COUNT_CHAINS
                    accs[c] = ind[r] if accs[c] is None else accs[c] + ind[r]
                    k += 1
            acc = functools.reduce(lambda a, b: a + b, [a for a in accs if a is not None])
            return jnp.sum(acc.astype(F32), axis=0, keepdims=True)

        def pk_keep(meta_ref, tile_ref, sel_meta_ref, sel_tile_ref, t_row):
            tb = jnp.broadcast_to(t_row.astype(BF16), (1, TQ))
            meta_ref[...] = jnp.where(sel_meta_ref[...] == tb, meta_ref[...], -one)
            for j in range(n):
                tile_ref[j] = jnp.where(sel_tile_ref[j] == tb, tile_ref[j], -one)

        def bisect(n_bits, to_value, meta_ref, tile_ref, need):
            def body(b, w):
                cand = w | jnp.left_shift(jnp.int32(1), n_bits - 1 - b)
                return jnp.where(pk_count(meta_ref, tile_ref, ge, to_value(cand)) >= need, cand, w)

            return lax.fori_loop(0, n_bits, body, jnp.zeros((1, TQ), jnp.int32))

        w_hi = bisect(16, top_half_value, him_ref, hi_ref, kf)
        need1 = kf - pk_count(him_ref, hi_ref, gt, top_half_value(w_hi))
        pk_keep(b1m_ref, b1_ref, him_ref, hi_ref, top_half_value(w_hi))
        d1 = bisect(8, as_f32, b1m_ref, b1_ref, need1)
        need0 = need1 - pk_count(b1m_ref, b1_ref, gt, as_f32(d1))
        pk_keep(b0m_ref, b0_ref, b1m_ref, b1_ref, as_f32(d1))
        d0 = bisect(8, as_f32, b0m_ref, b0_ref, need0)
        return jnp.left_shift(w_hi - 32768, 16) | jnp.left_shift(d1, 8) | d0

    assert PACK * st_ref.shape[0] + N_META // PACK <= 256, "bf16 counters must stay exact"
    for n in range(1, st_ref.shape[0] + 1):
        @pl.when(n_tiles == n)
        def _(n=n):
            key_ref[...] = find_threshold(n)

    def count(ind_meta, ind_tile):
        p = ind_meta(stm_ref[...]).reshape(N_META // 8, 8, TQ).sum(axis=0)
        p = lax.fori_loop(0, n_tiles,
                          lambda j, p: p + ind_tile(st_ref[j], j).reshape(TQ // 8, 8, TQ).sum(axis=0), p)
        return jnp.sum(p, axis=0, keepdims=True)

    def count_ge(t):
        ind = lambda x: jnp.where(x >= t, 1.0, 0.0)
        return count(ind, lambda x, j: ind(x))

    def count_gt(t):
        ind = lambda x: jnp.where(x > t, 1.0, 0.0)
        return count(ind, lambda x, j: ind(x))

    hint = _key_to_f32(key_ref[...])
    cnt_ref[0:1, :] = count_gt(hint)
    cnt_ref[1:2, :] = count_ge(hint)
    state = jnp.max(jnp.where((cnt_ref[0:1, :] >= kf) | (cnt_ref[1:2, :] < kf), 2.0,
                              jnp.where(cnt_ref[1:2, :] > kf, 1.0, 0.0)))

    @pl.when(state > 1.5)
    def _():
        def bit_body(b, u):
            cand = u + jnp.left_shift(jnp.int32(1), 31 - b)
            return jnp.where(count_ge(_key_to_f32(cand)) >= kf, cand, u)

        u = lax.fori_loop(0, 32, bit_body, jnp.full((1, TQ), INT_MIN, jnp.int32))
        key_ref[...] = u
        cnt_ref[0:1, :] = count_gt(_key_to_f32(u))
        cnt_ref[1:2, :] = count_ge(_key_to_f32(u))

    thr = _key_to_f32(key_ref[...])

    need = kf - cnt_ref[0:1, :]

    @pl.when(state > 0.5)
    def _():
        def count_ties_before(xc):
            return count(lambda x: jnp.where((x == thr) & (mrow < xc), 1.0, 0.0),
                         lambda x, j: jnp.where((x == thr) & (j * TQ + N_META + trow < xc), 1.0, 0.0))

        def pos_body(b, xs):
            cand = xs | jnp.left_shift(jnp.int32(1), 11 - b)
            return jnp.where(count_ties_before(cand) < need, cand, xs)

        last = lax.fori_loop(0, 12, pos_body, jnp.zeros((1, TQ), jnp.int32))
        sm = stm_ref[...]
        stm_ref[...] = jnp.where((sm == thr) & (mrow > last), -jnp.inf, sm)

        def drop_body(j, carry):
            sx = st_ref[j]
            st_ref[j] = jnp.where((sx == thr) & (j * TQ + N_META + trow > last), -jnp.inf, sx)
            return carry

        lax.fori_loop(0, n_tiles, drop_body, 0)

    def d_tile(st_tile, c_tile, ct_tile, bias_get, first=False):
        sel = jnp.where(st_tile >= thr, 0.0, NEG)
        rows = c_tile.shape[0]
        for h in range(B_HEADS):
            sd_ref[h, 0:rows, :] = (jnp.dot(c_tile, qlt_ref[0, 0, :, h * TQ:(h + 1) * TQ],
                                            preferred_element_type=F32)
                                    + (bias_get(A_HEADS + h) + sel))
        for h in range(B_HEADS):
            _online(sd_ref.at[h, 0:rows, :], slice(h * TQ, (h + 1) * TQ), m2_ref, l2_ref, acc2_ref,
                    ct_tile, first)

    d_tile(stm_ref[...], cm_ref[0, 0:N_META, :], ctm_ref[0, 0, :, 0:N_META], lambda h: tmall_ref[h, midx],
           first=True)

    def d_tiles(js):
        d_tile(cat([st_ref[j] for j in js], 0), c_ref[0, key_rows(js), :],
               cat([ct_ref[0, j] for j in js], 1), bias_rows(js))

    walk_tiles(d_tiles)

    for h in range(B_HEADS):
        lanes = slice(h * TQ, (h + 1) * TQ)
        olat = (acc2_ref[:, lanes] * (1.0 / l2_ref[:, lanes])).astype(BF16)
        ob = jnp.dot(wuvt_ref[h], olat, preferred_element_type=F32)
        r0 = A_HEADS * A_DV + h * B_DV
        ot_ref[r0:r0 + B_DV, :] = ob.astype(BF16)

    y = lax.dot_general(ot_ref[...], wout_ref[...], (((0,), (0,)), ((), ())),
                        preferred_element_type=F32)
    o_ref[0] = h_ref[0] + y


def _attention(q_side, h3d, k_side, meta_side, tall, tmall, lamp, subln, wuvt, wout):
    qat, qlt, iqt, iwt = q_side
    ka, vat, c, ct, ik = k_side
    kam, vatm, cm, ctm, ikm = meta_side
    b, s, d = h3d.shape
    nt = s // TQ
    in_specs = [
        pl.BlockSpec((1, qat.shape[1], TQ), lambda bi, i: (bi, 0, i)),
        pl.BlockSpec((1, 1) + qlt.shape[2:], lambda bi, i: (bi, i, 0, 0)),
        pl.BlockSpec((1, 1) + iqt.shape[2:], lambda bi, i: (bi, i, 0, 0)),
        pl.BlockSpec((1, iwt.shape[1], TQ), lambda bi, i: (bi, 0, i)),
        pl.BlockSpec((1, TQ, d), lambda bi, i: (bi, i, 0)),
        pl.BlockSpec((1,) + ka.shape[1:], lambda bi, i: (bi, 0, 0)),
        pl.BlockSpec((1,) + vat.shape[1:], lambda bi, i: (bi, 0, 0, 0)),
        pl.BlockSpec((1,) + c.shape[1:], lambda bi, i: (bi, 0, 0)),
        pl.BlockSpec((1,) + ct.shape[1:], lambda bi, i: (bi, 0, 0, 0)),
        pl.BlockSpec((1,) + ik.shape[1:], lambda bi, i: (bi, 0, 0)),
    ] + [_const_spec(a.shape) for a in (kam, vatm, cm, ctm, ikm, tall, tmall, lamp, subln, wuvt, wout)]
    scratch = [
        pltpu.VMEM((nt, TQ, TQ), F32),
        pltpu.VMEM((N_META, TQ), F32),
        pltpu.VMEM((nt, TQ, TQ), BF16), pltpu.VMEM((N_META, TQ), BF16),
        pltpu.VMEM((nt, TQ, TQ), BF16), pltpu.VMEM((N_META, TQ), BF16),
        pltpu.VMEM((nt, TQ, TQ), BF16), pltpu.VMEM((N_META, TQ), BF16),
        pltpu.VMEM((1, TQ), jnp.int32),
        pltpu.VMEM((8, TQ), F32),
        pltpu.VMEM((A_HEADS, 2 * A_DK, A_LANES), BF16),
        pltpu.VMEM((A_HEADS, KEY_TILES_PER_STEP * TQ, A_LANES), F32),
        pltpu.VMEM((B_HEADS, KEY_TILES_PER_STEP * TQ, TQ), F32),
        pltpu.VMEM((1, A_HEADS * A_LANES), F32),
        pltpu.VMEM((1, A_HEADS * A_LANES), F32),
        pltpu.VMEM((A_DV, A_HEADS * A_LANES), F32),
        pltpu.VMEM((1, B_HEADS * TQ), F32),
        pltpu.VMEM((1, B_HEADS * TQ), F32),
        pltpu.VMEM((B_DC, B_HEADS * TQ), F32),
        pltpu.VMEM((A_HEADS * A_DV + B_HEADS * B_DV, TQ), BF16),
    ]
    return pl.pallas_call(
        _attn_kernel,
        grid=(b, nt),
        in_specs=in_specs,
        out_specs=pl.BlockSpec((1, TQ, d), lambda bi, i: (bi, i, 0)),
        out_shape=jax.ShapeDtypeStruct((b, s, d), F32),
        scratch_shapes=scratch,
        compiler_params=_params(2),
        name="attention",
    )(qat, qlt, iqt, iwt, h3d, ka, vat, c, ct, ik, kam, vatm, cm, ctm, ikm,
      tall, tmall, lamp, subln, wuvt, wout)


def kernel(x, meta_tokens, rel_bias, ffn1_norm, ffn1_w_gate, ffn1_w_up, ffn1_w_down, mix_norm, w_in, a_q_norm, a_k_norm, a_lambda_q1, a_lambda_k1, a_lambda_q2, a_lambda_k2, a_subln, b_kv_norm, b_w_uk, b_q_norm, b_w_uv, w_out, ffn2_norm, ffn2_w_gate, ffn2_w_up, ffn2_w_down):
    bsz, seq, d = x.shape
    assert d == D_MODEL and seq % PROJ_TM == 0 and (bsz * seq) % FFN_TM == 0
    assert min(IDX_TOPK_MAX, (seq + N_META) // 4) == IDX_TOPK_MAX
    assert ffn1_norm.shape[0] == 1, "single layer"
    lyr = 0
    row = lambda v: v.reshape(1, -1).astype(F32)
    col = lambda v: v.reshape(-1, 1).astype(F32)

    w = w_in[lyr]
    o_qa, o_ka, o_va, o_qb, o_c, o_iq, o_ik, o_iw = np.cumsum(
        [0, 2 * A_HEADS * A_DK, 2 * A_HEADS * A_DK, A_HEADS * A_DV, B_HEADS * B_DQ, B_DC,
         IDX_HEADS * IDX_DIM, IDX_DIM])[:8]
    w_qa, w_ka, w_va, w_qb = w[:, o_qa:o_ka], w[:, o_ka:o_va], w[:, o_va:o_qb], w[:, o_qb:o_c]
    w_c, w_iq, w_ik, w_iw = w[:, o_c:o_iq], w[:, o_iq:o_ik], w[:, o_ik:o_iw], w[:, o_iw:o_iw + IDX_HEADS]
    wt = jnp.concatenate([w_qa, w_va, w_qb, w_c, w_iq, w_iw,
                          jnp.zeros((d, _T_END - _T_IW - IDX_HEADS), w.dtype)], axis=1).T.astype(BF16)
    wr = jnp.concatenate([w_ka, w_c, w_ik, jnp.zeros((d, _R_END - _R_IK - IDX_DIM), w.dtype)],
                         axis=1).astype(BF16)
    wukt = jnp.transpose(b_w_uk[lyr], (0, 2, 1)).astype(BF16)
    wuvt = jnp.transpose(b_w_uv[lyr], (0, 2, 1)).astype(BF16)
    gq = col(jnp.tile(a_q_norm[lyr], 2 * A_HEADS)) * (A_SCALE * LOG2E)
    gk = row(a_k_norm[lyr])
    gql = col(b_q_norm[lyr]) * (B_SCALE * LOG2E)
    gcr, gcc = row(b_kv_norm[lyr]), col(b_kv_norm[lyr])
    subln = col(a_subln[lyr]) * (1.0 - LAM_INIT)
    lamp = jnp.stack([a_lambda_q1[lyr], a_lambda_k1[lyr], a_lambda_q2[lyr], a_lambda_k2[lyr]]).astype(F32)
    proj_consts = (row(mix_norm[lyr]), wt, wr, wukt, gq, gk, gql, gcr, gcc)

    tall, tmall = _bias_tiles(rel_bias.astype(F32))

    ffn1 = (row(ffn1_norm[lyr]), ffn1_w_gate[lyr].astype(BF16), ffn1_w_up[lyr].astype(BF16),
            ffn1_w_down[lyr].astype(BF16))
    h1 = _ffn(x.reshape(bsz * seq, d), *ffn1, tm=FFN_TM).reshape(bsz, seq, d)
    meta_pad = jnp.pad(meta_tokens.astype(x.dtype), ((0, META_PAD - N_META), (0, 0)))
    h1m = _ffn(meta_pad, *ffn1, tm=META_PAD).reshape(1, META_PAD, d)

    qat, ka, vat, qlt, c, ct, iqt, ik, iwt = _inproj(h1, *proj_consts, tm=PROJ_TM, tw=TQ)
    _, kam, vatm, _, cm, ctm, _, ikm, _ = _inproj(h1m, *proj_consts, tm=META_PAD, tw=META_PAD)

    h2 = _attention((qat, qlt, iqt, iwt), h1, (ka, vat, c, ct, ik), (kam, vatm, cm, ctm, ikm),
                    tall, tmall, lamp, subln, wuvt, w_out[lyr].astype(BF16))

    ffn2 = (row(ffn2_norm[lyr]), ffn2_w_gate[lyr].astype(BF16), ffn2_w_up[lyr].astype(BF16),
            ffn2_w_down[lyr].astype(BF16))
    return _ffn(h2.reshape(bsz * seq, d), *ffn2, tm=FFN_TM).reshape(bsz, seq, d)
```

```python
import functools
import math

import jax
import jax.numpy as jnp
import numpy as np
from jax import lax
from jax.experimental import pallas as pl
from jax.experimental.pallas import tpu as pltpu

F32 = jnp.float32
BF16 = jnp.bfloat16

D_MODEL = 1024
N_META = 16
A_HEADS = 4
A_DK = 64
A_DV = 128
B_HEADS = 4
B_DQ = 128
B_DC = 256
B_DV = 128
IDX_HEADS = 8
IDX_DIM = 64
IDX_TOPK_MAX = 256
REL_BUCKETS = 32
REL_MAX_DIST = 128
D_FF = 2816
EPS = 1e-6
LOG2E = math.log2(math.e)
A_SCALE = A_DK ** -0.5
B_SCALE = B_DC ** -0.5
LAM_INIT = 0.8 - 0.6 * math.exp(-0.3 * 0)

NEG = -1e30
INT_MIN = -(2 ** 31)

TQ = 256
META_PAD = 128
FF_CHUNK = D_FF
FFN_TM = 512
PROJ_TM = 512
V7X_VMEM_LIMIT = 56 * 1024 * 1024

KEY_TILES_PER_STEP = 4
A_LANES = 2 * TQ
PACK = 16
COUNT_CHAINS = 4
assert N_META % PACK == 0
_T_QA, _T_VA, _T_QB, _T_C, _T_IQ, _T_IW, _T_END = 0, 512, 1024, 1536, 1792, 2304, 2320
_R_KA, _R_C, _R_IK, _R_END = 0, 512, 768, 896


def _bucket_edges():
    max_exact = REL_BUCKETS // 2
    n = np.arange(0, 4 * REL_MAX_DIST)
    nf = np.maximum(n, 1).astype(np.float64)
    large = max_exact + (np.log(nf / max_exact) / math.log(REL_MAX_DIST / max_exact)
                         * (REL_BUCKETS - max_exact)).astype(np.int64)
    large = np.minimum(large, REL_BUCKETS - 1)
    bucket = np.where(n < max_exact, n, large)
    return [int(n[np.argmax(bucket >= j)]) for j in range(REL_BUCKETS)]


_EDGES = _bucket_edges()
assert _EDGES[-1] <= TQ, "tiles two or more away from the diagonal must sit in the last bucket"


def _const_spec(shape):
    nd = len(shape)
    return pl.BlockSpec(shape, lambda *_: (0,) * nd, pipeline_mode=pl.Buffered(1))


def _params(n_grid):
    return pltpu.CompilerParams(dimension_semantics=("arbitrary",) * n_grid,
                                vmem_limit_bytes=V7X_VMEM_LIMIT)


def _rms_lanes(x, g_row):
    ms = jnp.mean(x * x, axis=-1, keepdims=True)
    return x * lax.rsqrt(ms + EPS) * g_row


def _bias_kernel(rb_ref, tall_ref, tmall_ref):
    h = pl.program_id(0)

    def table(d):
        val = jnp.full(d.shape, rb_ref[0, h] * LOG2E, F32)
        for j in range(1, REL_BUCKETS):
            val = jnp.where(d >= _EDGES[j], rb_ref[j, h] * LOG2E, val)
        return val

    far = rb_ref[REL_BUCKETS - 1, h] * LOG2E
    c = lax.broadcasted_iota(jnp.int32, (TQ, TQ), 0)
    r = lax.broadcasted_iota(jnp.int32, (TQ, TQ), 1)
    d0 = r - c
    tall_ref[0, 0] = jnp.where(d0 >= 0, table(d0), NEG)
    tall_ref[0, 1] = table(d0 + TQ)
    tall_ref[0, 2] = jnp.full((TQ, TQ), far, F32)
    m = lax.broadcasted_iota(jnp.int32, (N_META, TQ), 0)
    r2 = lax.broadcasted_iota(jnp.int32, (N_META, TQ), 1)
    tmall_ref[0, 0] = table(r2 + N_META - m)
    tmall_ref[0, 1] = jnp.full((N_META, TQ), far, F32)


def _bias_tiles(rel_bias):
    nh = rel_bias.shape[1]
    return pl.pallas_call(
        _bias_kernel,
        grid=(nh,),
        in_specs=[pl.BlockSpec(memory_space=pltpu.SMEM)],
        out_specs=[pl.BlockSpec((1, 3, TQ, TQ), lambda h: (h, 0, 0, 0)),
                   pl.BlockSpec((1, 2, N_META, TQ), lambda h: (h, 0, 0, 0))],
        out_shape=[jax.ShapeDtypeStruct((nh, 3, TQ, TQ), F32),
                   jax.ShapeDtypeStruct((nh, 2, N_META, TQ), F32)],
        compiler_params=_params(1),
        name="bias_tiles",
    )(rel_bias)


def _ffn_kernel(x_ref, g_ref, wg_ref, wu_ref, wd_ref, o_ref):
    x = x_ref[...]
    xn = _rms_lanes(x, g_ref[...]).astype(BF16)
    y = None
    for c in range(D_FF // FF_CHUNK):
        sl = slice(c * FF_CHUNK, (c + 1) * FF_CHUNK)
        g = jnp.dot(xn, wg_ref[:, sl], preferred_element_type=F32)
        u = jnp.dot(xn, wu_ref[:, sl], preferred_element_type=F32)
        a = (g * jax.nn.sigmoid(g) * u).astype(BF16)
        part = jnp.dot(a, wd_ref[sl, :], preferred_element_type=F32)
        y = part if y is None else y + part
    o_ref[...] = x + 0.5 * y


def _ffn(h2d, g_row, wg, wu, wd, tm):
    m, d = h2d.shape
    return pl.pallas_call(
        _ffn_kernel,
        grid=(m // tm,),
        in_specs=[pl.BlockSpec((tm, d), lambda i: (i, 0)),
                  _const_spec((1, d)),
                  _const_spec(wg.shape), _const_spec(wu.shape), _const_spec(wd.shape)],
        out_specs=pl.BlockSpec((tm, d), lambda i: (i, 0)),
        out_shape=jax.ShapeDtypeStruct((m, d), F32),
        compiler_params=_params(1),
        name="ffn",
    )(h2d, g_row, wg, wu, wd)


def _inproj_kernel(h_ref, g_ref, wt_ref, wr_ref, wukt_ref, gq_ref, gk_ref, gql_ref, gcr_ref, gcc_ref,
                   qat_ref, ka_ref, vat_ref, qlt_ref, c_ref, ct_ref, iqt_ref, ik_ref, iwt_ref, *, tm, tw):
    xn = _rms_lanes(h_ref[0], g_ref[...]).astype(BF16)
    subtiles = [(t, slice(t * tw, (t + 1) * tw)) for t in range(tm // tw)]

    def proj_t(r0, r1):
        return lax.dot_general(wt_ref[r0:r1, :], xn, (((1,), (1,)), ((), ())), preferred_element_type=F32)

    def proj_r(c0, c1):
        return jnp.dot(xn, wr_ref[:, c0:c1], preferred_element_type=F32)

    ka = proj_r(_R_KA, _R_C)
    for hm in range(2 * A_HEADS):
        cols = slice(hm * A_DK, (hm + 1) * A_DK)
        ka_ref[0, :, cols] = _rms_lanes(ka[:, cols], gk_ref[...]).astype(BF16)

    qa = proj_t(_T_QA, _T_VA).reshape(2 * A_HEADS, A_DK, tm)
    qa = qa * lax.rsqrt(jnp.mean(qa * qa, axis=1, keepdims=True) + EPS)
    qat_ref[0] = (qa.reshape(2 * A_HEADS * A_DK, tm) * gq_ref[...]).astype(BF16)

    qb = proj_t(_T_QB, _T_C).astype(BF16)
    for h in range(B_HEADS):
        ql = jnp.dot(wukt_ref[h], qb[h * B_DQ:(h + 1) * B_DQ], preferred_element_type=F32)
        ql = (ql * lax.rsqrt(jnp.mean(ql * ql, axis=0, keepdims=True) + EPS) * gql_ref[...]).astype(BF16)
        for t, sl in subtiles:
            qlt_ref[0, t, :, h * tw:(h + 1) * tw] = ql[:, sl]

    c_ref[0] = _rms_lanes(proj_r(_R_C, _R_IK), gcr_ref[...]).astype(BF16)

    ct = proj_t(_T_C, _T_IQ)
    ct = (ct * lax.rsqrt(jnp.mean(ct * ct, axis=0, keepdims=True) + EPS) * gcc_ref[...]).astype(BF16)
    for t, sl in subtiles:
        ct_ref[0, t] = ct[:, sl]

    va = proj_t(_T_VA, _T_QB).astype(BF16)
    for t, sl in subtiles:
        vat_ref[0, t] = va[:, sl]

    iq = proj_t(_T_IQ, _T_END)
    for h in range(IDX_HEADS):
        iqh = iq[h * IDX_DIM:(h + 1) * IDX_DIM].astype(BF16)
        for t, sl in subtiles:
            iqt_ref[0, t, :, h * tw:(h + 1) * tw] = iqh[:, sl]
    iw0 = _T_IW - _T_IQ
    iwt_ref[0] = iq[iw0:iw0 + IDX_HEADS] * (IDX_HEADS ** -0.5 * IDX_DIM ** -0.5)

    ik_ref[0] = proj_r(_R_IK, _R_END)[:, 0:IDX_DIM].astype(BF16)


def _inproj(h3d, g_row, wt, wr, wukt, gq, gk, gql, gcr, gcc, tm, tw):
    b, s, d = h3d.shape
    nt = s // tw
    kern = functools.partial(_inproj_kernel, tm=tm, tw=tw)
    out_shape = [
        jax.ShapeDtypeStruct((b, 2 * A_HEADS * A_DK, s), BF16),
        jax.ShapeDtypeStruct((b, s, 2 * A_HEADS * A_DK), BF16),
        jax.ShapeDtypeStruct((b, nt, A_HEADS * A_DV, tw), BF16),
        jax.ShapeDtypeStruct((b, nt, B_DC, B_HEADS * tw), BF16),
        jax.ShapeDtypeStruct((b, s, B_DC), BF16),
        jax.ShapeDtypeStruct((b, nt, B_DC, tw), BF16),
        jax.ShapeDtypeStruct((b, nt, IDX_DIM, IDX_HEADS * tw), BF16),
        jax.ShapeDtypeStruct((b, s, IDX_DIM), BF16),
        jax.ShapeDtypeStruct((b, IDX_HEADS, s), F32),
    ]
    k = tm // tw
    out_specs = [
        pl.BlockSpec((1, 2 * A_HEADS * A_DK, tm), lambda bi, i: (bi, 0, i)),
        pl.BlockSpec((1, tm, 2 * A_HEADS * A_DK), lambda bi, i: (bi, i, 0)),
        pl.BlockSpec((1, k, A_HEADS * A_DV, tw), lambda bi, i: (bi, i, 0, 0)),
        pl.BlockSpec((1, k, B_DC, B_HEADS * tw), lambda bi, i: (bi, i, 0, 0)),
        pl.BlockSpec((1, tm, B_DC), lambda bi, i: (bi, i, 0)),
        pl.BlockSpec((1, k, B_DC, tw), lambda bi, i: (bi, i, 0, 0)),
        pl.BlockSpec((1, k, IDX_DIM, IDX_HEADS * tw), lambda bi, i: (bi, i, 0, 0)),
        pl.BlockSpec((1, tm, IDX_DIM), lambda bi, i: (bi, i, 0)),
        pl.BlockSpec((1, IDX_HEADS, tm), lambda bi, i: (bi, 0, i)),
    ]
    consts = (g_row, wt, wr, wukt, gq, gk, gql, gcr, gcc)
    return pl.pallas_call(
        kern,
        grid=(b, s // tm),
        in_specs=[pl.BlockSpec((1, tm, d), lambda bi, i: (bi, i, 0))] + [_const_spec(a.shape) for a in consts],
        out_specs=out_specs,
        out_shape=out_shape,
        compiler_params=_params(2),
        name="inproj",
    )(h3d, *consts)


def _key_to_f32(u):
    bits = u ^ (jnp.right_shift(u, 31) & 0x7FFFFFFF)
    return lax.bitcast_convert_type(bits, F32)


def _online(s_ref, lanes, m_ref, l_ref, acc_ref, v_lhs, first=False):
    if first:
        m_new = jnp.max(s_ref[...], axis=0, keepdims=True)
        m_ref[:, lanes] = m_new
        p = jnp.exp2(s_ref[...] - m_new)
        l_ref[:, lanes] = jnp.sum(p, axis=0, keepdims=True)
        acc_ref[:, lanes] = jnp.dot(v_lhs, p.astype(BF16), preferred_element_type=F32)
        return
    m_old = m_ref[:, lanes]
    m_new = jnp.maximum(m_old, jnp.max(s_ref[...], axis=0, keepdims=True))
    m_ref[:, lanes] = m_new
    alpha = jnp.exp2(m_old - m_new)
    p = jnp.exp2(s_ref[...] - m_new)
    l_ref[:, lanes] = alpha * l_ref[:, lanes] + jnp.sum(p, axis=0, keepdims=True)
    acc_ref[:, lanes] = alpha * acc_ref[:, lanes] + jnp.dot(v_lhs, p.astype(BF16),
                                                           preferred_element_type=F32)


def _attn_kernel(qat_ref, qlt_ref, iqt_ref, iwt_ref, h_ref,
                 ka_ref, vat_ref, c_ref, ct_ref, ik_ref,
                 kam_ref, vatm_ref, cm_ref, ctm_ref, ikm_ref,
                 tall_ref, tmall_ref, lamp_ref, subln_ref, wuvt_ref, wout_ref,
                 o_ref,
                 st_ref, stm_ref, hi_ref, him_ref, b1_ref, b1m_ref, b0_ref, b0m_ref,
                 key_ref, cnt_ref, qbd_ref, sa_ref, sd_ref, m_ref, l_ref, acc_ref, m2_ref, l2_ref, acc2_ref, ot_ref):
    i = pl.program_id(1)
    n_tiles = i + 1
    midx = jnp.minimum(i, 1)

    zq = jnp.zeros((A_DK, TQ), BF16)
    for h in range(A_HEADS):
        q0 = qat_ref[0, (2 * h) * A_DK:(2 * h + 1) * A_DK, :]
        q1 = qat_ref[0, (2 * h + 1) * A_DK:(2 * h + 2) * A_DK, :]
        qbd_ref[h, 0:A_DK, 0:TQ] = q0
        qbd_ref[h, 0:A_DK, TQ:A_LANES] = zq
        qbd_ref[h, A_DK:2 * A_DK, 0:TQ] = zq
        qbd_ref[h, A_DK:2 * A_DK, TQ:A_LANES] = q1

    def a_tile(k_get, v_get, bias_get, first=False):
        rows = v_get(0).shape[1]
        for h in range(A_HEADS):
            bias = bias_get(h)
            sa_ref[h, 0:rows, :] = (jnp.dot(k_get(h), qbd_ref[h], preferred_element_type=F32)
                                    + jnp.concatenate([bias, bias], axis=1))
        for h in range(A_HEADS):
            _online(sa_ref.at[h, 0:rows, :], slice(h * A_LANES, (h + 1) * A_LANES),
                    m_ref, l_ref, acc_ref, v_get(h), first)

    qrel = i * TQ + lax.broadcasted_iota(jnp.int32, (1, TQ), 1)

    def idx_scores(ik_tile):
        d = jnp.dot(ik_tile, iqt_ref[0, 0], preferred_element_type=F32)
        acc = jnp.zeros((ik_tile.shape[0], TQ), F32)
        for h in range(IDX_HEADS):
            acc = acc + iwt_ref[0, h:h + 1, :] * jnp.maximum(d[:, h * TQ:(h + 1) * TQ], 0.0)
        return acc

    def store_digits(st, hi_dst, b1_dst, b0_dst):
        bits = lax.bitcast_convert_type(st, jnp.int32)
        key = bits ^ (jnp.right_shift(bits, 31) & 0x7FFFFFFF)
        hi_dst[...] = lax.bitcast_convert_type(bits & jnp.int32(-65536), F32).astype(BF16)
        b1_dst[...] = (jnp.right_shift(key, 8) & 255).astype(F32).astype(BF16)
        b0_dst[...] = (key & 255).astype(F32).astype(BF16)

    mrow = lax.broadcasted_iota(jnp.int32, (N_META, TQ), 0)
    trow = lax.broadcasted_iota(jnp.int32, (TQ, TQ), 0)

    a_tile(lambda h: kam_ref[0, 0:N_META, h * 2 * A_DK:(h + 1) * 2 * A_DK],
           lambda h: vatm_ref[0, 0, h * A_DV:(h + 1) * A_DV, 0:N_META],
           lambda h: tmall_ref[h, midx], first=True)
    sm = idx_scores(ikm_ref[0, 0:N_META, :])
    stm_ref[...] = sm
    store_digits(sm, him_ref, b1m_ref, b0m_ref)

    assert KEY_TILES_PER_STEP == 4
    n_quads = jnp.right_shift(n_tiles, 2)
    has_pair = (n_tiles & 2) != 0
    has_tail = (n_tiles & 1) != 0
    cat = lambda parts, axis: parts[0] if len(parts) == 1 else jnp.concatenate(parts, axis=axis)

    def walk_tiles(visit):
        def body(q, carry):
            visit([KEY_TILES_PER_STEP * q + t for t in range(KEY_TILES_PER_STEP)])
            return carry

        lax.fori_loop(0, n_quads, body, 0)

        @pl.when(has_pair)
        def _():
            visit([KEY_TILES_PER_STEP * n_quads, KEY_TILES_PER_STEP * n_quads + 1])

        @pl.when(has_tail)
        def _():
            visit([n_tiles - 1])

    def key_rows(js):
        return pl.ds(pl.multiple_of(js[0] * TQ, TQ), len(js) * TQ)

    def bias_rows(js):
        kidx = [jnp.minimum(i - j, 2) for j in js]
        return lambda h: cat([tall_ref[h, k] for k in kidx], 0)

    def ab_tiles(js):
        rows = key_rows(js)
        a_tile(lambda h: ka_ref[0, rows, h * 2 * A_DK:(h + 1) * 2 * A_DK],
               lambda h: cat([vat_ref[0, j, h * A_DV:(h + 1) * A_DV, :] for j in js], 1),
               bias_rows(js))
        for t0 in range(0, len(js), 2):
            jj = js[t0:t0 + 2]
            sc = idx_scores(ik_ref[0, key_rows(jj), :])
            for t, j in enumerate(jj):
                sx = jnp.where(j * TQ + trow <= qrel, sc[t * TQ:(t + 1) * TQ], -jnp.inf)
                st_ref[j] = sx
                store_digits(sx, hi_ref.at[j], b1_ref.at[j], b0_ref.at[j])

    walk_tiles(ab_tiles)

    lp = lamp_ref[...]
    lam = (jnp.exp(jnp.sum(lp[0:1] * lp[1:2], axis=-1, keepdims=True))
           - jnp.exp(jnp.sum(lp[2:3] * lp[3:4], axis=-1, keepdims=True)) + LAM_INIT)
    for h in range(A_HEADS):
        l0 = slice(h * A_LANES, h * A_LANES + TQ)
        l1 = slice(h * A_LANES + TQ, (h + 1) * A_LANES)
        o = acc_ref[:, l0] * (1.0 / l_ref[:, l0]) - acc_ref[:, l1] * (lam / l_ref[:, l1])
        o = o * lax.rsqrt(jnp.mean(o * o, axis=0, keepdims=True) + EPS) * subln_ref[...]
        ot_ref[h * A_DV:(h + 1) * A_DV, :] = o.astype(BF16)

    kf = jnp.minimum(IDX_TOPK_MAX, qrel + (N_META + 1)).astype(F32)
    one, zero = jnp.full((), 1, BF16), jnp.full((), 0, BF16)

    ge = lambda x, t: x >= t
    gt = lambda x, t: x > t
    as_f32 = lambda d: d.astype(F32)

    def top_half_value(w):
        k = w - 32768
        b16 = (k ^ (jnp.right_shift(k, 15) & 0x7FFF)) & 0xFFFF
        return lax.bitcast_convert_type(jnp.left_shift(b16, 16), F32)

    def find_threshold(n):
        def pk_count(meta_ref, tile_ref, pred, t_row):
            tb = jnp.broadcast_to(t_row.astype(BF16), (PACK, TQ))[None]
            accs = [None] * COUNT_CHAINS
            k = 0
            for x in [meta_ref[...]] + [tile_ref[j] for j in range(n)]:
                x3 = x.reshape(x.shape[0] // PACK, PACK, TQ)
                ind = jnp.where(pred(x3, tb), one, zero)
                for r in range(ind.shape[0]):
                    c = k % COUNT_CHAINS
                    accs[c] = ind[r] if accs[c] is None else accs[c] + ind[r]
                    k += 1
            acc = functools.reduce(lambda a, b: a + b, [a for a in accs if a is not None])
            return jnp.sum(acc.astype(F32), axis=0, keepdims=True)

        def pk_keep(meta_ref, tile_ref, sel_meta_ref, sel_tile_ref, t_row):
            tb = jnp.broadcast_to(t_row.astype(BF16), (1, TQ))
            meta_ref[...] = jnp.where(sel_meta_ref[...] == tb, meta_ref[...], -one)
            for j in range(n):
                tile_ref[j] = jnp.where(sel_tile_ref[j] == tb, tile_ref[j], -one)

        def bisect(n_bits, to_value, meta_ref, tile_ref, need):
            def body(b, w):
                cand = w | jnp.left_shift(jnp.int32(1), n_bits - 1 - b)
                return jnp.where(pk_count(meta_ref, tile_ref, ge, to_value(cand)) >= need, cand, w)

            return lax.fori_loop(0, n_bits, body, jnp.zeros((1, TQ), jnp.int32))

        w_hi = bisect(16, top_half_value, him_ref, hi_ref, kf)
        need1 = kf - pk_count(him_ref, hi_ref, gt, top_half_value(w_hi))
        pk_keep(b1m_ref, b1_ref, him_ref, hi_ref, top_half_value(w_hi))
        d1 = bisect(8, as_f32, b1m_ref, b1_ref, need1)
        need0 = need1 - pk_count(b1m_ref, b1_ref, gt, as_f32(d1))
        pk_keep(b0m_ref, b0_ref, b1m_ref, b1_ref, as_f32(d1))
        d0 = bisect(8, as_f32, b0m_ref, b0_ref, need0)
        return jnp.left_shift(w_hi - 32768, 16) | jnp.left_shift(d1, 8) | d0

    assert PACK * st_ref.shape[0] + N_META // PACK <= 256, "bf16 counters must stay exact"
    for n in range(1, st_ref.shape[0] + 1):
        @pl.when(n_tiles == n)
        def _(n=n):
            key = find_threshold(n)
            key_ref[...] = key
            t = _key_to_f32(key)
            gts, ges = [None] * COUNT_CHAINS, [None] * COUNT_CHAINS
            k = 0
            for x in [stm_ref[...]] + [st_ref[j] for j in range(n)]:
                x3 = x.reshape(x.shape[0] // 8, 8, TQ)
                for r in range(x3.shape[0]):
                    c = k % COUNT_CHAINS
                    gt1, ge1 = jnp.where(x3[r] > t, 1.0, 0.0), jnp.where(x3[r] >= t, 1.0, 0.0)
                    gts[c] = gt1 if gts[c] is None else gts[c] + gt1
                    ges[c] = ge1 if ges[c] is None else ges[c] + ge1
                    k += 1
            total = lambda parts: jnp.sum(functools.reduce(lambda a, b: a + b, parts), axis=0, keepdims=True)
            cnt_ref[0:1, :] = total(gts)
            cnt_ref[1:2, :] = total(ges)

    def count(ind_meta, ind_tile):
        p = ind_meta(stm_ref[...]).reshape(N_META // 8, 8, TQ).sum(axis=0)
        p = lax.fori_loop(0, n_tiles,
                          lambda j, p: p + ind_tile(st_ref[j], j).reshape(TQ // 8, 8, TQ).sum(axis=0), p)
        return jnp.sum(p, axis=0, keepdims=True)

    def count_ge(t):
        ind = lambda x: jnp.where(x >= t, 1.0, 0.0)
        return count(ind, lambda x, j: ind(x))

    def count_gt(t):
        ind = lambda x: jnp.where(x > t, 1.0, 0.0)
        return count(ind, lambda x, j: ind(x))

    state = jnp.max(jnp.where((cnt_ref[0:1, :] >= kf) | (cnt_ref[1:2, :] < kf), 2.0,
                              jnp.where(cnt_ref[1:2, :] > kf, 1.0, 0.0)))

    @pl.when(state > 1.5)
    def _():
        def bit_body(b, u):
            cand = u + jnp.left_shift(jnp.int32(1), 31 - b)
            return jnp.where(count_ge(_key_to_f32(cand)) >= kf, cand, u)

        u = lax.fori_loop(0, 32, bit_body, jnp.full((1, TQ), INT_MIN, jnp.int32))
        key_ref[...] = u
        cnt_ref[0:1, :] = count_gt(_key_to_f32(u))
        cnt_ref[1:2, :] = count_ge(_key_to_f32(u))

    thr = _key_to_f32(key_ref[...])

    need = kf - cnt_ref[0:1, :]

    @pl.when(state > 0.5)
    def _():
        def count_ties_before(xc):
            return count(lambda x: jnp.where((x == thr) & (mrow < xc), 1.0, 0.0),
                         lambda x, j: jnp.where((x == thr) & (j * TQ + N_META + trow < xc), 1.0, 0.0))

        def pos_body(b, xs):
            cand = xs | jnp.left_shift(jnp.int32(1), 11 - b)
            return jnp.where(count_ties_before(cand) < need, cand, xs)

        last = lax.fori_loop(0, 12, pos_body, jnp.zeros((1, TQ), jnp.int32))
        sm = stm_ref[...]
        stm_ref[...] = jnp.where((sm == thr) & (mrow > last), -jnp.inf, sm)

        def drop_body(j, carry):
            sx = st_ref[j]
            st_ref[j] = jnp.where((sx == thr) & (j * TQ + N_META + trow > last), -jnp.inf, sx)
            return carry

        lax.fori_loop(0, n_tiles, drop_body, 0)

    def d_tile(st_tile, c_tile, ct_tile, bias_get, first=False):
        sel = jnp.where(st_tile >= thr, 0.0, NEG)
        rows = c_tile.shape[0]
        for h in range(B_HEADS):
            sd_ref[h, 0:rows, :] = (jnp.dot(c_tile, qlt_ref[0, 0, :, h * TQ:(h + 1) * TQ],
                                            preferred_element_type=F32)
                                    + (bias_get(A_HEADS + h) + sel))
        for h in range(B_HEADS):
            _online(sd_ref.at[h, 0:rows, :], slice(h * TQ, (h + 1) * TQ), m2_ref, l2_ref, acc2_ref,
                    ct_tile, first)

    d_tile(stm_ref[...], cm_ref[0, 0:N_META, :], ctm_ref[0, 0, :, 0:N_META], lambda h: tmall_ref[h, midx],
           first=True)

    def d_tiles(js):
        d_tile(cat([st_ref[j] for j in js], 0), c_ref[0, key_rows(js), :],
               cat([ct_ref[0, j] for j in js], 1), bias_rows(js))

    walk_tiles(d_tiles)

    for h in range(B_HEADS):
        lanes = slice(h * TQ, (h + 1) * TQ)
        olat = (acc2_ref[:, lanes] * (1.0 / l2_ref[:, lanes])).astype(BF16)
        ob = jnp.dot(wuvt_ref[h], olat, preferred_element_type=F32)
        r0 = A_HEADS * A_DV + h * B_DV
        ot_ref[r0:r0 + B_DV, :] = ob.astype(BF16)

    y = lax.dot_general(ot_ref[...], wout_ref[...], (((0,), (0,)), ((), ())),
                        preferred_element_type=F32)
    o_ref[0] = h_ref[0] + y


def _attention(q_side, h3d, k_side, meta_side, tall, tmall, lamp, subln, wuvt, wout):
    qat, qlt, iqt, iwt = q_side
    ka, vat, c, ct, ik = k_side
    kam, vatm, cm, ctm, ikm = meta_side
    b, s, d = h3d.shape
    nt = s // TQ
    in_specs = [
        pl.BlockSpec((1, qat.shape[1], TQ), lambda bi, i: (bi, 0, i)),
        pl.BlockSpec((1, 1) + qlt.shape[2:], lambda bi, i: (bi, i, 0, 0)),
        pl.BlockSpec((1, 1) + iqt.shape[2:], lambda bi, i: (bi, i, 0, 0)),
        pl.BlockSpec((1, iwt.shape[1], TQ), lambda bi, i: (bi, 0, i)),
        pl.BlockSpec((1, TQ, d), lambda bi, i: (bi, i, 0)),
        pl.BlockSpec((1,) + ka.shape[1:], lambda bi, i: (bi, 0, 0)),
        pl.BlockSpec((1,) + vat.shape[1:], lambda bi, i: (bi, 0, 0, 0)),
        pl.BlockSpec((1,) + c.shape[1:], lambda bi, i: (bi, 0, 0)),
        pl.BlockSpec((1,) + ct.shape[1:], lambda bi, i: (bi, 0, 0, 0)),
        pl.BlockSpec((1,) + ik.shape[1:], lambda bi, i: (bi, 0, 0)),
    ] + [_const_spec(a.shape) for a in (kam, vatm, cm, ctm, ikm, tall, tmall, lamp, subln, wuvt, wout)]
    scratch = [
        pltpu.VMEM((nt, TQ, TQ), F32),
        pltpu.VMEM((N_META, TQ), F32),
        pltpu.VMEM((nt, TQ, TQ), BF16), pltpu.VMEM((N_META, TQ), BF16),
        pltpu.VMEM((nt, TQ, TQ), BF16), pltpu.VMEM((N_META, TQ), BF16),
        pltpu.VMEM((nt, TQ, TQ), BF16), pltpu.VMEM((N_META, TQ), BF16),
        pltpu.VMEM((1, TQ), jnp.int32),
        pltpu.VMEM((8, TQ), F32),
        pltpu.VMEM((A_HEADS, 2 * A_DK, A_LANES), BF16),
        pltpu.VMEM((A_HEADS, KEY_TILES_PER_STEP * TQ, A_LANES), F32),
        pltpu.VMEM((B_HEADS, KEY_TILES_PER_STEP * TQ, TQ), F32),
        pltpu.VMEM((1, A_HEADS * A_LANES), F32),
        pltpu.VMEM((1, A_HEADS * A_LANES), F32),
        pltpu.VMEM((A_DV, A_HEADS * A_LANES), F32),
        pltpu.VMEM((1, B_HEADS * TQ), F32),
        pltpu.VMEM((1, B_HEADS * TQ), F32),
        pltpu.VMEM((B_DC, B_HEADS * TQ), F32),
        pltpu.VMEM((A_HEADS * A_DV + B_HEADS * B_DV, TQ), BF16),
    ]
    return pl.pallas_call(
        _attn_kernel,
        grid=(b, nt),
        in_specs=in_specs,
        out_specs=pl.BlockSpec((1, TQ, d), lambda bi, i: (bi, i, 0)),
        out_shape=jax.ShapeDtypeStruct((b, s, d), F32),
        scratch_shapes=scratch,
        compiler_params=_params(2),
        name="attention",
    )(qat, qlt, iqt, iwt, h3d, ka, vat, c, ct, ik, kam, vatm, cm, ctm, ikm,
      tall, tmall, lamp, subln, wuvt, wout)


def kernel(x, meta_tokens, rel_bias, ffn1_norm, ffn1_w_gate, ffn1_w_up, ffn1_w_down, mix_norm, w_in, a_q_norm, a_k_norm, a_lambda_q1, a_lambda_k1, a_lambda_q2, a_lambda_k2, a_subln, b_kv_norm, b_w_uk, b_q_norm, b_w_uv, w_out, ffn2_norm, ffn2_w_gate, ffn2_w_up, ffn2_w_down):
    bsz, seq, d = x.shape
    assert d == D_MODEL and seq % PROJ_TM == 0 and (bsz * seq) % FFN_TM == 0
    assert min(IDX_TOPK_MAX, (seq + N_META) // 4) == IDX_TOPK_MAX
    assert ffn1_norm.shape[0] == 1, "single layer"
    lyr = 0
    row = lambda v: v.reshape(1, -1).astype(F32)
    col = lambda v: v.reshape(-1, 1).astype(F32)

    w = w_in[lyr]
    o_qa, o_ka, o_va, o_qb, o_c, o_iq, o_ik, o_iw = np.cumsum(
        [0, 2 * A_HEADS * A_DK, 2 * A_HEADS * A_DK, A_HEADS * A_DV, B_HEADS * B_DQ, B_DC,
         IDX_HEADS * IDX_DIM, IDX_DIM])[:8]
    w_qa, w_ka, w_va, w_qb = w[:, o_qa:o_ka], w[:, o_ka:o_va], w[:, o_va:o_qb], w[:, o_qb:o_c]
    w_c, w_iq, w_ik, w_iw = w[:, o_c:o_iq], w[:, o_iq:o_ik], w[:, o_ik:o_iw], w[:, o_iw:o_iw + IDX_HEADS]
    wt = jnp.concatenate([w_qa, w_va, w_qb, w_c, w_iq, w_iw,
                          jnp.zeros((d, _T_END - _T_IW - IDX_HEADS), w.dtype)], axis=1).T.astype(BF16)
    wr = jnp.concatenate([w_ka, w_c, w_ik, jnp.zeros((d, _R_END - _R_IK - IDX_DIM), w.dtype)],
                         axis=1).astype(BF16)
    wukt = jnp.transpose(b_w_uk[lyr], (0, 2, 1)).astype(BF16)
    wuvt = jnp.transpose(b_w_uv[lyr], (0, 2, 1)).astype(BF16)
    gq = col(jnp.tile(a_q_norm[lyr], 2 * A_HEADS)) * (A_SCALE * LOG2E)
    gk = row(a_k_norm[lyr])
    gql = col(b_q_norm[lyr]) * (B_SCALE * LOG2E)
    gcr, gcc = row(b_kv_norm[lyr]), col(b_kv_norm[lyr])
    subln = col(a_subln[lyr]) * (1.0 - LAM_INIT)
    lamp = jnp.stack([a_lambda_q1[lyr], a_lambda_k1[lyr], a_lambda_q2[lyr], a_lambda_k2[lyr]]).astype(F32)
    proj_consts = (row(mix_norm[lyr]), wt, wr, wukt, gq, gk, gql, gcr, gcc)

    tall, tmall = _bias_tiles(rel_bias.astype(F32))

    ffn1 = (row(ffn1_norm[lyr]), ffn1_w_gate[lyr].astype(BF16), ffn1_w_up[lyr].astype(BF16),
            ffn1_w_down[lyr].astype(BF16))
    h1 = _ffn(x.reshape(bsz * seq, d), *ffn1, tm=FFN_TM).reshape(bsz, seq, d)
    meta_pad = jnp.pad(meta_tokens.astype(x.dtype), ((0, META_PAD - N_META), (0, 0)))
    h1m = _ffn(meta_pad, *ffn1, tm=META_PAD).reshape(1, META_PAD, d)

    qat, ka, vat, qlt, c, ct, iqt, ik, iwt = _inproj(h1, *proj_consts, tm=PROJ_TM, tw=TQ)
    _, kam, vatm, _, cm, ctm, _, ikm, _ = _inproj(h1m, *proj_consts, tm=META_PAD, tw=META_PAD)

    h2 = _attention((qat, qlt, iqt, iwt), h1, (ka, vat, c, ct, ik), (kam, vatm, cm, ctm, ikm),
                    tall, tmall, lamp, subln, wuvt, w_out[lyr].astype(BF16))

    ffn2 = (row(ffn2_norm[lyr]), ffn2_w_gate[lyr].astype(BF16), ffn2_w_up[lyr].astype(BF16),
            ffn2_w_down[lyr].astype(BF16))
    return _ffn(h2.reshape(bsz * seq, d), *ffn2, tm=FFN_TM).reshape(bsz, seq, d)
```

```python
import functools
import math

import jax
import jax.numpy as jnp
import numpy as np
from jax import lax
from jax.experimental import pallas as pl
from jax.experimental.pallas import tpu as pltpu

F32 = jnp.float32
BF16 = jnp.bfloat16

D_MODEL = 1024
N_META = 16
A_HEADS = 4
A_DK = 64
A_DV = 128
B_HEADS = 4
B_DQ = 128
B_DC = 256
B_DV = 128
IDX_HEADS = 8
IDX_DIM = 64
IDX_TOPK_MAX = 256
REL_BUCKETS = 32
REL_MAX_DIST = 128
D_FF = 2816
EPS = 1e-6
LOG2E = math.log2(math.e)
A_SCALE = A_DK ** -0.5
B_SCALE = B_DC ** -0.5
LAM_INIT = 0.8 - 0.6 * math.exp(-0.3 * 0)

NEG = -1e30
INT_MIN = -(2 ** 31)

TQ = 256
META_PAD = 128
FF_CHUNK = D_FF
FFN_TM = 512
PROJ_TM = 512
V7X_VMEM_LIMIT = 56 * 1024 * 1024

KEY_TILES_PER_STEP = 4
A_LANES = 2 * TQ
PACK = 16
COUNT_CHAINS = 4
assert N_META % PACK == 0
_T_QA, _T_VA, _T_QB, _T_C, _T_IQ, _T_IW, _T_END = 0, 512, 1024, 1536, 1792, 2304, 2320
_R_KA, _R_C, _R_IK, _R_END = 0, 512, 768, 896


def _bucket_edges():
    max_exact = REL_BUCKETS // 2
    n = np.arange(0, 4 * REL_MAX_DIST)
    nf = np.maximum(n, 1).astype(np.float64)
    large = max_exact + (np.log(nf / max_exact) / math.log(REL_MAX_DIST / max_exact)
                         * (REL_BUCKETS - max_exact)).astype(np.int64)
    large = np.minimum(large, REL_BUCKETS - 1)
    bucket = np.where(n < max_exact, n, large)
    return [int(n[np.argmax(bucket >= j)]) for j in range(REL_BUCKETS)]


_EDGES = _bucket_edges()
assert _EDGES[-1] <= TQ, "tiles two or more away from the diagonal must sit in the last bucket"


def _const_spec(shape):
    nd = len(shape)
    return pl.BlockSpec(shape, lambda *_: (0,) * nd, pipeline_mode=pl.Buffered(1))


def _params(n_grid):
    return pltpu.CompilerParams(dimension_semantics=("arbitrary",) * n_grid,
                                vmem_limit_bytes=V7X_VMEM_LIMIT)


def _rms_lanes(x, g_row):
    ms = jnp.mean(x * x, axis=-1, keepdims=True)
    return x * lax.rsqrt(ms + EPS) * g_row


def _bias_kernel(rb_ref, tall_ref, tmall_ref):
    h = pl.program_id(0)

    def table(d):
        val = jnp.full(d.shape, rb_ref[0, h] * LOG2E, F32)
        for j in range(1, REL_BUCKETS):
            val = jnp.where(d >= _EDGES[j], rb_ref[j, h] * LOG2E, val)
        return val

    far = rb_ref[REL_BUCKETS - 1, h] * LOG2E
    c = lax.broadcasted_iota(jnp.int32, (TQ, TQ), 0)
    r = lax.broadcasted_iota(jnp.int32, (TQ, TQ), 1)
    d0 = r - c
    tall_ref[0, 0] = jnp.where(d0 >= 0, table(d0), NEG)
    tall_ref[0, 1] = table(d0 + TQ)
    tall_ref[0, 2] = jnp.full((TQ, TQ), far, F32)
    m = lax.broadcasted_iota(jnp.int32, (N_META, TQ), 0)
    r2 = lax.broadcasted_iota(jnp.int32, (N_META, TQ), 1)
    tmall_ref[0, 0] = table(r2 + N_META - m)
    tmall_ref[0, 1] = jnp.full((N_META, TQ), far, F32)


def _bias_tiles(rel_bias):
    nh = rel_bias.shape[1]
    return pl.pallas_call(
        _bias_kernel,
        grid=(nh,),
        in_specs=[pl.BlockSpec(memory_space=pltpu.SMEM)],
        out_specs=[pl.BlockSpec((1, 3, TQ, TQ), lambda h: (h, 0, 0, 0)),
                   pl.BlockSpec((1, 2, N_META, TQ), lambda h: (h, 0, 0, 0))],
        out_shape=[jax.ShapeDtypeStruct((nh, 3, TQ, TQ), F32),
                   jax.ShapeDtypeStruct((nh, 2, N_META, TQ), F32)],
        compiler_params=_params(1),
        name="bias_tiles",
    )(rel_bias)


def _ffn_kernel(x_ref, g_ref, wg_ref, wu_ref, wd_ref, o_ref):
    x = x_ref[...]
    xn = _rms_lanes(x, g_ref[...]).astype(BF16)
    y = None
    for c in range(D_FF // FF_CHUNK):
        sl = slice(c * FF_CHUNK, (c + 1) * FF_CHUNK)
        g = jnp.dot(xn, wg_ref[:, sl], preferred_element_type=F32)
        u = jnp.dot(xn, wu_ref[:, sl], preferred_element_type=F32)
        a = (g * jax.nn.sigmoid(g) * u).astype(BF16)
        part = jnp.dot(a, wd_ref[sl, :], preferred_element_type=F32)
        y = part if y is None else y + part
    o_ref[...] = x + 0.5 * y


def _ffn(h2d, g_row, wg, wu, wd, tm):
    m, d = h2d.shape
    return pl.pallas_call(
        _ffn_kernel,
        grid=(m // tm,),
        in_specs=[pl.BlockSpec((tm, d), lambda i: (i, 0)),
                  _const_spec((1, d)),
                  _const_spec(wg.shape), _const_spec(wu.shape), _const_spec(wd.shape)],
        out_specs=pl.BlockSpec((tm, d), lambda i: (i, 0)),
        out_shape=jax.ShapeDtypeStruct((m, d), F32),
        compiler_params=_params(1),
        name="ffn",
    )(h2d, g_row, wg, wu, wd)


def _inproj_kernel(h_ref, g_ref, wt_ref, wr_ref, wukt_ref, gq_ref, gk_ref, gql_ref, gcr_ref, gcc_ref,
                   qat_ref, ka_ref, vat_ref, qlt_ref, c_ref, ct_ref, iqt_ref, ik_ref, iwt_ref, *, tm, tw):
    xn = _rms_lanes(h_ref[0], g_ref[...]).astype(BF16)
    subtiles = [(t, slice(t * tw, (t + 1) * tw)) for t in range(tm // tw)]

    def proj_t(r0, r1):
        return lax.dot_general(wt_ref[r0:r1, :], xn, (((1,), (1,)), ((), ())), preferred_element_type=F32)

    def proj_r(c0, c1):
        return jnp.dot(xn, wr_ref[:, c0:c1], preferred_element_type=F32)

    ka = proj_r(_R_KA, _R_C)
    for hm in range(2 * A_HEADS):
        cols = slice(hm * A_DK, (hm + 1) * A_DK)
        ka_ref[0, :, cols] = _rms_lanes(ka[:, cols], gk_ref[...]).astype(BF16)

    qa = proj_t(_T_QA, _T_VA).reshape(2 * A_HEADS, A_DK, tm)
    qa = qa * lax.rsqrt(jnp.mean(qa * qa, axis=1, keepdims=True) + EPS)
    qat_ref[0] = (qa.reshape(2 * A_HEADS * A_DK, tm) * gq_ref[...]).astype(BF16)

    qb = proj_t(_T_QB, _T_C).astype(BF16)
    for h in range(B_HEADS):
        ql = jnp.dot(wukt_ref[h], qb[h * B_DQ:(h + 1) * B_DQ], preferred_element_type=F32)
        ql = (ql * lax.rsqrt(jnp.mean(ql * ql, axis=0, keepdims=True) + EPS) * gql_ref[...]).astype(BF16)
        for t, sl in subtiles:
            qlt_ref[0, t, :, h * tw:(h + 1) * tw] = ql[:, sl]

    c_ref[0] = _rms_lanes(proj_r(_R_C, _R_IK), gcr_ref[...]).astype(BF16)

    ct = proj_t(_T_C, _T_IQ)
    ct = (ct * lax.rsqrt(jnp.mean(ct * ct, axis=0, keepdims=True) + EPS) * gcc_ref[...]).astype(BF16)
    for t, sl in subtiles:
        ct_ref[0, t] = ct[:, sl]

    va = proj_t(_T_VA, _T_QB).astype(BF16)
    for t, sl in subtiles:
        vat_ref[0, t] = va[:, sl]

    iq = proj_t(_T_IQ, _T_END)
    for h in range(IDX_HEADS):
        iqh = iq[h * IDX_DIM:(h + 1) * IDX_DIM].astype(BF16)
        for t, sl in subtiles:
            iqt_ref[0, t, :, h * tw:(h + 1) * tw] = iqh[:, sl]
    iw0 = _T_IW - _T_IQ
    iwt_ref[0] = iq[iw0:iw0 + IDX_HEADS] * (IDX_HEADS ** -0.5 * IDX_DIM ** -0.5)

    ik_ref[0] = proj_r(_R_IK, _R_END)[:, 0:IDX_DIM].astype(BF16)


def _inproj(h3d, g_row, wt, wr, wukt, gq, gk, gql, gcr, gcc, tm, tw):
    b, s, d = h3d.shape
    nt = s // tw
    kern = functools.partial(_inproj_kernel, tm=tm, tw=tw)
    out_shape = [
        jax.ShapeDtypeStruct((b, 2 * A_HEADS * A_DK, s), BF16),
        jax.ShapeDtypeStruct((b, s, 2 * A_HEADS * A_DK), BF16),
        jax.ShapeDtypeStruct((b, nt, A_HEADS * A_DV, tw), BF16),
        jax.ShapeDtypeStruct((b, nt, B_DC, B_HEADS * tw), BF16),
        jax.ShapeDtypeStruct((b, s, B_DC), BF16),
        jax.ShapeDtypeStruct((b, nt, B_DC, tw), BF16),
        jax.ShapeDtypeStruct((b, nt, IDX_DIM, IDX_HEADS * tw), BF16),
        jax.ShapeDtypeStruct((b, s, IDX_DIM), BF16),
        jax.ShapeDtypeStruct((b, IDX_HEADS, s), F32),
    ]
    k = tm // tw
    out_specs = [
        pl.BlockSpec((1, 2 * A_HEADS * A_DK, tm), lambda bi, i: (bi, 0, i)),
        pl.BlockSpec((1, tm, 2 * A_HEADS * A_DK), lambda bi, i: (bi, i, 0)),
        pl.BlockSpec((1, k, A_HEADS * A_DV, tw), lambda bi, i: (bi, i, 0, 0)),
        pl.BlockSpec((1, k, B_DC, B_HEADS * tw), lambda bi, i: (bi, i, 0, 0)),
        pl.BlockSpec((1, tm, B_DC), lambda bi, i: (bi, i, 0)),
        pl.BlockSpec((1, k, B_DC, tw), lambda bi, i: (bi, i, 0, 0)),
        pl.BlockSpec((1, k, IDX_DIM, IDX_HEADS * tw), lambda bi, i: (bi, i, 0, 0)),
        pl.BlockSpec((1, tm, IDX_DIM), lambda bi, i: (bi, i, 0)),
        pl.BlockSpec((1, IDX_HEADS, tm), lambda bi, i: (bi, 0, i)),
    ]
    consts = (g_row, wt, wr, wukt, gq, gk, gql, gcr, gcc)
    return pl.pallas_call(
        kern,
        grid=(b, s // tm),
        in_specs=[pl.BlockSpec((1, tm, d), lambda bi, i: (bi, i, 0))] + [_const_spec(a.shape) for a in consts],
        out_specs=out_specs,
        out_shape=out_shape,
        compiler_params=_params(2),
        name="inproj",
    )(h3d, *consts)


def _key_to_f32(u):
    bits = u ^ (jnp.right_shift(u, 31) & 0x7FFFFFFF)
    return lax.bitcast_convert_type(bits, F32)


def _online(s_ref, lanes, m_ref, l_ref, acc_ref, v_lhs, first=False):
    if first:
        m_new = jnp.max(s_ref[...], axis=0, keepdims=True)
        m_ref[:, lanes] = m_new
        p = jnp.exp2(s_ref[...] - m_new)
        if l_ref is not None:
            l_ref[:, lanes] = jnp.sum(p, axis=0, keepdims=True)
        acc_ref[:, lanes] = jnp.dot(v_lhs, p.astype(BF16), preferred_element_type=F32)
        return
    m_old = m_ref[:, lanes]
    m_new = jnp.maximum(m_old, jnp.max(s_ref[...], axis=0, keepdims=True))
    m_ref[:, lanes] = m_new
    alpha = jnp.exp2(m_old - m_new)
    p = jnp.exp2(s_ref[...] - m_new)
    if l_ref is not None:
        l_ref[:, lanes] = alpha * l_ref[:, lanes] + jnp.sum(p, axis=0, keepdims=True)
    acc_ref[:, lanes] = alpha * acc_ref[:, lanes] + jnp.dot(v_lhs, p.astype(BF16),
                                                           preferred_element_type=F32)


def _attn_kernel(qat_ref, qlt_ref, iqt_ref, iwt_ref, h_ref,
                 ka_ref, vat_ref, c_ref, ct_ref, ik_ref,
                 kam_ref, vatm_ref, cm_ref, ctm_ref, ikm_ref,
                 tall_ref, tmall_ref, lamp_ref, subln_ref, wuvt_ref, wout_ref,
                 o_ref,
                 st_ref, stm_ref, hi_ref, him_ref, b1_ref, b1m_ref, b0_ref, b0m_ref,
                 key_ref, cnt_ref, qbd_ref, sa_ref, sd_ref, m_ref, acc_ref, m2_ref, l2_ref, acc2_ref, ot_ref):
    i = pl.program_id(1)
    n_tiles = i + 1
    midx = jnp.minimum(i, 1)

    zq = jnp.zeros((A_DK, TQ), BF16)
    for h in range(A_HEADS):
        q0 = qat_ref[0, (2 * h) * A_DK:(2 * h + 1) * A_DK, :]
        q1 = qat_ref[0, (2 * h + 1) * A_DK:(2 * h + 2) * A_DK, :]
        qbd_ref[h, 0:A_DK, 0:TQ] = q0
        qbd_ref[h, 0:A_DK, TQ:A_LANES] = zq
        qbd_ref[h, A_DK:2 * A_DK, 0:TQ] = zq
        qbd_ref[h, A_DK:2 * A_DK, TQ:A_LANES] = q1

    def a_tile(k_get, v_get, bias_get, first=False):
        rows = v_get(0).shape[1]
        for h in range(A_HEADS):
            bias = bias_get(h)
            sa_ref[h, 0:rows, :] = (jnp.dot(k_get(h), qbd_ref[h], preferred_element_type=F32)
                                    + jnp.concatenate([bias, bias], axis=1))
        ones = jnp.ones((PACK, rows), BF16)
        for h in range(A_HEADS):
            _online(sa_ref.at[h, 0:rows, :], slice(h * A_LANES, (h + 1) * A_LANES),
                    m_ref, None, acc_ref, jnp.concatenate([v_get(h), ones], axis=0), first)

    qrel = i * TQ + lax.broadcasted_iota(jnp.int32, (1, TQ), 1)

    def idx_scores(ik_tile):
        d = jnp.dot(ik_tile, iqt_ref[0, 0], preferred_element_type=F32)
        acc = jnp.zeros((ik_tile.shape[0], TQ), F32)
        for h in range(IDX_HEADS):
            acc = acc + iwt_ref[0, h:h + 1, :] * jnp.maximum(d[:, h * TQ:(h + 1) * TQ], 0.0)
        return acc

    def store_digits(st, hi_dst, b1_dst, b0_dst):
        bits = lax.bitcast_convert_type(st, jnp.int32)
        key = bits ^ (jnp.right_shift(bits, 31) & 0x7FFFFFFF)
        hi_dst[...] = lax.bitcast_convert_type(bits & jnp.int32(-65536), F32).astype(BF16)
        b1_dst[...] = (jnp.right_shift(key, 8) & 255).astype(F32).astype(BF16)
        b0_dst[...] = (key & 255).astype(F32).astype(BF16)

    mrow = lax.broadcasted_iota(jnp.int32, (N_META, TQ), 0)
    trow = lax.broadcasted_iota(jnp.int32, (TQ, TQ), 0)

    a_tile(lambda h: kam_ref[0, 0:N_META, h * 2 * A_DK:(h + 1) * 2 * A_DK],
           lambda h: vatm_ref[0, 0, h * A_DV:(h + 1) * A_DV, 0:N_META],
           lambda h: tmall_ref[h, midx], first=True)
    sm = idx_scores(ikm_ref[0, 0:N_META, :])
    stm_ref[...] = sm
    store_digits(sm, him_ref, b1m_ref, b0m_ref)

    assert KEY_TILES_PER_STEP == 4
    n_quads = jnp.right_shift(n_tiles, 2)
    has_pair = (n_tiles & 2) != 0
    has_tail = (n_tiles & 1) != 0
    cat = lambda parts, axis: parts[0] if len(parts) == 1 else jnp.concatenate(parts, axis=axis)

    def walk_tiles(visit):
        def body(q, carry):
            visit([KEY_TILES_PER_STEP * q + t for t in range(KEY_TILES_PER_STEP)])
            return carry

        lax.fori_loop(0, n_quads, body, 0)

        @pl.when(has_pair)
        def _():
            visit([KEY_TILES_PER_STEP * n_quads, KEY_TILES_PER_STEP * n_quads + 1])

        @pl.when(has_tail)
        def _():
            visit([n_tiles - 1])

    def key_rows(js):
        return pl.ds(pl.multiple_of(js[0] * TQ, TQ), len(js) * TQ)

    def bias_rows(js):
        kidx = [jnp.minimum(i - j, 2) for j in js]
        return lambda h: cat([tall_ref[h, k] for k in kidx], 0)

    def ab_tiles(js):
        rows = key_rows(js)
        a_tile(lambda h: ka_ref[0, rows, h * 2 * A_DK:(h + 1) * 2 * A_DK],
               lambda h: cat([vat_ref[0, j, h * A_DV:(h + 1) * A_DV, :] for j in js], 1),
               bias_rows(js))
        for t0 in range(0, len(js), 2):
            jj = js[t0:t0 + 2]
            sc = idx_scores(ik_ref[0, key_rows(jj), :])
            for t, j in enumerate(jj):
                sx = jnp.where(j * TQ + trow <= qrel, sc[t * TQ:(t + 1) * TQ], -jnp.inf)
                st_ref[j] = sx
                store_digits(sx, hi_ref.at[j], b1_ref.at[j], b0_ref.at[j])

    walk_tiles(ab_tiles)

    lp = lamp_ref[...]
    lam = (jnp.exp(jnp.sum(lp[0:1] * lp[1:2], axis=-1, keepdims=True))
           - jnp.exp(jnp.sum(lp[2:3] * lp[3:4], axis=-1, keepdims=True)) + LAM_INIT)
    for h in range(A_HEADS):
        l0 = slice(h * A_LANES, h * A_LANES + TQ)
        l1 = slice(h * A_LANES + TQ, (h + 1) * A_LANES)
        den0, den1 = acc_ref[A_DV:A_DV + 1, l0], acc_ref[A_DV:A_DV + 1, l1]
        o = acc_ref[0:A_DV, l0] * (1.0 / den0) - acc_ref[0:A_DV, l1] * (lam / den1)
        o = o * lax.rsqrt(jnp.mean(o * o, axis=0, keepdims=True) + EPS) * subln_ref[...]
        ot_ref[h * A_DV:(h + 1) * A_DV, :] = o.astype(BF16)

    kf = jnp.minimum(IDX_TOPK_MAX, qrel + (N_META + 1)).astype(F32)
    one, zero = jnp.full((), 1, BF16), jnp.full((), 0, BF16)

    ge = lambda x, t: x >= t
    gt = lambda x, t: x > t
    as_f32 = lambda d: d.astype(F32)

    def top_half_value(w):
        k = w - 32768
        b16 = (k ^ (jnp.right_shift(k, 15) & 0x7FFF)) & 0xFFFF
        return lax.bitcast_convert_type(jnp.left_shift(b16, 16), F32)

    def find_threshold(n):
        def pk_count(meta_ref, tile_ref, pred, t_row):
            tb = jnp.broadcast_to(t_row.astype(BF16), (PACK, TQ))[None]
            accs = [None] * COUNT_CHAINS
            k = 0
            for x in [meta_ref[...]] + [tile_ref[j] for j in range(n)]:
                x3 = x.reshape(x.shape[0] // PACK, PACK, TQ)
                ind = jnp.where(pred(x3, tb), one, zero)
                for r in range(ind.shape[0]):
                    c = k % COUNT_CHAINS
                    accs[c] = ind[r] if accs[c] is None else accs[c] + ind[r]
                    k += 1
            acc = functools.reduce(lambda a, b: a + b, [a for a in accs if a is not None])
            return jnp.sum(acc.astype(F32), axis=0, keepdims=True)

        def pk_keep(meta_ref, tile_ref, sel_meta_ref, sel_tile_ref, t_row):
            tb = jnp.broadcast_to(t_row.astype(BF16), (1, TQ))
            meta_ref[...] = jnp.where(sel_meta_ref[...] == tb, meta_ref[...], -one)
            for j in range(n):
                tile_ref[j] = jnp.where(sel_tile_ref[j] == tb, tile_ref[j], -one)

        def bisect(n_bits, to_value, meta_ref, tile_ref, need):
            def body(b, w):
                cand = w | jnp.left_shift(jnp.int32(1), n_bits - 1 - b)
                return jnp.where(pk_count(meta_ref, tile_ref, ge, to_value(cand)) >= need, cand, w)

            return lax.fori_loop(0, n_bits, body, jnp.zeros((1, TQ), jnp.int32))

        w_hi = bisect(16, top_half_value, him_ref, hi_ref, kf)
        need1 = kf - pk_count(him_ref, hi_ref, gt, top_half_value(w_hi))
        pk_keep(b1m_ref, b1_ref, him_ref, hi_ref, top_half_value(w_hi))
        d1 = bisect(8, as_f32, b1m_ref, b1_ref, need1)
        need0 = need1 - pk_count(b1m_ref, b1_ref, gt, as_f32(d1))
        pk_keep(b0m_ref, b0_ref, b1m_ref, b1_ref, as_f32(d1))
        d0 = bisect(8, as_f32, b0m_ref, b0_ref, need0)
        return jnp.left_shift(w_hi - 32768, 16) | jnp.left_shift(d1, 8) | d0

    assert PACK * st_ref.shape[0] + N_META // PACK <= 256, "bf16 counters must stay exact"
    for n in range(1, st_ref.shape[0] + 1):
        @pl.when(n_tiles == n)
        def _(n=n):
            key = find_threshold(n)
            key_ref[...] = key
            t = _key_to_f32(key)
            gts, ges = [None] * COUNT_CHAINS, [None] * COUNT_CHAINS
            k = 0
            for x in [stm_ref[...]] + [st_ref[j] for j in range(n)]:
                x3 = x.reshape(x.shape[0] // 8, 8, TQ)
                for r in range(x3.shape[0]):
                    c = k % COUNT_CHAINS
                    gt1, ge1 = jnp.where(x3[r] > t, 1.0, 0.0), jnp.where(x3[r] >= t, 1.0, 0.0)
                    gts[c] = gt1 if gts[c] is None else gts[c] + gt1
                    ges[c] = ge1 if ges[c] is None else ges[c] + ge1
                    k += 1
            total = lambda parts: jnp.sum(functools.reduce(lambda a, b: a + b, parts), axis=0, keepdims=True)
            cnt_ref[0:1, :] = total(gts)
            cnt_ref[1:2, :] = total(ges)

    def count(ind_meta, ind_tile):
        p = ind_meta(stm_ref[...]).reshape(N_META // 8, 8, TQ).sum(axis=0)
        p = lax.fori_loop(0, n_tiles,
                          lambda j, p: p + ind_tile(st_ref[j], j).reshape(TQ // 8, 8, TQ).sum(axis=0), p)
        return jnp.sum(p, axis=0, keepdims=True)

    def count_ge(t):
        ind = lambda x: jnp.where(x >= t, 1.0, 0.0)
        return count(ind, lambda x, j: ind(x))

    def count_gt(t):
        ind = lambda x: jnp.where(x > t, 1.0, 0.0)
        return count(ind, lambda x, j: ind(x))

    state = jnp.max(jnp.where((cnt_ref[0:1, :] >= kf) | (cnt_ref[1:2, :] < kf), 2.0,
                              jnp.where(cnt_ref[1:2, :] > kf, 1.0, 0.0)))

    @pl.when(state > 1.5)
    def _():
        def bit_body(b, u):
            cand = u + jnp.left_shift(jnp.int32(1), 31 - b)
            return jnp.where(count_ge(_key_to_f32(cand)) >= kf, cand, u)

        u = lax.fori_loop(0, 32, bit_body, jnp.full((1, TQ), INT_MIN, jnp.int32))
        key_ref[...] = u
        cnt_ref[0:1, :] = count_gt(_key_to_f32(u))
        cnt_ref[1:2, :] = count_ge(_key_to_f32(u))

    thr = _key_to_f32(key_ref[...])

    need = kf - cnt_ref[0:1, :]

    @pl.when(state > 0.5)
    def _():
        def count_ties_before(xc):
            return count(lambda x: jnp.where((x == thr) & (mrow < xc), 1.0, 0.0),
                         lambda x, j: jnp.where((x == thr) & (j * TQ + N_META + trow < xc), 1.0, 0.0))

        def pos_body(b, xs):
            cand = xs | jnp.left_shift(jnp.int32(1), 11 - b)
            return jnp.where(count_ties_before(cand) < need, cand, xs)

        last = lax.fori_loop(0, 12, pos_body, jnp.zeros((1, TQ), jnp.int32))
        sm = stm_ref[...]
        stm_ref[...] = jnp.where((sm == thr) & (mrow > last), -jnp.inf, sm)

        def drop_body(j, carry):
            sx = st_ref[j]
            st_ref[j] = jnp.where((sx == thr) & (j * TQ + N_META + trow > last), -jnp.inf, sx)
            return carry

        lax.fori_loop(0, n_tiles, drop_body, 0)

    def d_tile(st_tile, c_tile, ct_tile, bias_get, first=False):
        sel = jnp.where(st_tile >= thr, 0.0, NEG)
        rows = c_tile.shape[0]
        for h in range(B_HEADS):
            sd_ref[h, 0:rows, :] = (jnp.dot(c_tile, qlt_ref[0, 0, :, h * TQ:(h + 1) * TQ],
                                            preferred_element_type=F32)
                                    + (bias_get(A_HEADS + h) + sel))
        for h in range(B_HEADS):
            _online(sd_ref.at[h, 0:rows, :], slice(h * TQ, (h + 1) * TQ), m2_ref, l2_ref, acc2_ref,
                    ct_tile, first)

    d_tile(stm_ref[...], cm_ref[0, 0:N_META, :], ctm_ref[0, 0, :, 0:N_META], lambda h: tmall_ref[h, midx],
           first=True)

    def d_tiles(js):
        d_tile(cat([st_ref[j] for j in js], 0), c_ref[0, key_rows(js), :],
               cat([ct_ref[0, j] for j in js], 1), bias_rows(js))

    walk_tiles(d_tiles)

    for h in range(B_HEADS):
        lanes = slice(h * TQ, (h + 1) * TQ)
        olat = (acc2_ref[:, lanes] * (1.0 / l2_ref[:, lanes])).astype(BF16)
        ob = jnp.dot(wuvt_ref[h], olat, preferred_element_type=F32)
        r0 = A_HEADS * A_DV + h * B_DV
        ot_ref[r0:r0 + B_DV, :] = ob.astype(BF16)

    y = lax.dot_general(ot_ref[...], wout_ref[...], (((0,), (0,)), ((), ())),
                        preferred_element_type=F32)
    o_ref[0] = h_ref[0] + y


def _attention(q_side, h3d, k_side, meta_side, tall, tmall, lamp, subln, wuvt, wout):
    qat, qlt, iqt, iwt = q_side
    ka, vat, c, ct, ik = k_side
    kam, vatm, cm, ctm, ikm = meta_side
    b, s, d = h3d.shape
    nt = s // TQ
    in_specs = [
        pl.BlockSpec((1, qat.shape[1], TQ), lambda bi, i: (bi, 0, i)),
        pl.BlockSpec((1, 1) + qlt.shape[2:], lambda bi, i: (bi, i, 0, 0)),
        pl.BlockSpec((1, 1) + iqt.shape[2:], lambda bi, i: (bi, i, 0, 0)),
        pl.BlockSpec((1, iwt.shape[1], TQ), lambda bi, i: (bi, 0, i)),
        pl.BlockSpec((1, TQ, d), lambda bi, i: (bi, i, 0)),
        pl.BlockSpec((1,) + ka.shape[1:], lambda bi, i: (bi, 0, 0)),
        pl.BlockSpec((1,) + vat.shape[1:], lambda bi, i: (bi, 0, 0, 0)),
        pl.BlockSpec((1,) + c.shape[1:], lambda bi, i: (bi, 0, 0)),
        pl.BlockSpec((1,) + ct.shape[1:], lambda bi, i: (bi, 0, 0, 0)),
        pl.BlockSpec((1,) + ik.shape[1:], lambda bi, i: (bi, 0, 0)),
    ] + [_const_spec(a.shape) for a in (kam, vatm, cm, ctm, ikm, tall, tmall, lamp, subln, wuvt, wout)]
    scratch = [
        pltpu.VMEM((nt, TQ, TQ), F32),
        pltpu.VMEM((N_META, TQ), F32),
        pltpu.VMEM((nt, TQ, TQ), BF16), pltpu.VMEM((N_META, TQ), BF16),
        pltpu.VMEM((nt, TQ, TQ), BF16), pltpu.VMEM((N_META, TQ), BF16),
        pltpu.VMEM((nt, TQ, TQ), BF16), pltpu.VMEM((N_META, TQ), BF16),
        pltpu.VMEM((1, TQ), jnp.int32),
        pltpu.VMEM((8, TQ), F32),
        pltpu.VMEM((A_HEADS, 2 * A_DK, A_LANES), BF16),
        pltpu.VMEM((A_HEADS, KEY_TILES_PER_STEP * TQ, A_LANES), F32),
        pltpu.VMEM((B_HEADS, KEY_TILES_PER_STEP * TQ, TQ), F32),
        pltpu.VMEM((1, A_HEADS * A_LANES), F32),
        pltpu.VMEM((A_DV + PACK, A_HEADS * A_LANES), F32),
        pltpu.VMEM((1, B_HEADS * TQ), F32),
        pltpu.VMEM((1, B_HEADS * TQ), F32),
        pltpu.VMEM((B_DC, B_HEADS * TQ), F32),
        pltpu.VMEM((A_HEADS * A_DV + B_HEADS * B_DV, TQ), BF16),
    ]
    return pl.pallas_call(
        _attn_kernel,
        grid=(b, nt),
        in_specs=in_specs,
        out_specs=pl.BlockSpec((1, TQ, d), lambda bi, i: (bi, i, 0)),
        out_shape=jax.ShapeDtypeStruct((b, s, d), F32),
        scratch_shapes=scratch,
        compiler_params=_params(2),
        name="attention",
    )(qat, qlt, iqt, iwt, h3d, ka, vat, c, ct, ik, kam, vatm, cm, ctm, ikm,
      tall, tmall, lamp, subln, wuvt, wout)


def kernel(x, meta_tokens, rel_bias, ffn1_norm, ffn1_w_gate, ffn1_w_up, ffn1_w_down, mix_norm, w_in, a_q_norm, a_k_norm, a_lambda_q1, a_lambda_k1, a_lambda_q2, a_lambda_k2, a_subln, b_kv_norm, b_w_uk, b_q_norm, b_w_uv, w_out, ffn2_norm, ffn2_w_gate, ffn2_w_up, ffn2_w_down):
    bsz, seq, d = x.shape
    assert d == D_MODEL and seq % PROJ_TM == 0 and (bsz * seq) % FFN_TM == 0
    assert min(IDX_TOPK_MAX, (seq + N_META) // 4) == IDX_TOPK_MAX
    assert ffn1_norm.shape[0] == 1, "single layer"
    lyr = 0
    row = lambda v: v.reshape(1, -1).astype(F32)
    col = lambda v: v.reshape(-1, 1).astype(F32)

    w = w_in[lyr]
    o_qa, o_ka, o_va, o_qb, o_c, o_iq, o_ik, o_iw = np.cumsum(
        [0, 2 * A_HEADS * A_DK, 2 * A_HEADS * A_DK, A_HEADS * A_DV, B_HEADS * B_DQ, B_DC,
         IDX_HEADS * IDX_DIM, IDX_DIM])[:8]
    w_qa, w_ka, w_va, w_qb = w[:, o_qa:o_ka], w[:, o_ka:o_va], w[:, o_va:o_qb], w[:, o_qb:o_c]
    w_c, w_iq, w_ik, w_iw = w[:, o_c:o_iq], w[:, o_iq:o_ik], w[:, o_ik:o_iw], w[:, o_iw:o_iw + IDX_HEADS]
    wt = jnp.concatenate([w_qa, w_va, w_qb, w_c, w_iq, w_iw,
                          jnp.zeros((d, _T_END - _T_IW - IDX_HEADS), w.dtype)], axis=1).T.astype(BF16)
    wr = jnp.concatenate([w_ka, w_c, w_ik, jnp.zeros((d, _R_END - _R_IK - IDX_DIM), w.dtype)],
                         axis=1).astype(BF16)
    wukt = jnp.transpose(b_w_uk[lyr], (0, 2, 1)).astype(BF16)
    wuvt = jnp.transpose(b_w_uv[lyr], (0, 2, 1)).astype(BF16)
    gq = col(jnp.tile(a_q_norm[lyr], 2 * A_HEADS)) * (A_SCALE * LOG2E)
    gk = row(a_k_norm[lyr])
    gql = col(b_q_norm[lyr]) * (B_SCALE * LOG2E)
    gcr, gcc = row(b_kv_norm[lyr]), col(b_kv_norm[lyr])
    subln = col(a_subln[lyr]) * (1.0 - LAM_INIT)
    lamp = jnp.stack([a_lambda_q1[lyr], a_lambda_k1[lyr], a_lambda_q2[lyr], a_lambda_k2[lyr]]).astype(F32)
    proj_consts = (row(mix_norm[lyr]), wt, wr, wukt, gq, gk, gql, gcr, gcc)

    tall, tmall = _bias_tiles(rel_bias.astype(F32))

    ffn1 = (row(ffn1_norm[lyr]), ffn1_w_gate[lyr].astype(BF16), ffn1_w_up[lyr].astype(BF16),
            ffn1_w_down[lyr].astype(BF16))
    h1 = _ffn(x.reshape(bsz * seq, d), *ffn1, tm=FFN_TM).reshape(bsz, seq, d)
    meta_pad = jnp.pad(meta_tokens.astype(x.dtype), ((0, META_PAD - N_META), (0, 0)))
    h1m = _ffn(meta_pad, *ffn1, tm=META_PAD).reshape(1, META_PAD, d)

    qat, ka, vat, qlt, c, ct, iqt, ik, iwt = _inproj(h1, *proj_consts, tm=PROJ_TM, tw=TQ)
    _, kam, vatm, _, cm, ctm, _, ikm, _ = _inproj(h1m, *proj_consts, tm=META_PAD, tw=META_PAD)

    h2 = _attention((qat, qlt, iqt, iwt), h1, (ka, vat, c, ct, ik), (kam, vatm, cm, ctm, ikm),
                    tall, tmall, lamp, subln, wuvt, w_out[lyr].astype(BF16))

    ffn2 = (row(ffn2_norm[lyr]), ffn2_w_gate[lyr].astype(BF16), ffn2_w_up[lyr].astype(BF16),
            ffn2_w_down[lyr].astype(BF16))
    return _ffn(h2.reshape(bsz * seq, d), *ffn2, tm=FFN_TM).reshape(bsz, seq, d)
```

```python
import functools
import math

import jax
import jax.numpy as jnp
import numpy as np
from jax import lax
from jax.experimental import pallas as pl
from jax.experimental.pallas import tpu as pltpu

F32 = jnp.float32
BF16 = jnp.bfloat16

D_MODEL = 1024
N_META = 16
A_HEADS = 4
A_DK = 64
A_DV = 128
B_HEADS = 4
B_DQ = 128
B_DC = 256
B_DV = 128
IDX_HEADS = 8
IDX_DIM = 64
IDX_TOPK_MAX = 256
REL_BUCKETS = 32
REL_MAX_DIST = 128
D_FF = 2816
EPS = 1e-6
LOG2E = math.log2(math.e)
A_SCALE = A_DK ** -0.5
B_SCALE = B_DC ** -0.5
LAM_INIT = 0.8 - 0.6 * math.exp(-0.3 * 0)

NEG = -1e30
INT_MIN = -(2 ** 31)

TQ = 256
META_PAD = 128
FF_CHUNK = D_FF
FFN_TM = 512
PROJ_TM = 512
V7X_VMEM_LIMIT = 56 * 1024 * 1024

KEY_TILES_PER_STEP = 4
A_LANES = 2 * TQ
PACK = 16
COUNT_CHAINS = 4
assert N_META % PACK == 0
_T_QA, _T_VA, _T_QB, _T_C, _T_IQ, _T_IW, _T_END = 0, 512, 1024, 1536, 1792, 2304, 2320
_R_KA, _R_C, _R_IK, _R_END = 0, 512, 768, 896


def _bucket_edges():
    max_exact = REL_BUCKETS // 2
    n = np.arange(0, 4 * REL_MAX_DIST)
    nf = np.maximum(n, 1).astype(np.float64)
    large = max_exact + (np.log(nf / max_exact) / math.log(REL_MAX_DIST / max_exact)
                         * (REL_BUCKETS - max_exact)).astype(np.int64)
    large = np.minimum(large, REL_BUCKETS - 1)
    bucket = np.where(n < max_exact, n, large)
    return [int(n[np.argmax(bucket >= j)]) for j in range(REL_BUCKETS)]


_EDGES = _bucket_edges()
assert _EDGES[-1] <= TQ, "tiles two or more away from the diagonal must sit in the last bucket"


def _const_spec(shape):
    nd = len(shape)
    return pl.BlockSpec(shape, lambda *_: (0,) * nd, pipeline_mode=pl.Buffered(1))


def _params(n_grid):
    return pltpu.CompilerParams(dimension_semantics=("arbitrary",) * n_grid,
                                vmem_limit_bytes=V7X_VMEM_LIMIT)


def _rms_lanes(x, g_row):
    ms = jnp.mean(x * x, axis=-1, keepdims=True)
    return x * lax.rsqrt(ms + EPS) * g_row


def _bias_kernel(rb_ref, tall_ref, tmall_ref):
    h = pl.program_id(0)

    def table(d):
        val = jnp.full(d.shape, rb_ref[0, h] * LOG2E, F32)
        for j in range(1, REL_BUCKETS):
            val = jnp.where(d >= _EDGES[j], rb_ref[j, h] * LOG2E, val)
        return val

    far = rb_ref[REL_BUCKETS - 1, h] * LOG2E
    c = lax.broadcasted_iota(jnp.int32, (TQ, TQ), 0)
    r = lax.broadcasted_iota(jnp.int32, (TQ, TQ), 1)
    d0 = r - c
    tall_ref[0, 0] = jnp.where(d0 >= 0, table(d0), NEG)
    tall_ref[0, 1] = table(d0 + TQ)
    tall_ref[0, 2] = jnp.full((TQ, TQ), far, F32)
    m = lax.broadcasted_iota(jnp.int32, (N_META, TQ), 0)
    r2 = lax.broadcasted_iota(jnp.int32, (N_META, TQ), 1)
    tmall_ref[0, 0] = table(r2 + N_META - m)
    tmall_ref[0, 1] = jnp.full((N_META, TQ), far, F32)


def _bias_tiles(rel_bias):
    nh = rel_bias.shape[1]
    return pl.pallas_call(
        _bias_kernel,
        grid=(nh,),
        in_specs=[pl.BlockSpec(memory_space=pltpu.SMEM)],
        out_specs=[pl.BlockSpec((1, 3, TQ, TQ), lambda h: (h, 0, 0, 0)),
                   pl.BlockSpec((1, 2, N_META, TQ), lambda h: (h, 0, 0, 0))],
        out_shape=[jax.ShapeDtypeStruct((nh, 3, TQ, TQ), F32),
                   jax.ShapeDtypeStruct((nh, 2, N_META, TQ), F32)],
        compiler_params=_params(1),
        name="bias_tiles",
    )(rel_bias)


def _ffn_kernel(x_ref, g_ref, wg_ref, wu_ref, wd_ref, o_ref):
    x = x_ref[...]
    xn = _rms_lanes(x, g_ref[...]).astype(BF16)
    y = None
    for c in range(D_FF // FF_CHUNK):
        sl = slice(c * FF_CHUNK, (c + 1) * FF_CHUNK)
        g = jnp.dot(xn, wg_ref[:, sl], preferred_element_type=F32)
        u = jnp.dot(xn, wu_ref[:, sl], preferred_element_type=F32)
        a = (g * jax.nn.sigmoid(g) * u).astype(BF16)
        part = jnp.dot(a, wd_ref[sl, :], preferred_element_type=F32)
        y = part if y is None else y + part
    o_ref[...] = x + 0.5 * y


def _ffn(h2d, g_row, wg, wu, wd, tm):
    m, d = h2d.shape
    return pl.pallas_call(
        _ffn_kernel,
        grid=(m // tm,),
        in_specs=[pl.BlockSpec((tm, d), lambda i: (i, 0)),
                  _const_spec((1, d)),
                  _const_spec(wg.shape), _const_spec(wu.shape), _const_spec(wd.shape)],
        out_specs=pl.BlockSpec((tm, d), lambda i: (i, 0)),
        out_shape=jax.ShapeDtypeStruct((m, d), F32),
        compiler_params=_params(1),
        name="ffn",
    )(h2d, g_row, wg, wu, wd)


def _inproj_kernel(h_ref, g_ref, wt_ref, wr_ref, wukt_ref, gq_ref, gk_ref, gql_ref, gcr_ref, gcc_ref,
                   qat_ref, ka_ref, vat_ref, qlt_ref, c_ref, ct_ref, iqt_ref, ik_ref, iwt_ref, *, tm, tw):
    xn = _rms_lanes(h_ref[0], g_ref[...]).astype(BF16)
    subtiles = [(t, slice(t * tw, (t + 1) * tw)) for t in range(tm // tw)]

    def proj_t(r0, r1):
        return lax.dot_general(wt_ref[r0:r1, :], xn, (((1,), (1,)), ((), ())), preferred_element_type=F32)

    def proj_r(c0, c1):
        return jnp.dot(xn, wr_ref[:, c0:c1], preferred_element_type=F32)

    ka = proj_r(_R_KA, _R_C)
    for hm in range(2 * A_HEADS):
        cols = slice(hm * A_DK, (hm + 1) * A_DK)
        ka_ref[0, :, cols] = _rms_lanes(ka[:, cols], gk_ref[...]).astype(BF16)

    qa = proj_t(_T_QA, _T_VA).reshape(2 * A_HEADS, A_DK, tm)
    qa = qa * lax.rsqrt(jnp.mean(qa * qa, axis=1, keepdims=True) + EPS)
    qat_ref[0] = (qa.reshape(2 * A_HEADS * A_DK, tm) * gq_ref[...]).astype(BF16)

    qb = proj_t(_T_QB, _T_C).astype(BF16)
    for h in range(B_HEADS):
        ql = jnp.dot(wukt_ref[h], qb[h * B_DQ:(h + 1) * B_DQ], preferred_element_type=F32)
        ql = (ql * lax.rsqrt(jnp.mean(ql * ql, axis=0, keepdims=True) + EPS) * gql_ref[...]).astype(BF16)
        for t, sl in subtiles:
            qlt_ref[0, t, :, h * tw:(h + 1) * tw] = ql[:, sl]

    c_ref[0] = _rms_lanes(proj_r(_R_C, _R_IK), gcr_ref[...]).astype(BF16)

    ct = proj_t(_T_C, _T_IQ)
    ct = (ct * lax.rsqrt(jnp.mean(ct * ct, axis=0, keepdims=True) + EPS) * gcc_ref[...]).astype(BF16)
    for t, sl in subtiles:
        ct_ref[0, t] = ct[:, sl]

    va = proj_t(_T_VA, _T_QB).astype(BF16)
    for t, sl in subtiles:
        vat_ref[0, t] = va[:, sl]

    iq = proj_t(_T_IQ, _T_END)
    for h in range(IDX_HEADS):
        iqh = iq[h * IDX_DIM:(h + 1) * IDX_DIM].astype(BF16)
        for t, sl in subtiles:
            iqt_ref[0, t, :, h * tw:(h + 1) * tw] = iqh[:, sl]
    iw0 = _T_IW - _T_IQ
    iwt_ref[0] = iq[iw0:iw0 + IDX_HEADS] * (IDX_HEADS ** -0.5 * IDX_DIM ** -0.5)

    ik_ref[0] = proj_r(_R_IK, _R_END)[:, 0:IDX_DIM].astype(BF16)


def _inproj(h3d, g_row, wt, wr, wukt, gq, gk, gql, gcr, gcc, tm, tw):
    b, s, d = h3d.shape
    nt = s // tw
    kern = functools.partial(_inproj_kernel, tm=tm, tw=tw)
    out_shape = [
        jax.ShapeDtypeStruct((b, 2 * A_HEADS * A_DK, s), BF16),
        jax.ShapeDtypeStruct((b, s, 2 * A_HEADS * A_DK), BF16),
        jax.ShapeDtypeStruct((b, nt, A_HEADS * A_DV, tw), BF16),
        jax.ShapeDtypeStruct((b, nt, B_DC, B_HEADS * tw), BF16),
        jax.ShapeDtypeStruct((b, s, B_DC), BF16),
        jax.ShapeDtypeStruct((b, nt, B_DC, tw), BF16),
        jax.ShapeDtypeStruct((b, nt, IDX_DIM, IDX_HEADS * tw), BF16),
        jax.ShapeDtypeStruct((b, s, IDX_DIM), BF16),
        jax.ShapeDtypeStruct((b, IDX_HEADS, s), F32),
    ]
    k = tm // tw
    out_specs = [
        pl.BlockSpec((1, 2 * A_HEADS * A_DK, tm), lambda bi, i: (bi, 0, i)),
        pl.BlockSpec((1, tm, 2 * A_HEADS * A_DK), lambda bi, i: (bi, i, 0)),
        pl.BlockSpec((1, k, A_HEADS * A_DV, tw), lambda bi, i: (bi, i, 0, 0)),
        pl.BlockSpec((1, k, B_DC, B_HEADS * tw), lambda bi, i: (bi, i, 0, 0)),
        pl.BlockSpec((1, tm, B_DC), lambda bi, i: (bi, i, 0)),
        pl.BlockSpec((1, k, B_DC, tw), lambda bi, i: (bi, i, 0, 0)),
        pl.BlockSpec((1, k, IDX_DIM, IDX_HEADS * tw), lambda bi, i: (bi, i, 0, 0)),
        pl.BlockSpec((1, tm, IDX_DIM), lambda bi, i: (bi, i, 0)),
        pl.BlockSpec((1, IDX_HEADS, tm), lambda bi, i: (bi, 0, i)),
    ]
    consts = (g_row, wt, wr, wukt, gq, gk, gql, gcr, gcc)
    return pl.pallas_call(
        kern,
        grid=(b, s // tm),
        in_specs=[pl.BlockSpec((1, tm, d), lambda bi, i: (bi, i, 0))] + [_const_spec(a.shape) for a in consts],
        out_specs=out_specs,
        out_shape=out_shape,
        compiler_params=_params(2),
        name="inproj",
    )(h3d, *consts)


def _key_to_f32(u):
    bits = u ^ (jnp.right_shift(u, 31) & 0x7FFFFFFF)
    return lax.bitcast_convert_type(bits, F32)


def _online(s_ref, lanes, m_ref, l_ref, acc_ref, v_lhs, first=False):
    if first:
        m_new = jnp.max(s_ref[...], axis=0, keepdims=True)
        m_ref[:, lanes] = m_new
        p = jnp.exp2(s_ref[...] - m_new)
        if l_ref is not None:
            l_ref[:, lanes] = jnp.sum(p, axis=0, keepdims=True)
        acc_ref[:, lanes] = jnp.dot(v_lhs, p.astype(BF16), preferred_element_type=F32)
        return
    m_old = m_ref[:, lanes]
    m_new = jnp.maximum(m_old, jnp.max(s_ref[...], axis=0, keepdims=True))
    m_ref[:, lanes] = m_new
    alpha = jnp.exp2(m_old - m_new)
    p = jnp.exp2(s_ref[...] - m_new)
    if l_ref is not None:
        l_ref[:, lanes] = alpha * l_ref[:, lanes] + jnp.sum(p, axis=0, keepdims=True)
    acc_ref[:, lanes] = alpha * acc_ref[:, lanes] + jnp.dot(v_lhs, p.astype(BF16),
                                                           preferred_element_type=F32)


def _attn_kernel(qat_ref, qlt_ref, iqt_ref, iwt_ref, h_ref,
                 ka_ref, vat_ref, c_ref, ct_ref, ik_ref,
                 kam_ref, vatm_ref, cm_ref, ctm_ref, ikm_ref,
                 tall_ref, tmall_ref, lamp_ref, subln_ref, wuvt_ref, wout_ref,
                 o_ref,
                 st_ref, stm_ref, hi_ref, him_ref, b1_ref, b1m_ref, b0_ref, b0m_ref,
                 key_ref, cnt_ref, qbd_ref, sa_ref, sd_ref, m_ref, acc_ref, m2_ref, acc2_ref, ot_ref):
    i = pl.program_id(1)
    n_tiles = i + 1
    midx = jnp.minimum(i, 1)

    zq = jnp.zeros((A_DK, TQ), BF16)
    for h in range(A_HEADS):
        q0 = qat_ref[0, (2 * h) * A_DK:(2 * h + 1) * A_DK, :]
        q1 = qat_ref[0, (2 * h + 1) * A_DK:(2 * h + 2) * A_DK, :]
        qbd_ref[h, 0:A_DK, 0:TQ] = q0
        qbd_ref[h, 0:A_DK, TQ:A_LANES] = zq
        qbd_ref[h, A_DK:2 * A_DK, 0:TQ] = zq
        qbd_ref[h, A_DK:2 * A_DK, TQ:A_LANES] = q1

    def a_tile(k_get, v_get, bias_get, first=False):
        rows = v_get(0).shape[1]
        for h in range(A_HEADS):
            bias = bias_get(h)
            sa_ref[h, 0:rows, :] = (jnp.dot(k_get(h), qbd_ref[h], preferred_element_type=F32)
                                    + jnp.concatenate([bias, bias], axis=1))
        ones = jnp.ones((PACK, rows), BF16)
        for h in range(A_HEADS):
            _online(sa_ref.at[h, 0:rows, :], slice(h * A_LANES, (h + 1) * A_LANES),
                    m_ref, None, acc_ref, jnp.concatenate([v_get(h), ones], axis=0), first)

    qrel = i * TQ + lax.broadcasted_iota(jnp.int32, (1, TQ), 1)

    def idx_scores(ik_tile):
        d = jnp.dot(ik_tile, iqt_ref[0, 0], preferred_element_type=F32)
        acc = jnp.zeros((ik_tile.shape[0], TQ), F32)
        for h in range(IDX_HEADS):
            acc = acc + iwt_ref[0, h:h + 1, :] * jnp.maximum(d[:, h * TQ:(h + 1) * TQ], 0.0)
        return acc

    def store_digits(st, hi_dst, b1_dst, b0_dst):
        bits = lax.bitcast_convert_type(st, jnp.int32)
        key = bits ^ (jnp.right_shift(bits, 31) & 0x7FFFFFFF)
        hi_dst[...] = lax.bitcast_convert_type(bits & jnp.int32(-65536), F32).astype(BF16)
        b1_dst[...] = (jnp.right_shift(key, 8) & 255).astype(F32).astype(BF16)
        b0_dst[...] = (key & 255).astype(F32).astype(BF16)

    mrow = lax.broadcasted_iota(jnp.int32, (N_META, TQ), 0)
    trow = lax.broadcasted_iota(jnp.int32, (TQ, TQ), 0)

    a_tile(lambda h: kam_ref[0, 0:N_META, h * 2 * A_DK:(h + 1) * 2 * A_DK],
           lambda h: vatm_ref[0, 0, h * A_DV:(h + 1) * A_DV, 0:N_META],
           lambda h: tmall_ref[h, midx], first=True)
    sm = idx_scores(ikm_ref[0, 0:N_META, :])
    stm_ref[...] = sm
    store_digits(sm, him_ref, b1m_ref, b0m_ref)

    assert KEY_TILES_PER_STEP == 4
    n_quads = jnp.right_shift(n_tiles, 2)
    has_pair = (n_tiles & 2) != 0
    has_tail = (n_tiles & 1) != 0
    cat = lambda parts, axis: parts[0] if len(parts) == 1 else jnp.concatenate(parts, axis=axis)

    def walk_tiles(visit):
        def body(q, carry):
            visit([KEY_TILES_PER_STEP * q + t for t in range(KEY_TILES_PER_STEP)])
            return carry

        lax.fori_loop(0, n_quads, body, 0)

        @pl.when(has_pair)
        def _():
            visit([KEY_TILES_PER_STEP * n_quads, KEY_TILES_PER_STEP * n_quads + 1])

        @pl.when(has_tail)
        def _():
            visit([n_tiles - 1])

    def key_rows(js):
        return pl.ds(pl.multiple_of(js[0] * TQ, TQ), len(js) * TQ)

    def bias_rows(js):
        kidx = [jnp.minimum(i - j, 2) for j in js]
        return lambda h: cat([tall_ref[h, k] for k in kidx], 0)

    def ab_tiles(js):
        rows = key_rows(js)
        a_tile(lambda h: ka_ref[0, rows, h * 2 * A_DK:(h + 1) * 2 * A_DK],
               lambda h: cat([vat_ref[0, j, h * A_DV:(h + 1) * A_DV, :] for j in js], 1),
               bias_rows(js))
        for t0 in range(0, len(js), 2):
            jj = js[t0:t0 + 2]
            sc = idx_scores(ik_ref[0, key_rows(jj), :])
            for t, j in enumerate(jj):
                sx = jnp.where(j * TQ + trow <= qrel, sc[t * TQ:(t + 1) * TQ], -jnp.inf)
                st_ref[j] = sx
                store_digits(sx, hi_ref.at[j], b1_ref.at[j], b0_ref.at[j])

    walk_tiles(ab_tiles)

    lp = lamp_ref[...]
    lam = (jnp.exp(jnp.sum(lp[0:1] * lp[1:2], axis=-1, keepdims=True))
           - jnp.exp(jnp.sum(lp[2:3] * lp[3:4], axis=-1, keepdims=True)) + LAM_INIT)
    for h in range(A_HEADS):
        l0 = slice(h * A_LANES, h * A_LANES + TQ)
        l1 = slice(h * A_LANES + TQ, (h + 1) * A_LANES)
        den0, den1 = acc_ref[A_DV:A_DV + 1, l0], acc_ref[A_DV:A_DV + 1, l1]
        o = acc_ref[0:A_DV, l0] * (1.0 / den0) - acc_ref[0:A_DV, l1] * (lam / den1)
        o = o * lax.rsqrt(jnp.mean(o * o, axis=0, keepdims=True) + EPS) * subln_ref[...]
        ot_ref[h * A_DV:(h + 1) * A_DV, :] = o.astype(BF16)

    kf = jnp.minimum(IDX_TOPK_MAX, qrel + (N_META + 1)).astype(F32)
    one, zero = jnp.full((), 1, BF16), jnp.full((), 0, BF16)

    ge = lambda x, t: x >= t
    gt = lambda x, t: x > t
    as_f32 = lambda d: d.astype(F32)

    def top_half_value(w):
        k = w - 32768
        b16 = (k ^ (jnp.right_shift(k, 15) & 0x7FFF)) & 0xFFFF
        return lax.bitcast_convert_type(jnp.left_shift(b16, 16), F32)

    def find_threshold(n):
        def pk_count(meta_ref, tile_ref, pred, t_row):
            tb = jnp.broadcast_to(t_row.astype(BF16), (PACK, TQ))[None]
            accs = [None] * COUNT_CHAINS
            k = 0
            for x in [meta_ref[...]] + [tile_ref[j] for j in range(n)]:
                x3 = x.reshape(x.shape[0] // PACK, PACK, TQ)
                ind = jnp.where(pred(x3, tb), one, zero)
                for r in range(ind.shape[0]):
                    c = k % COUNT_CHAINS
                    accs[c] = ind[r] if accs[c] is None else accs[c] + ind[r]
                    k += 1
            acc = functools.reduce(lambda a, b: a + b, [a for a in accs if a is not None])
            return jnp.sum(acc.astype(F32), axis=0, keepdims=True)

        def pk_keep(meta_ref, tile_ref, sel_meta_ref, sel_tile_ref, t_row):
            tb = jnp.broadcast_to(t_row.astype(BF16), (1, TQ))
            meta_ref[...] = jnp.where(sel_meta_ref[...] == tb, meta_ref[...], -one)
            for j in range(n):
                tile_ref[j] = jnp.where(sel_tile_ref[j] == tb, tile_ref[j], -one)

        def bisect(n_bits, to_value, meta_ref, tile_ref, need):
            def body(b, w):
                cand = w | jnp.left_shift(jnp.int32(1), n_bits - 1 - b)
                return jnp.where(pk_count(meta_ref, tile_ref, ge, to_value(cand)) >= need, cand, w)

            return lax.fori_loop(0, n_bits, body, jnp.zeros((1, TQ), jnp.int32))

        w_hi = bisect(16, top_half_value, him_ref, hi_ref, kf)
        need1 = kf - pk_count(him_ref, hi_ref, gt, top_half_value(w_hi))
        pk_keep(b1m_ref, b1_ref, him_ref, hi_ref, top_half_value(w_hi))
        d1 = bisect(8, as_f32, b1m_ref, b1_ref, need1)
        need0 = need1 - pk_count(b1m_ref, b1_ref, gt, as_f32(d1))
        pk_keep(b0m_ref, b0_ref, b1m_ref, b1_ref, as_f32(d1))
        d0 = bisect(8, as_f32, b0m_ref, b0_ref, need0)
        return jnp.left_shift(w_hi - 32768, 16) | jnp.left_shift(d1, 8) | d0

    assert PACK * st_ref.shape[0] + N_META // PACK <= 256, "bf16 counters must stay exact"
    for n in range(1, st_ref.shape[0] + 1):
        @pl.when(n_tiles == n)
        def _(n=n):
            key = find_threshold(n)
            key_ref[...] = key
            t = _key_to_f32(key)
            gts, ges = [None] * COUNT_CHAINS, [None] * COUNT_CHAINS
            k = 0
            for x in [stm_ref[...]] + [st_ref[j] for j in range(n)]:
                x3 = x.reshape(x.shape[0] // 8, 8, TQ)
                for r in range(x3.shape[0]):
                    c = k % COUNT_CHAINS
                    gt1, ge1 = jnp.where(x3[r] > t, 1.0, 0.0), jnp.where(x3[r] >= t, 1.0, 0.0)
                    gts[c] = gt1 if gts[c] is None else gts[c] + gt1
                    ges[c] = ge1 if ges[c] is None else ges[c] + ge1
                    k += 1
            total = lambda parts: jnp.sum(functools.reduce(lambda a, b: a + b, parts), axis=0, keepdims=True)
            cnt_ref[0:1, :] = total(gts)
            cnt_ref[1:2, :] = total(ges)

    def count(ind_meta, ind_tile):
        p = ind_meta(stm_ref[...]).reshape(N_META // 8, 8, TQ).sum(axis=0)
        p = lax.fori_loop(0, n_tiles,
                          lambda j, p: p + ind_tile(st_ref[j], j).reshape(TQ // 8, 8, TQ).sum(axis=0), p)
        return jnp.sum(p, axis=0, keepdims=True)

    def count_ge(t):
        ind = lambda x: jnp.where(x >= t, 1.0, 0.0)
        return count(ind, lambda x, j: ind(x))

    def count_gt(t):
        ind = lambda x: jnp.where(x > t, 1.0, 0.0)
        return count(ind, lambda x, j: ind(x))

    state = jnp.max(jnp.where((cnt_ref[0:1, :] >= kf) | (cnt_ref[1:2, :] < kf), 2.0,
                              jnp.where(cnt_ref[1:2, :] > kf, 1.0, 0.0)))

    @pl.when(state > 1.5)
    def _():
        def bit_body(b, u):
            cand = u + jnp.left_shift(jnp.int32(1), 31 - b)
            return jnp.where(count_ge(_key_to_f32(cand)) >= kf, cand, u)

        u = lax.fori_loop(0, 32, bit_body, jnp.full((1, TQ), INT_MIN, jnp.int32))
        key_ref[...] = u
        cnt_ref[0:1, :] = count_gt(_key_to_f32(u))
        cnt_ref[1:2, :] = count_ge(_key_to_f32(u))

    thr = _key_to_f32(key_ref[...])

    need = kf - cnt_ref[0:1, :]

    @pl.when(state > 0.5)
    def _():
        def count_ties_before(xc):
            return count(lambda x: jnp.where((x == thr) & (mrow < xc), 1.0, 0.0),
                         lambda x, j: jnp.where((x == thr) & (j * TQ + N_META + trow < xc), 1.0, 0.0))

        def pos_body(b, xs):
            cand = xs | jnp.left_shift(jnp.int32(1), 11 - b)
            return jnp.where(count_ties_before(cand) < need, cand, xs)

        last = lax.fori_loop(0, 12, pos_body, jnp.zeros((1, TQ), jnp.int32))
        sm = stm_ref[...]
        stm_ref[...] = jnp.where((sm == thr) & (mrow > last), -jnp.inf, sm)

        def drop_body(j, carry):
            sx = st_ref[j]
            st_ref[j] = jnp.where((sx == thr) & (j * TQ + N_META + trow > last), -jnp.inf, sx)
            return carry

        lax.fori_loop(0, n_tiles, drop_body, 0)

    def d_tile(st_tile, c_tile, ct_tile, bias_get, first=False):
        sel = jnp.where(st_tile >= thr, 0.0, NEG)
        rows = c_tile.shape[0]
        for h in range(B_HEADS):
            sd_ref[h, 0:rows, :] = (jnp.dot(c_tile, qlt_ref[0, 0, :, h * TQ:(h + 1) * TQ],
                                            preferred_element_type=F32)
                                    + (bias_get(A_HEADS + h) + sel))
        ct_ones = jnp.concatenate([ct_tile, jnp.ones((PACK, rows), BF16)], axis=0)
        for h in range(B_HEADS):
            _online(sd_ref.at[h, 0:rows, :], slice(h * TQ, (h + 1) * TQ), m2_ref, None, acc2_ref,
                    ct_ones, first)

    d_tile(stm_ref[...], cm_ref[0, 0:N_META, :], ctm_ref[0, 0, :, 0:N_META], lambda h: tmall_ref[h, midx],
           first=True)

    def d_tiles(js):
        d_tile(cat([st_ref[j] for j in js], 0), c_ref[0, key_rows(js), :],
               cat([ct_ref[0, j] for j in js], 1), bias_rows(js))

    walk_tiles(d_tiles)

    for h in range(B_HEADS):
        lanes = slice(h * TQ, (h + 1) * TQ)
        olat = (acc2_ref[0:B_DC, lanes] * (1.0 / acc2_ref[B_DC:B_DC + 1, lanes])).astype(BF16)
        ob = jnp.dot(wuvt_ref[h], olat, preferred_element_type=F32)
        r0 = A_HEADS * A_DV + h * B_DV
        ot_ref[r0:r0 + B_DV, :] = ob.astype(BF16)

    y = lax.dot_general(ot_ref[...], wout_ref[...], (((0,), (0,)), ((), ())),
                        preferred_element_type=F32)
    o_ref[0] = h_ref[0] + y


def _attention(q_side, h3d, k_side, meta_side, tall, tmall, lamp, subln, wuvt, wout):
    qat, qlt, iqt, iwt = q_side
    ka, vat, c, ct, ik = k_side
    kam, vatm, cm, ctm, ikm = meta_side
    b, s, d = h3d.shape
    nt = s // TQ
    in_specs = [
        pl.BlockSpec((1, qat.shape[1], TQ), lambda bi, i: (bi, 0, i)),
        pl.BlockSpec((1, 1) + qlt.shape[2:], lambda bi, i: (bi, i, 0, 0)),
        pl.BlockSpec((1, 1) + iqt.shape[2:], lambda bi, i: (bi, i, 0, 0)),
        pl.BlockSpec((1, iwt.shape[1], TQ), lambda bi, i: (bi, 0, i)),
        pl.BlockSpec((1, TQ, d), lambda bi, i: (bi, i, 0)),
        pl.BlockSpec((1,) + ka.shape[1:], lambda bi, i: (bi, 0, 0)),
        pl.BlockSpec((1,) + vat.shape[1:], lambda bi, i: (bi, 0, 0, 0)),
        pl.BlockSpec((1,) + c.shape[1:], lambda bi, i: (bi, 0, 0)),
        pl.BlockSpec((1,) + ct.shape[1:], lambda bi, i: (bi, 0, 0, 0)),
        pl.BlockSpec((1,) + ik.shape[1:], lambda bi, i: (bi, 0, 0)),
    ] + [_const_spec(a.shape) for a in (kam, vatm, cm, ctm, ikm, tall, tmall, lamp, subln, wuvt, wout)]
    scratch = [
        pltpu.VMEM((nt, TQ, TQ), F32),
        pltpu.VMEM((N_META, TQ), F32),
        pltpu.VMEM((nt, TQ, TQ), BF16), pltpu.VMEM((N_META, TQ), BF16),
        pltpu.VMEM((nt, TQ, TQ), BF16), pltpu.VMEM((N_META, TQ), BF16),
        pltpu.VMEM((nt, TQ, TQ), BF16), pltpu.VMEM((N_META, TQ), BF16),
        pltpu.VMEM((1, TQ), jnp.int32),
        pltpu.VMEM((8, TQ), F32),
        pltpu.VMEM((A_HEADS, 2 * A_DK, A_LANES), BF16),
        pltpu.VMEM((A_HEADS, KEY_TILES_PER_STEP * TQ, A_LANES), F32),
        pltpu.VMEM((B_HEADS, KEY_TILES_PER_STEP * TQ, TQ), F32),
        pltpu.VMEM((1, A_HEADS * A_LANES), F32),
        pltpu.VMEM((A_DV + PACK, A_HEADS * A_LANES), F32),
        pltpu.VMEM((1, B_HEADS * TQ), F32),
        pltpu.VMEM((B_DC + PACK, B_HEADS * TQ), F32),
        pltpu.VMEM((A_HEADS * A_DV + B_HEADS * B_DV, TQ), BF16),
    ]
    return pl.pallas_call(
        _attn_kernel,
        grid=(b, nt),
        in_specs=in_specs,
        out_specs=pl.BlockSpec((1, TQ, d), lambda bi, i: (bi, i, 0)),
        out_shape=jax.ShapeDtypeStruct((b, s, d), F32),
        scratch_shapes=scratch,
        compiler_params=_params(2),
        name="attention",
    )(qat, qlt, iqt, iwt, h3d, ka, vat, c, ct, ik, kam, vatm, cm, ctm, ikm,
      tall, tmall, lamp, subln, wuvt, wout)


def kernel(x, meta_tokens, rel_bias, ffn1_norm, ffn1_w_gate, ffn1_w_up, ffn1_w_down, mix_norm, w_in, a_q_norm, a_k_norm, a_lambda_q1, a_lambda_k1, a_lambda_q2, a_lambda_k2, a_subln, b_kv_norm, b_w_uk, b_q_norm, b_w_uv, w_out, ffn2_norm, ffn2_w_gate, ffn2_w_up, ffn2_w_down):
    bsz, seq, d = x.shape
    assert d == D_MODEL and seq % PROJ_TM == 0 and (bsz * seq) % FFN_TM == 0
    assert min(IDX_TOPK_MAX, (seq + N_META) // 4) == IDX_TOPK_MAX
    assert ffn1_norm.shape[0] == 1, "single layer"
    lyr = 0
    row = lambda v: v.reshape(1, -1).astype(F32)
    col = lambda v: v.reshape(-1, 1).astype(F32)

    w = w_in[lyr]
    o_qa, o_ka, o_va, o_qb, o_c, o_iq, o_ik, o_iw = np.cumsum(
        [0, 2 * A_HEADS * A_DK, 2 * A_HEADS * A_DK, A_HEADS * A_DV, B_HEADS * B_DQ, B_DC,
         IDX_HEADS * IDX_DIM, IDX_DIM])[:8]
    w_qa, w_ka, w_va, w_qb = w[:, o_qa:o_ka], w[:, o_ka:o_va], w[:, o_va:o_qb], w[:, o_qb:o_c]
    w_c, w_iq, w_ik, w_iw = w[:, o_c:o_iq], w[:, o_iq:o_ik], w[:, o_ik:o_iw], w[:, o_iw:o_iw + IDX_HEADS]
    wt = jnp.concatenate([w_qa, w_va, w_qb, w_c, w_iq, w_iw,
                          jnp.zeros((d, _T_END - _T_IW - IDX_HEADS), w.dtype)], axis=1).T.astype(BF16)
    wr = jnp.concatenate([w_ka, w_c, w_ik, jnp.zeros((d, _R_END - _R_IK - IDX_DIM), w.dtype)],
                         axis=1).astype(BF16)
    wukt = jnp.transpose(b_w_uk[lyr], (0, 2, 1)).astype(BF16)
    wuvt = jnp.transpose(b_w_uv[lyr], (0, 2, 1)).astype(BF16)
    gq = col(jnp.tile(a_q_norm[lyr], 2 * A_HEADS)) * (A_SCALE * LOG2E)
    gk = row(a_k_norm[lyr])
    gql = col(b_q_norm[lyr]) * (B_SCALE * LOG2E)
    gcr, gcc = row(b_kv_norm[lyr]), col(b_kv_norm[lyr])
    subln = col(a_subln[lyr]) * (1.0 - LAM_INIT)
    lamp = jnp.stack([a_lambda_q1[lyr], a_lambda_k1[lyr], a_lambda_q2[lyr], a_lambda_k2[lyr]]).astype(F32)
    proj_consts = (row(mix_norm[lyr]), wt, wr, wukt, gq, gk, gql, gcr, gcc)

    tall, tmall = _bias_tiles(rel_bias.astype(F32))

    ffn1 = (row(ffn1_norm[lyr]), ffn1_w_gate[lyr].astype(BF16), ffn1_w_up[lyr].astype(BF16),
            ffn1_w_down[lyr].astype(BF16))
    h1 = _ffn(x.reshape(bsz * seq, d), *ffn1, tm=FFN_TM).reshape(bsz, seq, d)
    meta_pad = jnp.pad(meta_tokens.astype(x.dtype), ((0, META_PAD - N_META), (0, 0)))
    h1m = _ffn(meta_pad, *ffn1, tm=META_PAD).reshape(1, META_PAD, d)

    qat, ka, vat, qlt, c, ct, iqt, ik, iwt = _inproj(h1, *proj_consts, tm=PROJ_TM, tw=TQ)
    _, kam, vatm, _, cm, ctm, _, ikm, _ = _inproj(h1m, *proj_consts, tm=META_PAD, tw=META_PAD)

    h2 = _attention((qat, qlt, iqt, iwt), h1, (ka, vat, c, ct, ik), (kam, vatm, cm, ctm, ikm),
                    tall, tmall, lamp, subln, wuvt, w_out[lyr].astype(BF16))

    ffn2 = (row(ffn2_norm[lyr]), ffn2_w_gate[lyr].astype(BF16), ffn2_w_up[lyr].astype(BF16),
            ffn2_w_down[lyr].astype(BF16))
    return _ffn(h2.reshape(bsz * seq, d), *ffn2, tm=FFN_TM).reshape(bsz, seq, d)
```
